```python
import math, functools
import jax, jax.numpy as jnp
from jax import lax
import numpy as np

D_MODEL = 4096
BATCH = 8
SEQ = 2048
DEPTH = 1
DEC_BATCH = 32
DEC_SEQ = 32
PAST_LEN = 2048

CHUNK = 64
GMLP_CHUNK = 128
D_A = D_MODEL // 2
G_A = 8
HEAD_DIM = 128
N_HEADS = D_MODEL // 256
N_KV_HEADS = N_HEADS // 4
ROT_DIM = HEAD_DIM // 4
N_IDX_HEADS = 16
IDX_DIM = 64
IDX_ROT = IDX_DIM // 4
TOPK_MAX = 256
QBLOCK = 64
ROPE_THETA = 500000.0
D_FF = 256 * ((8 * D_MODEL // 3 + 255) // 256)
CONV_WIDTH = 3
EPS = 1e-6
IN_SIZES = (D_A, D_A, N_HEADS * HEAD_DIM, N_KV_HEADS * HEAD_DIM, N_KV_HEADS * HEAD_DIM,
            N_IDX_HEADS * IDX_DIM, IDX_DIM, N_IDX_HEADS, D_MODEL, D_MODEL)
IN_COLS = sum(IN_SIZES)

kernel_name = "hybrid_gmlp_dsa_convffn_stream_step"


def rms_norm(x, g):
    xf = x.astype(jnp.float32)
    y = xf * lax.rsqrt(jnp.mean(xf * xf, axis=-1, keepdims=True) + EPS)
    return (y * g.astype(jnp.float32)).astype(x.dtype)


def partial_rope(x, pos, rot_dim):
    half = rot_dim // 2
    inv_freq = ROPE_THETA ** (-jnp.arange(half, dtype=jnp.float32) / half)
    ang = pos.astype(jnp.float32)[:, None] * inv_freq[None, :]
    cos = jnp.cos(ang)[None, :, None, :]
    sin = jnp.sin(ang)[None, :, None, :]
    xf = x.astype(jnp.float32)
    x1 = xf[..., :half]
    x2 = xf[..., half:rot_dim]
    out = jnp.concatenate([x1 * cos - x2 * sin, x2 * cos + x1 * sin, xf[..., rot_dim:]], axis=-1)
    return out.astype(x.dtype)


def project(h, w_in, pos):
    B, T, _ = h.shape
    offs = [int(o) for o in np.cumsum(IN_SIZES)[:-1]]
    u_a, v_a, q, k, v, q_idx, k_idx, w_idx, g_a, g_b = jnp.split(h @ w_in, offs, axis=-1)
    q = partial_rope(q.reshape(B, T, N_HEADS, HEAD_DIM), pos, ROT_DIM)
    k = partial_rope(k.reshape(B, T, N_KV_HEADS, HEAD_DIM), pos, ROT_DIM)
    v = v.reshape(B, T, N_KV_HEADS, HEAD_DIM)
    q_idx = partial_rope(q_idx.reshape(B, T, N_IDX_HEADS, IDX_DIM), pos, IDX_ROT)
    k_idx = partial_rope(k_idx[:, :, None, :], pos, IDX_ROT)[:, :, 0]
    w_idx = w_idx * (N_IDX_HEADS ** -0.5 * IDX_DIM ** -0.5)
    return u_a, v_a, q, k, v, q_idx, k_idx, w_idx, g_a, g_b


def gmlp_spatial(u, v, g_norm, ws, b):
    B, T, _ = v.shape
    vn = rms_norm(v, g_norm)
    n_c = -(-T // GMLP_CHUNK)
    tp = n_c * GMLP_CHUNK
    vp = jnp.pad(vn, ((0, 0), (0, tp - T), (0, 0))).reshape(B, n_c, GMLP_CHUNK, G_A, D_A // G_A)
    i = jnp.arange(GMLP_CHUNK)
    mask = (i[None, :] // CHUNK) <= (i[:, None] // CHUNK)
    wm = jnp.where(mask[None], ws, 0)
    s = jnp.einsum('gij,bcjgd->bcigd', wm, vp) + jnp.transpose(b)[None, None, :, :, None]
    s = s.reshape(B, tp, D_A)[:, :T]
    return u * s, vn


def dsa_attend(q, q_idx, w_idx, qpos, k_all, v_all, kidx_all, kpos, topk):
    B, Tq = q.shape[:2]
    admissible = (kpos[None, :] // CHUNK) <= (qpos[:, None] // CHUNK)
    rel = jax.nn.relu(jnp.einsum('bthd,bsd->bths', q_idx, kidx_all).astype(jnp.float32))
    score = jnp.einsum('bths,bth->bts', rel, w_idx.astype(jnp.float32))
    score = jnp.where(admissible[None], score, -jnp.inf)
    _, idx = lax.top_k(score, topk)
    k_sel = jax.vmap(lambda kk, ii: kk[ii])(k_all, idx)
    v_sel = jax.vmap(lambda vv, ii: vv[ii])(v_all, idx)
    valid = (kpos[idx] // CHUNK) <= (qpos[None, :, None] // CHUNK)
    qg = q.reshape(B, Tq, N_KV_HEADS, N_HEADS // N_KV_HEADS, HEAD_DIM)
    logits = jnp.einsum('btngd,btsnd->btngs', qg, k_sel).astype(jnp.float32) * (HEAD_DIM ** -0.5)
    logits = jnp.where(valid[:, :, None, None, :], logits, -jnp.inf)
    p = jax.nn.softmax(logits, axis=-1).astype(v_sel.dtype)
    out = jnp.einsum('btngs,btsnd->btngd', p, v_sel)
    return out.reshape(B, Tq, N_HEADS * HEAD_DIM)


def dsa_prompt(q, q_idx, w_idx, pos, k, v, k_idx, topk):
    B, S = q.shape[:2]
    nb = S // QBLOCK

    def blk(a):
        return jnp.swapaxes(a.reshape((B, nb, QBLOCK) + a.shape[2:]), 0, 1)

    def one(args):
        qb, qib, wib, posb = args
        return dsa_attend(qb, qib, wib, posb, k, v, k_idx, pos, topk)

    out = lax.map(one, (blk(q), blk(q_idx), blk(w_idx), pos.reshape(nb, QBLOCK)))
    return jnp.swapaxes(out, 0, 1).reshape(B, S, N_HEADS * HEAD_DIM)


def merge_branches(a_out, b_out, g_a, g_b, w_a, w_b, w_o):
    y = jax.nn.sigmoid(g_a) * (a_out @ w_a) + jax.nn.sigmoid(g_b) * (b_out @ w_b)
    return y @ w_o


def conv_ffn(h, buf, w_up, conv_w, conv_b, w_down):
    T = h.shape[1]
    z = h @ w_up
    zp = jnp.concatenate([buf.astype(z.dtype), z], axis=1)
    c = conv_b
    for tap in range(CONV_WIDTH):
        c = c + conv_w[tap] * zp[:, tap:tap + T]
    gate, up = jnp.split(c, 2, axis=-1)
    return (jax.nn.silu(gate) * up) @ w_down, zp[:, -(CONV_WIDTH - 1):]


def setup_inputs(seed: int = 0) -> dict:
    key = jax.random.key(seed)
    ks = jax.random.split(key, 24)
    f32 = jnp.float32

    def nrm(k, shape, scale=1.0):
        return jax.random.normal(k, shape, f32) * scale

    return {
        "x_prompt": nrm(ks[0], (BATCH, SEQ, D_MODEL)),
        "x_sample": nrm(ks[1], (DEC_BATCH, DEC_SEQ, D_MODEL)),
        "cache_k": nrm(ks[2], (DEPTH, DEC_BATCH, PAST_LEN, N_KV_HEADS, HEAD_DIM)),
        "cache_v": nrm(ks[3], (DEPTH, DEC_BATCH, PAST_LEN, N_KV_HEADS, HEAD_DIM)),
        "cache_kidx": nrm(ks[4], (DEPTH, DEC_BATCH, PAST_LEN, IDX_DIM)),
        "state_ffn_conv": nrm(ks[5], (DEPTH, DEC_BATCH, CONV_WIDTH - 1, 2 * D_FF)),
        "norm_attn_g": 1.0 + nrm(ks[6], (DEPTH, D_MODEL), 0.02),
        "w_in": nrm(ks[7], (DEPTH, D_MODEL, IN_COLS), D_MODEL ** -0.5),
        "gmlp_norm_g": 1.0 + nrm(ks[8], (DEPTH, D_A), 0.02),
        "gmlp_ws": nrm(ks[9], (DEPTH, G_A, GMLP_CHUNK, GMLP_CHUNK), GMLP_CHUNK ** -0.5),
        "gmlp_b": nrm(ks[10], (DEPTH, G_A, GMLP_CHUNK), 0.02),
        "w_branch_a": nrm(ks[11], (DEPTH, D_A, D_MODEL), D_A ** -0.5),
        "w_branch_b": nrm(ks[12], (DEPTH, N_HEADS * HEAD_DIM, D_MODEL), (N_HEADS * HEAD_DIM) ** -0.5),
        "w_out": nrm(ks[13], (DEPTH, D_MODEL, D_MODEL), D_MODEL ** -0.5),
        "norm_ffn_g": 1.0 + nrm(ks[14], (DEPTH, D_MODEL), 0.02),
        "w_up": nrm(ks[15], (DEPTH, D_MODEL, 2 * D_FF), D_MODEL ** -0.5),
        "conv_w": nrm(ks[16], (DEPTH, CONV_WIDTH, 2 * D_FF), CONV_WIDTH ** -0.5),
        "conv_b": nrm(ks[17], (DEPTH, 2 * D_FF), 0.02),
        "w_down": nrm(ks[18], (DEPTH, D_FF, D_MODEL), D_FF ** -0.5),
        "norm_final_g": 1.0 + nrm(ks[19], (D_MODEL,), 0.02),
    }


def reference(x_prompt, x_sample, cache_k, cache_v, cache_kidx, state_ffn_conv,
              norm_attn_g, w_in, gmlp_norm_g, gmlp_ws, gmlp_b, w_branch_a, w_branch_b,
              w_out, norm_ffn_g, w_up, conv_w, conv_b, w_down, norm_final_g):
    B, S, _ = x_prompt.shape
    DB, T, _ = x_sample.shape
    P = cache_k.shape[2]
    pos_p = jnp.arange(S, dtype=jnp.int32)
    pos_s = P + jnp.arange(T, dtype=jnp.int32)
    kpos_s = jnp.arange(P + T, dtype=jnp.int32)
    topk_p = min(TOPK_MAX, S // 4)
    topk_s = min(TOPK_MAX, (P + T) // 4)

    xp, xs = x_prompt, x_sample
    kp_l, vp_l, kip_l, cp_l = [], [], [], []
    ks_l, vs_l, kis_l, cs_l, gv_l = [], [], [], [], []
    for l in range(DEPTH):
        hp = rms_norm(xp, norm_attn_g[l])
        u_a, v_a, q, k, v, qi, ki, wi, ga, gb = project(hp, w_in[l], pos_p)
        a_out, _ = gmlp_spatial(u_a, v_a, gmlp_norm_g[l], gmlp_ws[l], gmlp_b[l])
        b_out = dsa_prompt(q, qi, wi, pos_p, k, v, ki, topk_p)
        xp = xp + merge_branches(a_out, b_out, ga, gb, w_branch_a[l], w_branch_b[l], w_out[l])
        buf0 = jnp.zeros((B, CONV_WIDTH - 1, 2 * D_FF), dtype=xp.dtype)
        f_out, cbuf_p = conv_ffn(rms_norm(xp, norm_ffn_g[l]), buf0, w_up[l], conv_w[l], conv_b[l], w_down[l])
        xp = xp + f_out
        kp_l.append(k); vp_l.append(v); kip_l.append(ki); cp_l.append(cbuf_p)

        hs = rms_norm(xs, norm_attn_g[l])
        u_a, v_a, q, k, v, qi, ki, wi, ga, gb = project(hs, w_in[l], pos_s)
        a_out, vn = gmlp_spatial(u_a, v_a, gmlp_norm_g[l], gmlp_ws[l], gmlp_b[l])
        k_all = jnp.concatenate([cache_k[l].astype(k.dtype), k], axis=1)
        v_all = jnp.concatenate([cache_v[l].astype(v.dtype), v], axis=1)
        ki_all = jnp.concatenate([cache_kidx[l].astype(ki.dtype), ki], axis=1)
        b_out = dsa_attend(q, qi, wi, pos_s, k_all, v_all, ki_all, kpos_s, topk_s)
        xs = xs + merge_branches(a_out, b_out, ga, gb, w_branch_a[l], w_branch_b[l], w_out[l])
        f_out, cbuf_s = conv_ffn(rms_norm(xs, norm_ffn_g[l]), state_ffn_conv[l], w_up[l], conv_w[l], conv_b[l], w_down[l])
        xs = xs + f_out
        ks_l.append(k); vs_l.append(v); kis_l.append(ki); cs_l.append(cbuf_s); gv_l.append(vn)

    y_prompt = rms_norm(xp, norm_final_g)
    y_sample = rms_norm(xs, norm_final_g)
    new_cache_k_prompt = jnp.stack(kp_l)
    new_cache_v_prompt = jnp.stack(vp_l)
    new_cache_kidx_prompt = jnp.stack(kip_l)
    new_state_ffn_conv_prompt = jnp.stack(cp_l)
    new_cache_k_sample = jnp.stack(ks_l)
    new_cache_v_sample = jnp.stack(vs_l)
    new_cache_kidx_sample = jnp.stack(kis_l)
    new_state_ffn_conv_sample = jnp.stack(cs_l)
    new_state_gmlp_v_sample = jnp.stack(gv_l)
    return (y_prompt, y_sample, new_cache_k_prompt, new_cache_v_prompt, new_cache_kidx_prompt,
            new_state_ffn_conv_prompt, new_cache_k_sample, new_cache_v_sample, new_cache_kidx_sample,
            new_state_ffn_conv_sample, new_state_gmlp_v_sample)
```

```python
import functools
import math

import jax
import jax.numpy as jnp
from jax import lax
from jax.experimental import pallas as pl
from jax.experimental.pallas import tpu as pltpu

CHUNK = 64
GMLP_CHUNK = 128
G_A = 8
HEAD_DIM = 128
Q_PER_KV = 4
N_IDX_HEADS = 16
IDX_DIM = 64
TOPK_MAX = 256
ROPE_THETA = 500000.0
CONV_WIDTH = 3
EPS = 1e-6

LANE = 128
SUBLANE = 8
VMEM_LIMIT_BYTES = 56 * 1024 * 1024

BF16 = jnp.bfloat16
F32 = jnp.float32


def _params(*semantics):
    return pltpu.CompilerParams(dimension_semantics=semantics, vmem_limit_bytes=VMEM_LIMIT_BYTES)


def _tile(n, pref, align):
    if n <= pref:
        return n
    t = (pref // align) * align
    while t >= align:
        if n % t == 0:
            return t
        t -= align
    raise ValueError(f"no {align}-aligned tile of {n} below {pref}")


def _rmsnorm_kernel(x_ref, g_ref, o_ref):
    x = x_ref[...]
    ms = jnp.mean(x * x, axis=-1, keepdims=True)
    o_ref[...] = (x * lax.rsqrt(ms + EPS) * g_ref[...]).astype(o_ref.dtype)


def rmsnorm(x, g, out_dtype):
    n, d = x.shape
    tr = _tile(n, 256, SUBLANE)
    return pl.pallas_call(
        _rmsnorm_kernel,
        grid=(n // tr,),
        in_specs=[pl.BlockSpec((tr, d), lambda i: (i, 0)), pl.BlockSpec((1, d), lambda i: (0, 0))],
        out_specs=pl.BlockSpec((tr, d), lambda i: (i, 0)),
        out_shape=jax.ShapeDtypeStruct((n, d), out_dtype),
        compiler_params=_params("parallel"),
        name="rmsnorm",
    )(x, g.reshape(1, d))


def _rope_slab(x, c, s1, s2, shift):
    return x * c + pltpu.roll(x, LANE - shift, 1) * s1 + pltpu.roll(x, shift, 1) * s2


def _proj_kernel(a_ref, b_ref, *rest, rope_shift):
    acc = jnp.dot(a_ref[...], b_ref[...], preferred_element_type=F32)
    if rope_shift:
        c_ref, s1_ref, s2_ref, o_ref = rest
        c, s1, s2 = c_ref[...], s1_ref[...], s2_ref[...]
        for h in range(acc.shape[1] // LANE):
            sl = slice(h * LANE, (h + 1) * LANE)
            o_ref[:, sl] = _rope_slab(acc[:, sl], c, s1, s2, rope_shift).astype(o_ref.dtype)
    else:
        (o_ref,) = rest
        o_ref[...] = acc.astype(o_ref.dtype)


def project(a, b, out_dtype, rope=None):
    m, k = a.shape
    n = b.shape[1]
    tm = _tile(m, 1024, SUBLANE)
    tn = _tile(n, 1024, LANE)
    in_specs = [pl.BlockSpec((tm, k), lambda i, j: (i, 0)), pl.BlockSpec((k, tn), lambda i, j: (0, j))]
    args = [a, b]
    shift = 0
    if rope is not None:
        shift, tabs = rope[0], rope[1:]
        reps = tabs[0].shape[0] // tm
        assert reps * tm == tabs[0].shape[0]
        for t in tabs:
            in_specs.append(pl.BlockSpec((tm, LANE), lambda i, j: (i % reps, 0)))
            args.append(t)
    return pl.pallas_call(
        functools.partial(_proj_kernel, rope_shift=shift),
        grid=(m // tm, n // tn),
        in_specs=in_specs,
        out_specs=pl.BlockSpec((tm, tn), lambda i, j: (i, j)),
        out_shape=jax.ShapeDtypeStruct((m, n), out_dtype),
        compiler_params=_params("parallel", "arbitrary"),
        name="project",
    )(*args)


def _rope_tables(pos, head_dim, rot_dim, rows):
    half = rot_dim // 2
    inv_freq = ROPE_THETA ** (-jnp.arange(half, dtype=F32) / half)
    ang = pos.astype(F32)[:, None] * inv_freq[None, :]
    cos, sin = jnp.cos(ang), jnp.sin(ang)
    t = pos.shape[0]
    zeros_h = jnp.zeros((t, half), F32)
    rest0 = jnp.zeros((t, head_dim - rot_dim), F32)
    c = jnp.concatenate([cos, cos, jnp.ones((t, head_dim - rot_dim), F32)], axis=1)
    s1 = jnp.concatenate([-sin, zeros_h, rest0], axis=1)
    s2 = jnp.concatenate([zeros_h, sin, rest0], axis=1)
    reps_l = LANE // head_dim
    reps_r = rows // t
    return tuple(jnp.tile(x, (reps_r, reps_l)) for x in (c, s1, s2))


def _gmlp_kernel(u_ref, v_ref, g_ref, wm_ref, b_ref, a_ref, *vn_out, groups):
    v = v_ref[...]
    vn = v * lax.rsqrt(jnp.mean(v * v, axis=-1, keepdims=True) + EPS) * g_ref[...]
    if vn_out:
        vn_out[0][...] = vn
    vb = vn.astype(BF16)
    gw = v.shape[1] // groups
    for g in range(groups):
        sl = slice(g * gw, (g + 1) * gw)
        s = jnp.dot(wm_ref[g], vb[:, sl], preferred_element_type=F32) + b_ref[:, sl]
        a_ref[:, sl] = (u_ref[:, sl] * s).astype(a_ref.dtype)


def gmlp(uv, g_norm, wm, bias, want_vn):
    m, d2 = uv.shape
    da = d2 // 2
    tc = GMLP_CHUNK
    out_shape = [jax.ShapeDtypeStruct((m, da), BF16)]
    out_specs = [pl.BlockSpec((tc, da), lambda i: (i, 0))]
    if want_vn:
        out_shape.append(jax.ShapeDtypeStruct((m, da), F32))
        out_specs.append(pl.BlockSpec((tc, da), lambda i: (i, 0)))
    res = pl.pallas_call(
        functools.partial(_gmlp_kernel, groups=wm.shape[0]),
        grid=(m // tc,),
        in_specs=[
            pl.BlockSpec((tc, da), lambda i: (i, 0)),
            pl.BlockSpec((tc, da), lambda i: (i, 1)),
            pl.BlockSpec((1, da), lambda i: (0, 0)),
            pl.BlockSpec(wm.shape, lambda i: (0, 0, 0)),
            pl.BlockSpec((tc, da), lambda i: (0, 0)),
        ],
        out_specs=out_specs,
        out_shape=out_shape,
        compiler_params=_params("parallel"),
        name="gmlp",
    )(uv, uv, g_norm.reshape(1, da), wm, bias)
    return res if want_vn else (res[0], None)


def _dsa_kernel(qi_ref, w_ref, q_ref, kidx_ref, k_ref, v_ref, o_ref, bias_ref, *,
                tq, n_kv, topk, causal, n_valid, scale):
    s_len = kidx_ref.shape[0]
    n_tiles = s_len // LANE
    kidx = kidx_ref[...]
    w = w_ref[...]
    dn = (((1,), (1,)), ((), ()))

    score = jnp.zeros((tq, s_len), F32)
    for h in range(N_IDX_HEADS):
        rel = lax.dot_general(qi_ref[h * tq:(h + 1) * tq, :], kidx, dn, preferred_element_type=F32)
        score = score + jnp.maximum(rel, 0.0) * w[:, h:h + 1]

    kpos = lax.broadcasted_iota(jnp.int32, (tq, s_len), 1)
    if causal:
        row = lax.broadcasted_iota(jnp.int32, (tq, s_len), 0) + pl.program_id(1) * tq
        adm = kpos < (jnp.right_shift(row, CHUNK.bit_length() - 1) + 1) * CHUNK
    else:
        adm = kpos < n_valid
    score = jnp.where(adm, score, -jnp.inf)

    kf = float(topk)

    def count_ge(t):
        return jnp.sum(jnp.where(score >= t, 1.0, 0.0), axis=1, keepdims=True)

    smax = jnp.max(score, axis=1, keepdims=True)
    smin = jnp.min(jnp.where(adm, score, jnp.inf), axis=1, keepdims=True)
    c_max = count_ge(smax)
    c_min = count_ge(smin)
    big = c_max > kf
    lo0 = jnp.where(big, smax, smin)
    hi0 = jnp.where(big, jnp.inf, smax)
    c0 = jnp.where(big, c_max, c_min)

    def active_of(lo, hi, c_lo):
        mid = 0.5 * lo + 0.5 * hi
        return mid, (c_lo > kf) & (mid > lo) & (mid < hi)

    def cond(carry):
        _, act = active_of(*carry)
        return jnp.max(jnp.where(act, 1.0, 0.0)) > 0.0

    def body(carry):
        lo, hi, c_lo = carry
        mid, act = active_of(lo, hi, c_lo)
        cnt = count_ge(mid)
        take = act & (cnt >= kf)
        drop = act & (cnt < kf)
        return jnp.where(take, mid, lo), jnp.where(drop, mid, hi), jnp.where(take, cnt, c_lo)

    lo, hi, c_lo = lax.while_loop(cond, body, (lo0, hi0, c0))

    bias_ref[...] = jnp.where(score >= lo, 0.0, -jnp.inf)
    tie_row = c_lo > kf

    @pl.when(jnp.max(jnp.where(tie_row, 1.0, 0.0)) > 0.0)
    def _():
        quota = kf - count_ge(hi)
        tri = (lax.broadcasted_iota(jnp.int32, (LANE, LANE), 0)
               <= lax.broadcasted_iota(jnp.int32, (LANE, LANE), 1)).astype(F32).astype(BF16)
        before = jnp.zeros((tq, 1), F32)
        for jt in range(n_tiles):
            sl = slice(jt * LANE, (jt + 1) * LANE)
            sc = score[:, sl]
            cand = jnp.where((sc >= lo) & (sc < hi), 1.0, 0.0)
            rank = jnp.dot(cand.astype(BF16), tri, preferred_element_type=F32) + before
            keep = (sc >= hi) | ((cand > 0.0) & (rank <= quota))
            bias_ref[:, sl] = jnp.where(tie_row, jnp.where(keep, 0.0, -jnp.inf),
                                        jnp.where(sc >= lo, 0.0, -jnp.inf))
            before = before + jnp.sum(cand, axis=1, keepdims=True)

    bias = bias_ref[...]
    bias_g = jnp.concatenate([bias] * Q_PER_KV, axis=0)
    for n in range(n_kv):
        qn = jnp.concatenate(
            [q_ref[:, (n * Q_PER_KV + g) * HEAD_DIM:(n * Q_PER_KV + g + 1) * HEAD_DIM]
             for g in range(Q_PER_KV)], axis=0)
        ksl = slice(n * HEAD_DIM, (n + 1) * HEAD_DIM)
        logits = lax.dot_general(qn, k_ref[:, ksl], dn, preferred_element_type=F32) * scale + bias_g
        mx = jnp.max(logits, axis=1, keepdims=True)
        p = jnp.exp(logits - mx)
        den = jnp.sum(p, axis=1, keepdims=True)
        o = jnp.dot(p.astype(BF16), v_ref[:, ksl], preferred_element_type=F32) / den
        for g in range(Q_PER_KV):
            hsl = slice((n * Q_PER_KV + g) * HEAD_DIM, (n * Q_PER_KV + g + 1) * HEAD_DIM)
            o_ref[:, hsl] = o[g * tq:(g + 1) * tq].astype(o_ref.dtype)


def dsa(qi_ht, w_idx, q, kidx, k, v, *, tq, topk, causal, n_valid):
    b, nq = qi_ht.shape[:2]
    s_len = kidx.shape[1]
    dq = q.shape[2]
    dkv = k.shape[2]
    kern = functools.partial(_dsa_kernel, tq=tq, n_kv=dkv // HEAD_DIM, topk=topk, causal=causal,
                             n_valid=n_valid, scale=HEAD_DIM ** -0.5)
    return pl.pallas_call(
        kern,
        grid=(b, nq),
        in_specs=[
            pl.BlockSpec((None, None, N_IDX_HEADS * tq, IDX_DIM), lambda i, j: (i, j, 0, 0)),
            pl.BlockSpec((None, tq, N_IDX_HEADS), lambda i, j: (i, j, 0)),
            pl.BlockSpec((None, tq, dq), lambda i, j: (i, j, 0)),
            pl.BlockSpec((None, s_len, IDX_DIM), lambda i, j: (i, 0, 0)),
            pl.BlockSpec((None, s_len, dkv), lambda i, j: (i, 0, 0)),
            pl.BlockSpec((None, s_len, dkv), lambda i, j: (i, 0, 0)),
        ],
        out_specs=pl.BlockSpec((None, tq, dq), lambda i, j: (i, j, 0)),
        out_shape=jax.ShapeDtypeStruct((b, nq * tq, dq), BF16),
        scratch_shapes=[pltpu.VMEM((tq, s_len), F32)],
        compiler_params=_params("parallel", "arbitrary"),
        name="dsa",
    )(qi_ht, w_idx, q, kidx, k, v)


def _merge_kernel(a_ref, b_ref, wa_ref, wb_ref, ga_ref, gb_ref, o_ref):
    ya = jnp.dot(a_ref[...], wa_ref[...], preferred_element_type=F32)
    yb = jnp.dot(b_ref[...], wb_ref[...], preferred_element_type=F32)
    y = jax.nn.sigmoid(ga_ref[...]) * ya + jax.nn.sigmoid(gb_ref[...]) * yb
    o_ref[...] = y.astype(o_ref.dtype)


def merge(a, b, wa, wb, gates):
    m, ka = a.shape
    kb = b.shape[1]
    d = wa.shape[1]
    tm = _tile(m, 1024, SUBLANE)
    tn = _tile(d, 512, LANE)
    nj = d // tn
    return pl.pallas_call(
        _merge_kernel,
        grid=(m // tm, nj),
        in_specs=[
            pl.BlockSpec((tm, ka), lambda i, j: (i, 0)),
            pl.BlockSpec((tm, kb), lambda i, j: (i, 0)),
            pl.BlockSpec((ka, tn), lambda i, j: (0, j)),
            pl.BlockSpec((kb, tn), lambda i, j: (0, j)),
            pl.BlockSpec((tm, tn), lambda i, j: (i, j)),
            pl.BlockSpec((tm, tn), lambda i, j: (i, j + nj)),
        ],
        out_specs=pl.BlockSpec((tm, tn), lambda i, j: (i, j)),
        out_shape=jax.ShapeDtypeStruct((m, d), BF16),
        compiler_params=_params("parallel", "arbitrary"),
        name="merge",
    )(a, b, wa, wb, gates, gates)


def _matmul_res_kernel(a_ref, b_ref, r_ref, o_ref):
    o_ref[...] = r_ref[...] + jnp.dot(a_ref[...], b_ref[...], preferred_element_type=F32)


def matmul_residual(a, b, res):
    m, k = a.shape
    n = b.shape[1]
    tm = _tile(m, 1024, SUBLANE)
    tn = _tile(n, 512, LANE)
    return pl.pallas_call(
        _matmul_res_kernel,
        grid=(m // tm, n // tn),
        in_specs=[
            pl.BlockSpec((tm, k), lambda i, j: (i, 0)),
            pl.BlockSpec((k, tn), lambda i, j: (0, j)),
            pl.BlockSpec((tm, tn), lambda i, j: (i, j)),
        ],
        out_specs=pl.BlockSpec((tm, tn), lambda i, j: (i, j)),
        out_shape=jax.ShapeDtypeStruct((m, n), F32),
        compiler_params=_params("parallel", "arbitrary"),
        name="matmul_residual",
    )(a, b, res)


def _ffn_up_kernel(h_ref, wg_ref, wu_ref, cwg_ref, cwu_ref, cbg_ref, cbu_ref, sg_ref, su_ref,
                   act_ref, zg_ref, zu_ref, bg_ref, bu_ref, *, seq_len, tiles_per_seq):
    tm = h_ref.shape[0]
    h = h_ref[...]
    pad = SUBLANE
    halves = ((wg_ref, cwg_ref, cbg_ref, sg_ref, zg_ref, bg_ref),
              (wu_ref, cwu_ref, cbu_ref, su_ref, zu_ref, bu_ref))
    conv = []
    for w_ref, cw_ref, cb_ref, st_ref, zlast_ref, buf_ref in halves:
        z = jnp.dot(h, w_ref[...], preferred_element_type=F32)
        buf_ref[pad:pad + tm, :] = z
        cw = cw_ref[...]
        cb = cb_ref[...]
        if tiles_per_seq >= 1:
            @pl.when(pl.program_id(1) % tiles_per_seq == 0)
            def _():
                buf_ref[pad - 2:pad, :] = st_ref[0]
            c = (cb + cw[0:1] * buf_ref[pad - 2:pad - 2 + tm, :] + cw[1:2] * buf_ref[pad - 1:pad - 1 + tm, :]
                 + cw[2:3] * z)
            tail = buf_ref[pad + tm - 2:pad + tm, :]
            zlast_ref[0] = tail
            buf_ref[pad - 2:pad, :] = tail
            conv.append(c)
        else:
            buf_ref[0:pad, :] = jnp.zeros((pad, z.shape[1]), F32)
            rowi = lax.broadcasted_iota(jnp.int32, (seq_len, z.shape[1]), 0)
            parts = []
            for s in range(tm // seq_len):
                base = pad + s * seq_len
                st0 = st_ref[s, 0:1, :]
                st1 = st_ref[s, 1:2, :]
                z0 = buf_ref[base:base + seq_len, :]
                p1 = jnp.where(rowi == 0, st1, buf_ref[base - 1:base - 1 + seq_len, :])
                p2 = jnp.where(rowi == 0, st0, jnp.where(rowi == 1, st1, buf_ref[base - 2:base - 2 + seq_len, :]))
                parts.append(cb + cw[0:1] * p2 + cw[1:2] * p1 + cw[2:3] * z0)
                zlast_ref[s] = z0[seq_len - 2:seq_len]
            conv.append(jnp.concatenate(parts, axis=0))
    gate, up = conv
    act_ref[...] = (gate * jax.nn.sigmoid(gate) * up).astype(act_ref.dtype)


def ffn_up(h, w_up, conv_w, conv_b, state, seq_len):
    m, d = h.shape
    f = w_up.shape[1] // 2
    tm = _tile(m, 1024, SUBLANE)
    tn = _tile(f, 256, LANE)
    nj = f // tn
    if seq_len >= tm:
        assert seq_len % tm == 0
        tiles_per_seq = seq_len // tm
        n_state = 1
        state_map_g = lambda j, i: (i // tiles_per_seq, 0, j)
        state_map_u = lambda j, i: (i // tiles_per_seq, 0, j + nj)
        n_last = m // tm
    else:
        assert tm % seq_len == 0 and seq_len % SUBLANE == 0
        tiles_per_seq = 0
        n_state = tm // seq_len
        state_map_g = lambda j, i: (i, 0, j)
        state_map_u = lambda j, i: (i, 0, j + nj)
        n_last = m // seq_len
    kern = functools.partial(_ffn_up_kernel, seq_len=seq_len, tiles_per_seq=tiles_per_seq)
    cb = conv_b.reshape(1, 2 * f)
    act, zg, zu = pl.pallas_call(
        kern,
        grid=(nj, m // tm),
        in_specs=[
            pl.BlockSpec((tm, d), lambda j, i: (i, 0)),
            pl.BlockSpec((d, tn), lambda j, i: (0, j)),
            pl.BlockSpec((d, tn), lambda j, i: (0, j + nj)),
            pl.BlockSpec((CONV_WIDTH, tn), lambda j, i: (0, j)),
            pl.BlockSpec((CONV_WIDTH, tn), lambda j, i: (0, j + nj)),
            pl.BlockSpec((1, tn), lambda j, i: (0, j)),
            pl.BlockSpec((1, tn), lambda j, i: (0, j + nj)),
            pl.BlockSpec((n_state, 2, tn), state_map_g),
            pl.BlockSpec((n_state, 2, tn), state_map_u),
        ],
        out_specs=[
            pl.BlockSpec((tm, tn), lambda j, i: (i, j)),
            pl.BlockSpec((n_state, 2, tn), lambda j, i: (i, 0, j)),
            pl.BlockSpec((n_state, 2, tn), lambda j, i: (i, 0, j)),
        ],
        out_shape=[
            jax.ShapeDtypeStruct((m, f), BF16),
            jax.ShapeDtypeStruct((n_last, 2, f), F32),
            jax.ShapeDtypeStruct((n_last, 2, f), F32),
        ],
        scratch_shapes=[pltpu.VMEM((tm + SUBLANE, tn), F32), pltpu.VMEM((tm + SUBLANE, tn), F32)],
        compiler_params=_params("arbitrary", "arbitrary"),
        name="ffn_up",
    )(h, w_up, w_up, conv_w, conv_w, cb, cb, state, state)
    zlast = jnp.concatenate([zg, zu], axis=-1)
    if tiles_per_seq > 1:
        zlast = zlast[tiles_per_seq - 1::tiles_per_seq]
    return act, zlast


def _ffn_down_kernel(a_ref, b_ref, r_ref, o_ref, acc_ref, *, nk):
    kk = pl.program_id(2)
    part = jnp.dot(a_ref[...], b_ref[...], preferred_element_type=F32)

    @pl.when(kk == 0)
    def _():
        acc_ref[...] = r_ref[...] + part

    @pl.when((kk > 0) & (kk < nk - 1))
    def _():
        acc_ref[...] = acc_ref[...] + part

    @pl.when(kk == nk - 1)
    def _():
        if nk == 1:
            o_ref[...] = r_ref[...] + part
        else:
            o_ref[...] = acc_ref[...] + part


def ffn_down(a, b, res):
    m, k = a.shape
    n = b.shape[1]
    tm = _tile(m, 1024, SUBLANE)
    tn = _tile(n, 512, LANE)
    tk = _tile(k, 5632, LANE)
    nk = k // tk
    return pl.pallas_call(
        functools.partial(_ffn_down_kernel, nk=nk),
        grid=(m // tm, n // tn, nk),
        in_specs=[
            pl.BlockSpec((tm, tk), lambda i, j, kk: (i, kk)),
            pl.BlockSpec((tk, tn), lambda i, j, kk: (kk, j)),
            pl.BlockSpec((tm, tn), lambda i, j, kk: (i, j)),
        ],
        out_specs=pl.BlockSpec((tm, tn), lambda i, j, kk: (i, j)),
        out_shape=jax.ShapeDtypeStruct((m, n), F32),
        scratch_shapes=[pltpu.VMEM((tm, tn), F32)],
        compiler_params=_params("parallel", "arbitrary", "arbitrary"),
        name="ffn_down",
    )(a, b, res)


def _split_w_in(w_in, d):
    da = d // 2
    n_heads = d // 256
    n_kv = n_heads // Q_PER_KV
    sizes = (da, da, n_heads * HEAD_DIM, n_kv * HEAD_DIM, n_kv * HEAD_DIM,
             N_IDX_HEADS * IDX_DIM, IDX_DIM, N_IDX_HEADS, d, d)
    offs = [0]
    for s in sizes:
        offs.append(offs[-1] + s)
    assert offs[-1] == w_in.shape[1]
    col = lambda a, b: w_in[:, offs[a]:offs[b]].astype(BF16)
    w_ki = jnp.pad(w_in[:, offs[6]:offs[8]], ((0, 0), (0, LANE - IDX_DIM - N_IDX_HEADS))).astype(BF16)
    return dict(uv=col(0, 2), q=col(2, 3), k=col(3, 4), v=col(4, 5), qi=col(5, 6), ki=w_ki, g=col(8, 10))


def _layer(x, pos, seq_len, w, past, conv_state, want_vn):
    bsz, t, d = x.shape
    m = bsz * t
    x2 = x.reshape(m, d)
    n_heads = d // 256
    n_kv = n_heads // Q_PER_KV

    h = rmsnorm(x2, w["norm_attn_g"], BF16)

    rows = max(t, _tile(m, 1024, SUBLANE))
    rope_h = (HEAD_DIM // 8,) + _rope_tables(pos, HEAD_DIM, HEAD_DIM // 4, rows)
    rope_i = (IDX_DIM // 8,) + _rope_tables(pos, IDX_DIM, IDX_DIM // 4, rows)
    ci, s1i, s2i = rope_i[1:]
    lane = jnp.arange(LANE)
    is_w = (lane >= IDX_DIM) & (lane < IDX_DIM + N_IDX_HEADS)
    w_scale = N_IDX_HEADS ** -0.5 * IDX_DIM ** -0.5
    rope_kw = (IDX_DIM // 8,
               jnp.where(is_w, w_scale, jnp.where(lane < IDX_DIM, ci, 1.0)).astype(F32),
               jnp.where(lane < IDX_DIM, s1i, 0.0), jnp.where(lane < IDX_DIM, s2i, 0.0))

    uv = project(h, w["in"]["uv"], F32)
    q = project(h, w["in"]["q"], BF16, rope_h)
    k = project(h, w["in"]["k"], F32, rope_h)
    v = project(h, w["in"]["v"], F32)
    qi = project(h, w["in"]["qi"], F32, rope_i)
    kw = project(h, w["in"]["ki"], F32, rope_kw)
    gates = project(h, w["in"]["g"], F32)
    kidx = kw[:, :IDX_DIM]
    widx = kw[:, IDX_DIM:IDX_DIM + N_IDX_HEADS]

    chunk_rows = min(t, GMLP_CHUNK)
    a_out, vn = gmlp(uv, w["gmlp_norm_g"], w["gmlp_wm"](chunk_rows), w["gmlp_bias"](chunk_rows), want_vn)

    k3 = k.reshape(bsz, t, n_kv * HEAD_DIM)
    v3 = v.reshape(bsz, t, n_kv * HEAD_DIM)
    ki3 = kidx.reshape(bsz, t, IDX_DIM)
    if past is None:
        tq = CHUNK
        k_all, v_all, ki_all = k3, v3, ki3
        n_valid = t
        causal = True
    else:
        tq = t
        pk, pv, pki = past
        p_len = pk.shape[1]
        n_valid = p_len + t
        s_pad = -(-n_valid // LANE) * LANE
        padk = lambda old, new: jnp.pad(jnp.concatenate([old.reshape(bsz, p_len, -1), new], axis=1),
                                        ((0, 0), (0, s_pad - n_valid), (0, 0)))
        k_all, v_all, ki_all = padk(pk, k3), padk(pv, v3), padk(pki, ki3)
        causal = False
    topk = min(TOPK_MAX, n_valid // 4)
    nq = t // tq
    qi_ht = (qi.reshape(bsz, nq, tq, N_IDX_HEADS, IDX_DIM).transpose(0, 1, 3, 2, 4)
             .reshape(bsz, nq, N_IDX_HEADS * tq, IDX_DIM).astype(BF16))
    b_out = dsa(qi_ht, widx.reshape(bsz, t, N_IDX_HEADS), q.reshape(bsz, t, n_heads * HEAD_DIM),
                ki_all.astype(BF16), k_all.astype(BF16), v_all.astype(BF16),
                tq=tq, topk=topk, causal=causal, n_valid=n_valid)

    y = merge(a_out, b_out.reshape(m, n_heads * HEAD_DIM), w["a"], w["b"], gates)
    x2 = matmul_residual(y, w["o"], x2)

    hf = rmsnorm(x2, w["norm_ffn_g"], BF16)
    act, zlast = ffn_up(hf, w["up"], w["conv_w"], w["conv_b"], conv_state, seq_len)
    x2 = ffn_down(act, w["down"], x2)
    return (x2.reshape(bsz, t, d), k.reshape(bsz, t, n_kv, HEAD_DIM), v.reshape(bsz, t, n_kv, HEAD_DIM),
            ki3, zlast, vn)


def kernel(x_prompt, x_sample, cache_k, cache_v, cache_kidx, state_ffn_conv, norm_attn_g, w_in, gmlp_norm_g, gmlp_ws, gmlp_b, w_branch_a, w_branch_b, w_out, norm_ffn_g, w_up, conv_w, conv_b, w_down, norm_final_g):
    bsz, s, d = x_prompt.shape
    dbsz, t, _ = x_sample.shape
    depth = w_in.shape[0]
    p_len = cache_k.shape[2]
    da = d // 2
    f2 = w_up.shape[2]
    assert s % GMLP_CHUNK == 0 and GMLP_CHUNK % t == 0 and s % CHUNK == 0

    pos_p = jnp.arange(s, dtype=jnp.int32)
    pos_s = p_len + jnp.arange(t, dtype=jnp.int32)
    ci = jnp.arange(GMLP_CHUNK)
    chunk_mask = (ci[None, :] // CHUNK) <= (ci[:, None] // CHUNK)

    xp, xs = x_prompt, x_sample
    outs = [[] for _ in range(9)]
    for l in range(depth):
        wm_full = jnp.where(chunk_mask[None], gmlp_ws[l], 0.0)
        bias_rows = jnp.repeat(jnp.transpose(gmlp_b[l]), da // G_A, axis=1)

        def gmlp_wm(rows, wm_full=wm_full):
            reps = GMLP_CHUNK // rows
            blk = wm_full[:, :rows, :rows]
            eye = jnp.eye(reps, dtype=F32)
            return jnp.einsum("ab,gij->gaibj", eye, blk).reshape(G_A, GMLP_CHUNK, GMLP_CHUNK).astype(BF16)

        def gmlp_bias(rows, bias_rows=bias_rows):
            return jnp.tile(bias_rows[:rows], (GMLP_CHUNK // rows, 1))

        w = dict(
            norm_attn_g=norm_attn_g[l], gmlp_norm_g=gmlp_norm_g[l], norm_ffn_g=norm_ffn_g[l],
            gmlp_wm=gmlp_wm, gmlp_bias=gmlp_bias,
            a=w_branch_a[l].astype(BF16), b=w_branch_b[l].astype(BF16), o=w_out[l].astype(BF16),
            up=w_up[l].astype(BF16), down=w_down[l].astype(BF16), conv_w=conv_w[l], conv_b=conv_b[l],
        )
        w["in"] = _split_w_in(w_in[l], d)

        xp, kp, vp, kip, cp, _ = _layer(xp, pos_p, s, w, None, jnp.zeros((bsz, CONV_WIDTH - 1, f2), F32),
                                        False)
        past = (cache_k[l], cache_v[l], cache_kidx[l])
        xs, ks, vs, kis, cs, gv = _layer(xs, pos_s, t, w, past, state_ffn_conv[l], True)
        for lst, val in zip(outs, (kp, vp, kip, cp, ks, vs, kis, cs, gv.reshape(dbsz, t, da))):
            lst.append(val)

    y_prompt = rmsnorm(xp.reshape(bsz * s, d), norm_final_g, F32).reshape(bsz, s, d)
    y_sample = rmsnorm(xs.reshape(dbsz * t, d), norm_final_g, F32).reshape(dbsz, t, d)
    return (y_prompt, y_sample) + tuple(jnp.stack(o) for o in outs)
```

```python
import functools

import jax
import jax.numpy as jnp
from jax import lax
from jax.experimental import pallas as pl
from jax.experimental.pallas import tpu as pltpu

CHUNK = 64
GMLP_CHUNK = 128
G_A = 8
HEAD_DIM = 128
Q_PER_KV = 4
N_IDX_HEADS = 16
IDX_DIM = 64
TOPK_MAX = 256
ROPE_THETA = 500000.0
CONV_WIDTH = 3
EPS = 1e-6

LANE = 128
SUBLANE = 8
VMEM_LIMIT_BYTES = 56 * 1024 * 1024

FFN_UP_ROW_SPLITS = 4
DSA_PREFIX_CLASSES = 4

BF16 = jnp.bfloat16
F32 = jnp.float32
NT_DIMS = (((1,), (1,)), ((), ()))


def _params(*semantics):
    return pltpu.CompilerParams(dimension_semantics=semantics, vmem_limit_bytes=VMEM_LIMIT_BYTES)


def _tile(n, pref, align):
    if n <= pref:
        return n
    t = (pref // align) * align
    while t >= align:
        if n % t == 0:
            return t
        t -= align
    raise ValueError(f"no {align}-aligned tile of {n} below {pref}")


def _first_row_tile():
    return pl.program_id(1) == 0


def _rmsnorm_kernel(x_ref, g_ref, o_ref):
    x = x_ref[...]
    ms = jnp.mean(x * x, axis=-1, keepdims=True)
    o_ref[...] = (x * lax.rsqrt(ms + EPS) * g_ref[...]).astype(o_ref.dtype)


def rmsnorm(x, g, out_dtype):
    n, d = x.shape
    tr = _tile(n, 256, SUBLANE)
    return pl.pallas_call(
        _rmsnorm_kernel,
        grid=(n // tr,),
        in_specs=[pl.BlockSpec((tr, d), lambda i: (i, 0)), pl.BlockSpec((1, d), lambda i: (0, 0))],
        out_specs=pl.BlockSpec((tr, d), lambda i: (i, 0)),
        out_shape=jax.ShapeDtypeStruct((n, d), out_dtype),
        compiler_params=_params("parallel"),
        name="rmsnorm",
    )(x, g.reshape(1, d))


def _rope_slab(x, c, s1, s2, shift):
    return x * c + pltpu.roll(x, LANE - shift, 1) * s1 + pltpu.roll(x, shift, 1) * s2


def _proj_kernel(a_ref, b_ref, *rest, rope_shift):
    if rope_shift:
        c_ref, s1_ref, s2_ref, o_ref, bq_ref = rest
    else:
        o_ref, bq_ref = rest

    @pl.when(_first_row_tile())
    def _():
        bq_ref[...] = b_ref[...].astype(BF16)

    acc = jnp.dot(a_ref[...], bq_ref[...], preferred_element_type=F32)
    if rope_shift:
        c, s1, s2 = c_ref[...], s1_ref[...], s2_ref[...]
        for h in range(acc.shape[1] // LANE):
            sl = slice(h * LANE, (h + 1) * LANE)
            o_ref[:, sl] = _rope_slab(acc[:, sl], c, s1, s2, rope_shift).astype(o_ref.dtype)
    else:
        o_ref[...] = acc.astype(o_ref.dtype)


def project(a, w, col0, ncols, out_dtype, rope=None):
    m, k = a.shape
    tm = _tile(m, 1024, SUBLANE)
    tn = _tile(ncols, 512, LANE)
    assert col0 % tn == 0
    j0 = col0 // tn
    in_specs = [pl.BlockSpec((tm, k), lambda j, i: (i, 0)), pl.BlockSpec((k, tn), lambda j, i: (0, j + j0))]
    args = [a, w]
    shift = 0
    if rope is not None:
        shift, tabs = rope[0], rope[1:]
        reps = tabs[0].shape[0] // tm
        assert reps * tm == tabs[0].shape[0]
        for t in tabs:
            in_specs.append(pl.BlockSpec((tm, LANE), lambda j, i: (i % reps, 0)))
            args.append(t)
    return pl.pallas_call(
        functools.partial(_proj_kernel, rope_shift=shift),
        grid=(ncols // tn, m // tm),
        in_specs=in_specs,
        out_specs=pl.BlockSpec((tm, tn), lambda j, i: (i, j)),
        out_shape=jax.ShapeDtypeStruct((m, ncols), out_dtype),
        scratch_shapes=[pltpu.VMEM((k, tn), BF16)],
        compiler_params=_params("arbitrary", "arbitrary"),
        name="project",
    )(*args)


def _rope_tables(pos, head_dim, rot_dim, rows):
    half = rot_dim // 2
    inv_freq = ROPE_THETA ** (-jnp.arange(half, dtype=F32) / half)
    ang = pos.astype(F32)[:, None] * inv_freq[None, :]
    cos, sin = jnp.cos(ang), jnp.sin(ang)
    t = pos.shape[0]
    zeros_h = jnp.zeros((t, half), F32)
    rest0 = jnp.zeros((t, head_dim - rot_dim), F32)
    c = jnp.concatenate([cos, cos, jnp.ones((t, head_dim - rot_dim), F32)], axis=1)
    s1 = jnp.concatenate([-sin, zeros_h, rest0], axis=1)
    s2 = jnp.concatenate([zeros_h, sin, rest0], axis=1)
    reps_l = LANE // head_dim
    reps_r = rows // t
    return tuple(jnp.tile(x, (reps_r, reps_l)) for x in (c, s1, s2))


def _gmlp_kernel(u_ref, v_ref, g_ref, wm_ref, b_ref, a_ref, *vn_out, groups):
    v = v_ref[...]
    vn = v * lax.rsqrt(jnp.mean(v * v, axis=-1, keepdims=True) + EPS) * g_ref[...]
    if vn_out:
        vn_out[0][...] = vn
    vb = vn.astype(BF16)
    gw = v.shape[1] // groups
    for g in range(groups):
        sl = slice(g * gw, (g + 1) * gw)
        s = jnp.dot(wm_ref[g], vb[:, sl], preferred_element_type=F32) + b_ref[:, sl]
        a_ref[:, sl] = (u_ref[:, sl] * s).astype(a_ref.dtype)


def gmlp(uv, g_norm, wm, bias, want_vn):
    m, d2 = uv.shape
    da = d2 // 2
    tc = GMLP_CHUNK
    out_shape = [jax.ShapeDtypeStruct((m, da), BF16)]
    out_specs = [pl.BlockSpec((tc, da), lambda i: (i, 0))]
    if want_vn:
        out_shape.append(jax.ShapeDtypeStruct((m, da), F32))
        out_specs.append(pl.BlockSpec((tc, da), lambda i: (i, 0)))
    res = pl.pallas_call(
        functools.partial(_gmlp_kernel, groups=wm.shape[0]),
        grid=(m // tc,),
        in_specs=[
            pl.BlockSpec((tc, da), lambda i: (i, 0)),
            pl.BlockSpec((tc, da), lambda i: (i, 1)),
            pl.BlockSpec((1, da), lambda i: (0, 0)),
            pl.BlockSpec(wm.shape, lambda i: (0, 0, 0)),
            pl.BlockSpec((tc, da), lambda i: (0, 0)),
        ],
        out_specs=out_specs,
        out_shape=out_shape,
        compiler_params=_params("parallel"),
        name="gmlp",
    )(uv, uv, g_norm.reshape(1, da), wm, bias)
    return res if want_vn else (res[0], None)


def _dsa_select_attend(qi_ref, kwq_ref, q_ref, klo_ref, khi_ref, ks_ref, vs_ref, o_ref, bias_ref, *,
                       s_c, tq, n_kv, topk, row0, n_valid, scale):
    n_tiles = s_c // LANE
    w = kwq_ref[:, IDX_DIM:IDX_DIM + N_IDX_HEADS]
    klo = klo_ref[0:s_c, :]
    khi = khi_ref[0:s_c, :]

    score = jnp.zeros((tq, s_c), F32)
    for p in range(N_IDX_HEADS // 2):
        qp = qi_ref[:, p * LANE:(p + 1) * LANE]
        rel0 = lax.dot_general(qp, klo, NT_DIMS, preferred_element_type=F32)
        rel1 = lax.dot_general(qp, khi, NT_DIMS, preferred_element_type=F32)
        score = (score + jnp.maximum(rel0, 0.0) * w[:, 2 * p:2 * p + 1]
                 + jnp.maximum(rel1, 0.0) * w[:, 2 * p + 1:2 * p + 2])

    kpos = lax.broadcasted_iota(jnp.int32, (tq, s_c), 1)
    if row0 is not None:
        row = lax.broadcasted_iota(jnp.int32, (tq, s_c), 0) + row0
        adm = kpos < (jnp.right_shift(row, CHUNK.bit_length() - 1) + 1) * CHUNK
    else:
        adm = kpos < n_valid
    score = jnp.where(adm, score, -jnp.inf)

    kf = float(topk)

    def count_ge(t):
        return jnp.sum(jnp.where(score >= t, 1.0, 0.0), axis=1, keepdims=True)

    smax = jnp.max(score, axis=1, keepdims=True)
    smin = jnp.min(jnp.where(adm, score, jnp.inf), axis=1, keepdims=True)
    c_max = count_ge(smax)
    c_min = count_ge(smin)
    big = c_max > kf
    lo0 = jnp.where(big, smax, smin)
    hi0 = jnp.where(big, jnp.inf, smax)
    c0 = jnp.where(big, c_max, c_min)

    def active_of(lo, hi, c_lo):
        mid = 0.5 * lo + 0.5 * hi
        return mid, (c_lo > kf) & (mid > lo) & (mid < hi)

    def cond(carry):
        _, act = active_of(*carry)
        return jnp.max(jnp.where(act, 1.0, 0.0)) > 0.0

    def body(carry):
        lo, hi, c_lo = carry
        mid, act = active_of(lo, hi, c_lo)
        cnt = count_ge(mid)
        take = act & (cnt >= kf)
        drop = act & (cnt < kf)
        return jnp.where(take, mid, lo), jnp.where(drop, mid, hi), jnp.where(take, cnt, c_lo)

    lo, hi, c_lo = lax.while_loop(cond, body, (lo0, hi0, c0))

    bias_ref[:, 0:s_c] = jnp.where(score >= lo, 0.0, -jnp.inf)
    tie_row = c_lo > kf

    @pl.when(jnp.max(jnp.where(tie_row, 1.0, 0.0)) > 0.0)
    def _():
        quota = kf - count_ge(hi)
        tri = (lax.broadcasted_iota(jnp.int32, (LANE, LANE), 0)
               <= lax.broadcasted_iota(jnp.int32, (LANE, LANE), 1)).astype(F32).astype(BF16)
        before = jnp.zeros((tq, 1), F32)
        for jt in range(n_tiles):
            sl = slice(jt * LANE, (jt + 1) * LANE)
            sc = score[:, sl]
            cand = jnp.where((sc >= lo) & (sc < hi), 1.0, 0.0)
            rank = jnp.dot(cand.astype(BF16), tri, preferred_element_type=F32) + before
            keep = (sc >= hi) | ((cand > 0.0) & (rank <= quota))
            bias_ref[:, sl] = jnp.where(tie_row, jnp.where(keep, 0.0, -jnp.inf),
                                        jnp.where(sc >= lo, 0.0, -jnp.inf))
            before = before + jnp.sum(cand, axis=1, keepdims=True)

    bias = bias_ref[:, 0:s_c]
    bias_g = jnp.concatenate([bias] * Q_PER_KV, axis=0)
    for n in range(n_kv):
        qn = jnp.concatenate(
            [q_ref[:, (n * Q_PER_KV + g) * HEAD_DIM:(n * Q_PER_KV + g + 1) * HEAD_DIM]
             for g in range(Q_PER_KV)], axis=0)
        ksl = slice(n * HEAD_DIM, (n + 1) * HEAD_DIM)
        logits = lax.dot_general(qn, ks_ref[0:s_c, ksl], NT_DIMS, preferred_element_type=F32) * scale + bias_g
        mx = jnp.max(logits, axis=1, keepdims=True)
        p = jnp.exp(logits - mx)
        den = jnp.sum(p, axis=1, keepdims=True)
        o = jnp.dot(p.astype(BF16), vs_ref[0:s_c, ksl], preferred_element_type=F32) / den
        for g in range(Q_PER_KV):
            hsl = slice((n * Q_PER_KV + g) * HEAD_DIM, (n * Q_PER_KV + g + 1) * HEAD_DIM)
            o_ref[:, hsl] = o[g * tq:(g + 1) * tq].astype(o_ref.dtype)


def _dsa_kernel(qi_ref, kwq_ref, q_ref, kw_ref, k_ref, v_ref, *rest, tq, t_new, p_len, n_classes, topk):
    if p_len:
        pki_ref, pk_ref, pv_ref, o_ref, klo_ref, khi_ref, ks_ref, vs_ref, bias_ref, tmp_ref = rest
    else:
        o_ref, klo_ref, khi_ref, ks_ref, vs_ref, bias_ref = rest
    s_pad = ks_ref.shape[0]
    n_valid = p_len + t_new
    n_kv = ks_ref.shape[1] // HEAD_DIM
    j = pl.program_id(1)

    @pl.when(j == 0)
    def _():
        lane = lax.broadcasted_iota(jnp.int32, (t_new, LANE), 1)
        new_lo = jnp.where(lane < IDX_DIM, kw_ref[...], 0.0)
        if p_len:
            tmp_ref[...] = jnp.zeros(tmp_ref.shape, F32)
            tmp_ref[0:p_len, 0:IDX_DIM] = pki_ref[...]
            tmp_ref[p_len:n_valid, :] = new_lo
            lo = tmp_ref[...]
            for dst, past, new in ((ks_ref, pk_ref, k_ref), (vs_ref, pv_ref, v_ref)):
                dst[0:p_len, :] = past[...].astype(BF16)
                dst[p_len:n_valid, :] = new[...].astype(BF16)
                if s_pad > n_valid:
                    dst[n_valid:s_pad, :] = jnp.zeros((s_pad - n_valid, dst.shape[1]), BF16)
        else:
            lo = new_lo
            ks_ref[...] = k_ref[...].astype(BF16)
            vs_ref[...] = v_ref[...].astype(BF16)
        klo_ref[...] = lo.astype(BF16)
        khi_ref[...] = pltpu.roll(lo, IDX_DIM, 1).astype(BF16)

    common = dict(tq=tq, n_kv=n_kv, topk=topk, n_valid=n_valid, scale=HEAD_DIM ** -0.5)
    refs = (qi_ref, kwq_ref, q_ref, klo_ref, khi_ref, ks_ref, vs_ref, o_ref, bias_ref)
    if p_len:
        _dsa_select_attend(*refs, s_c=s_pad, row0=None, **common)
    else:
        per_class = pl.num_programs(1) // n_classes
        for c in range(n_classes):
            @pl.when(j // per_class == c)
            def _(c=c):
                _dsa_select_attend(*refs, s_c=(c + 1) * (s_pad // n_classes), row0=j * tq, **common)


def dsa(qi, kw, q, k, v, past, *, tq, topk):
    b, t, dq = q.shape
    dkv = k.shape[2]
    nq = t // tq
    row = lambda i, j: (i, j, 0)
    full = lambda i, j: (i, 0, 0)
    in_specs = [
        pl.BlockSpec((None, tq, qi.shape[2]), row),
        pl.BlockSpec((None, tq, LANE), row),
        pl.BlockSpec((None, tq, dq), row),
        pl.BlockSpec((None, t, LANE), full),
        pl.BlockSpec((None, t, dkv), full),
        pl.BlockSpec((None, t, dkv), full),
    ]
    args = [qi, kw, q, kw, k, v]
    if past is None:
        p_len = 0
        s_pad = t
        n_classes = DSA_PREFIX_CLASSES if (nq % DSA_PREFIX_CLASSES == 0
                                           and t % (DSA_PREFIX_CLASSES * LANE) == 0) else 1
        assert tq == CHUNK
    else:
        layer, pki, pk, pv = past
        p_len = pk.shape[2]
        s_pad = -(-(p_len + t) // LANE) * LANE
        n_classes = 1
        assert nq == 1
        cache = lambda i, j: (layer, i, 0, 0)
        in_specs += [pl.BlockSpec((None, None, p_len, IDX_DIM), cache),
                     pl.BlockSpec((None, None, p_len, dkv), cache),
                     pl.BlockSpec((None, None, p_len, dkv), cache)]
        args += [pki, pk, pv]
    scratch = [pltpu.VMEM((s_pad, LANE), BF16), pltpu.VMEM((s_pad, LANE), BF16),
               pltpu.VMEM((s_pad, dkv), BF16), pltpu.VMEM((s_pad, dkv), BF16),
               pltpu.VMEM((tq, s_pad), F32)]
    if past is not None:
        scratch.append(pltpu.VMEM((s_pad, LANE), F32))
    kern = functools.partial(_dsa_kernel, tq=tq, t_new=t, p_len=p_len, n_classes=n_classes, topk=topk)
    return pl.pallas_call(
        kern,
        grid=(b, nq),
        in_specs=in_specs,
        out_specs=pl.BlockSpec((None, tq, dq), row),
        out_shape=jax.ShapeDtypeStruct((b, t, dq), BF16),
        scratch_shapes=scratch,
        compiler_params=_params("arbitrary", "arbitrary"),
        name="dsa",
    )(*args)


def _merge_kernel(a_ref, b_ref, wa_ref, wb_ref, ga_ref, gb_ref, o_ref, waq_ref, wbq_ref):
    @pl.when(_first_row_tile())
    def _():
        waq_ref[...] = wa_ref[...].astype(BF16)
        wbq_ref[...] = wb_ref[...].astype(BF16)

    ya = jnp.dot(a_ref[...], waq_ref[...], preferred_element_type=F32)
    yb = jnp.dot(b_ref[...], wbq_ref[...], preferred_element_type=F32)
    y = jax.nn.sigmoid(ga_ref[...]) * ya + jax.nn.sigmoid(gb_ref[...]) * yb
    o_ref[...] = y.astype(o_ref.dtype)


def merge(a, b, wa, wb, gates):
    m, ka = a.shape
    kb = b.shape[1]
    d = wa.shape[1]
    tm = _tile(m, 1024, SUBLANE)
    tn = _tile(d, 512, LANE)
    nj = d // tn
    return pl.pallas_call(
        _merge_kernel,
        grid=(nj, m // tm),
        in_specs=[
            pl.BlockSpec((tm, ka), lambda j, i: (i, 0)),
            pl.BlockSpec((tm, kb), lambda j, i: (i, 0)),
            pl.BlockSpec((ka, tn), lambda j, i: (0, j)),
            pl.BlockSpec((kb, tn), lambda j, i: (0, j)),
            pl.BlockSpec((tm, tn), lambda j, i: (i, j)),
            pl.BlockSpec((tm, tn), lambda j, i: (i, j + nj)),
        ],
        out_specs=pl.BlockSpec((tm, tn), lambda j, i: (i, j)),
        out_shape=jax.ShapeDtypeStruct((m, d), BF16),
        scratch_shapes=[pltpu.VMEM((ka, tn), BF16), pltpu.VMEM((kb, tn), BF16)],
        compiler_params=_params("arbitrary", "arbitrary"),
        name="merge",
    )(a, b, wa, wb, gates, gates)


def _matmul_res_kernel(a_ref, b_ref, r_ref, o_ref, bq_ref):
    @pl.when(_first_row_tile())
    def _():
        bq_ref[...] = b_ref[...].astype(BF16)

    o_ref[...] = r_ref[...] + jnp.dot(a_ref[...], bq_ref[...], preferred_element_type=F32)


def matmul_residual(a, b, res):
    m, k = a.shape
    n = b.shape[1]
    tm = _tile(m, 1024, SUBLANE)
    tn = _tile(n, 512, LANE)
    return pl.pallas_call(
        _matmul_res_kernel,
        grid=(n // tn, m // tm),
        in_specs=[
            pl.BlockSpec((tm, k), lambda j, i: (i, 0)),
            pl.BlockSpec((k, tn), lambda j, i: (0, j)),
            pl.BlockSpec((tm, tn), lambda j, i: (i, j)),
        ],
        out_specs=pl.BlockSpec((tm, tn), lambda j, i: (i, j)),
        out_shape=jax.ShapeDtypeStruct((m, n), F32),
        scratch_shapes=[pltpu.VMEM((k, tn), BF16)],
        compiler_params=_params("arbitrary", "arbitrary"),
        name="matmul_residual",
    )(a, b, res)


def _ffn_up_kernel(h_ref, wg_ref, wu_ref, cwg_ref, cwu_ref, cbg_ref, cbu_ref, sg_ref, su_ref,
                   act_ref, zg_ref, zu_ref, wq_ref, buf_ref, *, seq_len, tiles_per_seq, n_sub):
    tm = h_ref.shape[0]
    tn = act_ref.shape[1]
    pad = SUBLANE

    @pl.when(_first_row_tile())
    def _():
        wq_ref[:, 0:tn] = wg_ref[...].astype(BF16)
        wq_ref[:, tn:2 * tn] = wu_ref[...].astype(BF16)

    cw = jnp.concatenate([cwg_ref[...], cwu_ref[...]], axis=1)
    cb = jnp.concatenate([cbg_ref[...], cbu_ref[...]], axis=1)

    def gated(c):
        gate, up = c[:, :tn], c[:, tn:]
        return (gate * jax.nn.sigmoid(gate) * up).astype(act_ref.dtype)

    if tiles_per_seq >= 1:
        @pl.when(pl.program_id(1) % tiles_per_seq == 0)
        def _():
            buf_ref[pad - 2:pad, 0:tn] = sg_ref[0]
            buf_ref[pad - 2:pad, tn:2 * tn] = su_ref[0]
        rs = tm // n_sub
        zs = [jnp.dot(h_ref[r * rs:(r + 1) * rs, :], wq_ref[...], preferred_element_type=F32)
              for r in range(n_sub)]
        for r, z in enumerate(zs):
            base = pad + r * rs
            buf_ref[base:base + rs, :] = z
            c = (cb + cw[0:1] * buf_ref[base - 2:base - 2 + rs, :]
                 + cw[1:2] * buf_ref[base - 1:base - 1 + rs, :] + cw[2:3] * z)
            act_ref[r * rs:(r + 1) * rs, :] = gated(c)
        tail = buf_ref[pad + tm - 2:pad + tm, :]
        zg_ref[0] = tail[:, :tn]
        zu_ref[0] = tail[:, tn:]
        buf_ref[pad - 2:pad, :] = tail
    else:
        buf_ref[0:pad, :] = jnp.zeros((pad, 2 * tn), F32)
        buf_ref[pad:pad + tm, :] = jnp.dot(h_ref[...], wq_ref[...], preferred_element_type=F32)
        rowi = lax.broadcasted_iota(jnp.int32, (seq_len, 2 * tn), 0)
        for s in range(tm // seq_len):
            base = pad + s * seq_len
            st0 = jnp.concatenate([sg_ref[s, 0:1, :], su_ref[s, 0:1, :]], axis=1)
            st1 = jnp.concatenate([sg_ref[s, 1:2, :], su_ref[s, 1:2, :]], axis=1)
            z0 = buf_ref[base:base + seq_len, :]
            p1 = jnp.where(rowi == 0, st1, buf_ref[base - 1:base - 1 + seq_len, :])
            p2 = jnp.where(rowi == 0, st0, jnp.where(rowi == 1, st1, buf_ref[base - 2:base - 2 + seq_len, :]))
            act_ref[s * seq_len:(s + 1) * seq_len, :] = gated(cb + cw[0:1] * p2 + cw[1:2] * p1 + cw[2:3] * z0)
            tail = buf_ref[base + seq_len - 2:base + seq_len, :]
            zg_ref[s] = tail[:, :tn]
            zu_ref[s] = tail[:, tn:]


def ffn_up(h, w_up, conv_w, conv_b, state, seq_len):
    m, d = h.shape
    f = w_up.shape[1] // 2
    tm = _tile(m, 1024, SUBLANE)
    tn = _tile(f, 256, LANE)
    nj = f // tn
    if seq_len >= tm:
        assert seq_len % tm == 0
        tiles_per_seq = seq_len // tm
        n_state = 1
        state_map_g = lambda j, i: (i // tiles_per_seq, 0, j)
        state_map_u = lambda j, i: (i // tiles_per_seq, 0, j + nj)
        n_last = m // tm
    else:
        assert tm % seq_len == 0 and seq_len % SUBLANE == 0
        tiles_per_seq = 0
        n_state = tm // seq_len
        state_map_g = lambda j, i: (i, 0, j)
        state_map_u = lambda j, i: (i, 0, j + nj)
        n_last = m // seq_len
    kern = functools.partial(_ffn_up_kernel, seq_len=seq_len, tiles_per_seq=tiles_per_seq,
                             n_sub=FFN_UP_ROW_SPLITS)
    cb = conv_b.reshape(1, 2 * f)
    act, zg, zu = pl.pallas_call(
        kern,
        grid=(nj, m // tm),
        in_specs=[
            pl.BlockSpec((tm, d), lambda j, i: (i, 0)),
            pl.BlockSpec((d, tn), lambda j, i: (0, j)),
            pl.BlockSpec((d, tn), lambda j, i: (0, j + nj)),
            pl.BlockSpec((CONV_WIDTH, tn), lambda j, i: (0, j)),
            pl.BlockSpec((CONV_WIDTH, tn), lambda j, i: (0, j + nj)),
            pl.BlockSpec((1, tn), lambda j, i: (0, j)),
            pl.BlockSpec((1, tn), lambda j, i: (0, j + nj)),
            pl.BlockSpec((n_state, 2, tn), state_map_g),
            pl.BlockSpec((n_state, 2, tn), state_map_u),
        ],
        out_specs=[
            pl.BlockSpec((tm, tn), lambda j, i: (i, j)),
            pl.BlockSpec((n_state, 2, tn), lambda j, i: (i, 0, j)),
            pl.BlockSpec((n_state, 2, tn), lambda j, i: (i, 0, j)),
        ],
        out_shape=[
            jax.ShapeDtypeStruct((m, f), BF16),
            jax.ShapeDtypeStruct((n_last, 2, f), F32),
            jax.ShapeDtypeStruct((n_last, 2, f), F32),
        ],
        scratch_shapes=[pltpu.VMEM((d, 2 * tn), BF16), pltpu.VMEM((tm + SUBLANE, 2 * tn), F32)],
        compiler_params=_params("arbitrary", "arbitrary"),
        name="ffn_up",
    )(h, w_up, w_up, conv_w, conv_w, cb, cb, state, state)
    zlast = jnp.concatenate([zg, zu], axis=-1)
    if tiles_per_seq > 1:
        zlast = zlast[tiles_per_seq - 1::tiles_per_seq]
    return act, zlast


def _ffn_down_kernel(a_ref, b_ref, r_ref, o_ref, acc_ref, *, nk):
    kk = pl.program_id(2)
    part = jnp.dot(a_ref[...], b_ref[...], preferred_element_type=F32)

    @pl.when(kk == 0)
    def _():
        acc_ref[...] = r_ref[...] + part

    @pl.when((kk > 0) & (kk < nk - 1))
    def _():
        acc_ref[...] = acc_ref[...] + part

    @pl.when(kk == nk - 1)
    def _():
        if nk == 1:
            o_ref[...] = r_ref[...] + part
        else:
            o_ref[...] = acc_ref[...] + part


def ffn_down(a, b, res):
    m, k = a.shape
    n = b.shape[1]
    tm = _tile(m, 1024, SUBLANE)
    tn = _tile(n, 512, LANE)
    tk = _tile(k, 5632, LANE)
    nk = k // tk
    return pl.pallas_call(
        functools.partial(_ffn_down_kernel, nk=nk),
        grid=(m // tm, n // tn, nk),
        in_specs=[
            pl.BlockSpec((tm, tk), lambda i, j, kk: (i, kk)),
            pl.BlockSpec((tk, tn), lambda i, j, kk: (kk, j)),
            pl.BlockSpec((tm, tn), lambda i, j, kk: (i, j)),
        ],
        out_specs=pl.BlockSpec((tm, tn), lambda i, j, kk: (i, j)),
        out_shape=jax.ShapeDtypeStruct((m, n), F32),
        scratch_shapes=[pltpu.VMEM((tm, tn), F32)],
        compiler_params=_params("parallel", "arbitrary", "arbitrary"),
        name="ffn_down",
    )(a, b, res)


def _in_offsets(d):
    da = d // 2
    n_heads = d // 256
    n_kv = n_heads // Q_PER_KV
    sizes = (da, da, n_heads * HEAD_DIM, n_kv * HEAD_DIM, n_kv * HEAD_DIM,
             N_IDX_HEADS * IDX_DIM, IDX_DIM, N_IDX_HEADS, d, d)
    offs = [0]
    for s in sizes:
        offs.append(offs[-1] + s)
    return offs


def _layer(x, pos, seq_len, w, past, conv_state, want_vn):
    bsz, t, d = x.shape
    m = bsz * t
    x2 = x.reshape(m, d)
    n_heads = d // 256
    n_kv = n_heads // Q_PER_KV
    offs = _in_offsets(d)
    w_in = w["in"]
    assert offs[-1] == w_in.shape[1] and offs[8] - offs[6] <= LANE

    h = rmsnorm(x2, w["norm_attn_g"], BF16)

    rows = max(t, _tile(m, 1024, SUBLANE))
    rope_h = (HEAD_DIM // 8,) + _rope_tables(pos, HEAD_DIM, HEAD_DIM // 4, rows)
    rope_i = (IDX_DIM // 8,) + _rope_tables(pos, IDX_DIM, IDX_DIM // 4, rows)
    ci, s1i, s2i = rope_i[1:]
    lane = jnp.arange(LANE)
    is_w = (lane >= IDX_DIM) & (lane < IDX_DIM + N_IDX_HEADS)
    w_scale = N_IDX_HEADS ** -0.5 * IDX_DIM ** -0.5
    rope_kw = (IDX_DIM // 8,
               jnp.where(is_w, w_scale, jnp.where(lane < IDX_DIM, ci, 1.0)).astype(F32),
               jnp.where(lane < IDX_DIM, s1i, 0.0), jnp.where(lane < IDX_DIM, s2i, 0.0))

    uv = project(h, w_in, offs[0], offs[2] - offs[0], F32)
    q = project(h, w_in, offs[2], offs[3] - offs[2], BF16, rope_h)
    k = project(h, w_in, offs[3], offs[4] - offs[3], F32, rope_h)
    v = project(h, w_in, offs[4], offs[5] - offs[4], F32)
    qi = project(h, w_in, offs[5], offs[6] - offs[5], BF16, rope_i)
    kw = project(h, w_in, offs[6], LANE, F32, rope_kw)
    gates = project(h, w["in_gates"], 0, 2 * d, F32)

    chunk_rows = min(t, GMLP_CHUNK)
    a_out, vn = gmlp(uv, w["gmlp_norm_g"], w["gmlp_wm"](chunk_rows), w["gmlp_bias"](chunk_rows), want_vn)

    n_valid = t if past is None else past[2].shape[2] + t
    b_out = dsa(qi.reshape(bsz, t, -1), kw.reshape(bsz, t, LANE), q.reshape(bsz, t, -1),
                k.reshape(bsz, t, -1), v.reshape(bsz, t, -1), past,
                tq=CHUNK if past is None else t, topk=min(TOPK_MAX, n_valid // 4))

    y = merge(a_out, b_out.reshape(m, n_heads * HEAD_DIM), w["a"], w["b"], gates)
    x2 = matmul_residual(y, w["o"], x2)

    hf = rmsnorm(x2, w["norm_ffn_g"], BF16)
    act, zlast = ffn_up(hf, w["up"], w["conv_w"], w["conv_b"], conv_state, seq_len)
    x2 = ffn_down(act, w["down"], x2)
    kidx = kw[:, :IDX_DIM].reshape(bsz, t, IDX_DIM)
    return (x2.reshape(bsz, t, d), k.reshape(bsz, t, n_kv, HEAD_DIM), v.reshape(bsz, t, n_kv, HEAD_DIM),
            kidx, zlast, vn)


def kernel(x_prompt, x_sample, cache_k, cache_v, cache_kidx, state_ffn_conv, norm_attn_g, w_in, gmlp_norm_g, gmlp_ws, gmlp_b, w_branch_a, w_branch_b, w_out, norm_ffn_g, w_up, conv_w, conv_b, w_down, norm_final_g):
    bsz, s, d = x_prompt.shape
    dbsz, t, _ = x_sample.shape
    depth = w_in.shape[0]
    p_len = cache_k.shape[2]
    da = d // 2
    f2 = w_up.shape[2]
    assert s % GMLP_CHUNK == 0 and GMLP_CHUNK % t == 0 and s % CHUNK == 0

    pos_p = jnp.arange(s, dtype=jnp.int32)
    pos_s = p_len + jnp.arange(t, dtype=jnp.int32)
    ci = jnp.arange(GMLP_CHUNK)
    chunk_mask = (ci[None, :] // CHUNK) <= (ci[:, None] // CHUNK)
    cache_k4 = cache_k.reshape(depth, dbsz, p_len, -1)
    cache_v4 = cache_v.reshape(depth, dbsz, p_len, -1)
    gate_col0 = _in_offsets(d)[8]

    xp, xs = x_prompt, x_sample
    outs = [[] for _ in range(9)]
    for l in range(depth):
        wm_full = jnp.where(chunk_mask[None], gmlp_ws[l], 0.0)
        bias_rows = jnp.repeat(jnp.transpose(gmlp_b[l]), da // G_A, axis=1)

        def gmlp_wm(rows, wm_full=wm_full):
            reps = GMLP_CHUNK // rows
            blk = wm_full[:, :rows, :rows]
            eye = jnp.eye(reps, dtype=F32)
            return jnp.einsum("ab,gij->gaibj", eye, blk).reshape(G_A, GMLP_CHUNK, GMLP_CHUNK).astype(BF16)

        def gmlp_bias(rows, bias_rows=bias_rows):
            return jnp.tile(bias_rows[:rows], (GMLP_CHUNK // rows, 1))

        w = dict(
            norm_attn_g=norm_attn_g[l], gmlp_norm_g=gmlp_norm_g[l], norm_ffn_g=norm_ffn_g[l],
            gmlp_wm=gmlp_wm, gmlp_bias=gmlp_bias,
            a=w_branch_a[l], b=w_branch_b[l], o=w_out[l], up=w_up[l], conv_w=conv_w[l], conv_b=conv_b[l],
            down=w_down[l].astype(BF16),
        )
        w["in"] = w_in[l]
        w["in_gates"] = w_in[l][:, gate_col0:]

        xp, kp, vp, kip, cp, _ = _layer(xp, pos_p, s, w, None, jnp.zeros((bsz, CONV_WIDTH - 1, f2), F32),
                                        False)
        past = (l, cache_kidx, cache_k4, cache_v4)
        xs, ks, vs, kis, cs, gv = _layer(xs, pos_s, t, w, past, state_ffn_conv[l], True)
        for lst, val in zip(outs, (kp, vp, kip, cp, ks, vs, kis, cs, gv.reshape(dbsz, t, da))):
            lst.append(val)

    y_prompt = rmsnorm(xp.reshape(bsz * s, d), norm_final_g, F32).reshape(bsz, s, d)
    y_sample = rmsnorm(xs.reshape(dbsz * t, d), norm_final_g, F32).reshape(dbsz, t, d)
    stack = lambda o: o[0][None] if depth == 1 else jnp.stack(o)
    return (y_prompt, y_sample) + tuple(stack(o) for o in outs)
```

```python
import functools

import jax
import jax.numpy as jnp
from jax import lax
from jax.experimental import pallas as pl
from jax.experimental.pallas import tpu as pltpu

CHUNK = 64
GMLP_CHUNK = 128
G_A = 8
HEAD_DIM = 128
Q_PER_KV = 4
N_IDX_HEADS = 16
IDX_DIM = 64
TOPK_MAX = 256
ROPE_THETA = 500000.0
CONV_WIDTH = 3
EPS = 1e-6

LANE = 128
SUBLANE = 8
VMEM_LIMIT_BYTES = 56 * 1024 * 1024

FFN_UP_ROW_SPLITS = 4
DSA_PREFIX_CLASSES = 4

BF16 = jnp.bfloat16
F32 = jnp.float32
NT_DIMS = (((1,), (1,)), ((), ()))


def _params(*semantics):
    return pltpu.CompilerParams(dimension_semantics=semantics, vmem_limit_bytes=VMEM_LIMIT_BYTES)


def _tile(n, pref, align):
    if n <= pref:
        return n
    t = (pref // align) * align
    while t >= align:
        if n % t == 0:
            return t
        t -= align
    raise ValueError(f"no {align}-aligned tile of {n} below {pref}")


def _first_row_tile():
    return pl.program_id(1) == 0


def _rmsnorm_kernel(x_ref, g_ref, o_ref):
    x = x_ref[...]
    ms = jnp.mean(x * x, axis=-1, keepdims=True)
    o_ref[...] = (x * lax.rsqrt(ms + EPS) * g_ref[...]).astype(o_ref.dtype)


def rmsnorm(x, g, out_dtype):
    n, d = x.shape
    tr = _tile(n, 256, SUBLANE)
    return pl.pallas_call(
        _rmsnorm_kernel,
        grid=(n // tr,),
        in_specs=[pl.BlockSpec((tr, d), lambda i: (i, 0)), pl.BlockSpec((1, d), lambda i: (0, 0))],
        out_specs=pl.BlockSpec((tr, d), lambda i: (i, 0)),
        out_shape=jax.ShapeDtypeStruct((n, d), out_dtype),
        compiler_params=_params("parallel"),
        name="rmsnorm",
    )(x, g.reshape(1, d))


def _rope_slab(x, c, s1, s2, shift):
    return x * c + pltpu.roll(x, LANE - shift, 1) * s1 + pltpu.roll(x, shift, 1) * s2


def _proj_kernel(a_ref, b_ref, *rest, rope_shift, head_rows):
    if rope_shift:
        c_ref, s1_ref, s2_ref, o_ref, bq_ref = rest
    else:
        o_ref, bq_ref = rest

    @pl.when(_first_row_tile())
    def _():
        bq_ref[...] = b_ref[...].astype(BF16)

    tm = a_ref.shape[0]
    acc = lax.dot_general(a_ref[...], bq_ref[...], NT_DIMS, preferred_element_type=F32)
    if rope_shift:
        c, s1, s2 = c_ref[...], s1_ref[...], s2_ref[...]
    n_slabs = acc.shape[1] // LANE
    for h in range(n_slabs):
        y = acc[:, h * LANE:(h + 1) * LANE]
        if rope_shift:
            y = _rope_slab(y, c, s1, s2, rope_shift)
        if head_rows:
            o_ref[pl.ds(h, tm, stride=n_slabs), :] = y.astype(o_ref.dtype)
        else:
            o_ref[:, h * LANE:(h + 1) * LANE] = y.astype(o_ref.dtype)


def project(a, wt, col0, ncols, out_dtype, rope=None, head_rows=False):
    m, k = a.shape
    tm = _tile(m, 1024, SUBLANE)
    tn = ncols if head_rows else _tile(ncols, 512, LANE)
    assert col0 % tn == 0
    j0 = col0 // tn
    in_specs = [pl.BlockSpec((tm, k), lambda j, i: (i, 0)), pl.BlockSpec((tn, k), lambda j, i: (j + j0, 0))]
    args = [a, wt]
    shift = 0
    if rope is not None:
        shift, tabs = rope[0], rope[1:]
        reps = tabs[0].shape[0] // tm
        assert reps * tm == tabs[0].shape[0]
        for t in tabs:
            in_specs.append(pl.BlockSpec((tm, LANE), lambda j, i: (i % reps, 0)))
            args.append(t)
    if head_rows:
        heads = ncols // LANE
        out_spec = pl.BlockSpec((tm * heads, LANE), lambda j, i: (i, 0))
        out_shape = jax.ShapeDtypeStruct((m * heads, LANE), out_dtype)
    else:
        out_spec = pl.BlockSpec((tm, tn), lambda j, i: (i, j))
        out_shape = jax.ShapeDtypeStruct((m, ncols), out_dtype)
    return pl.pallas_call(
        functools.partial(_proj_kernel, rope_shift=shift, head_rows=head_rows),
        grid=(ncols // tn, m // tm),
        in_specs=in_specs,
        out_specs=out_spec,
        out_shape=out_shape,
        scratch_shapes=[pltpu.VMEM((tn, k), BF16)],
        compiler_params=_params("arbitrary", "arbitrary"),
        name="project",
    )(*args)


def _rope_tables(pos, head_dim, rot_dim, rows):
    half = rot_dim // 2
    inv_freq = ROPE_THETA ** (-jnp.arange(half, dtype=F32) / half)
    ang = pos.astype(F32)[:, None] * inv_freq[None, :]
    cos, sin = jnp.cos(ang), jnp.sin(ang)
    t = pos.shape[0]
    zeros_h = jnp.zeros((t, half), F32)
    rest0 = jnp.zeros((t, head_dim - rot_dim), F32)
    c = jnp.concatenate([cos, cos, jnp.ones((t, head_dim - rot_dim), F32)], axis=1)
    s1 = jnp.concatenate([-sin, zeros_h, rest0], axis=1)
    s2 = jnp.concatenate([zeros_h, sin, rest0], axis=1)
    reps_l = LANE // head_dim
    reps_r = rows // t
    return tuple(jnp.tile(x, (reps_r, reps_l)) for x in (c, s1, s2))


def _gmlp_kernel(u_ref, v_ref, g_ref, wm_ref, b_ref, a_ref, *vn_out, groups):
    v = v_ref[...]
    vn = v * lax.rsqrt(jnp.mean(v * v, axis=-1, keepdims=True) + EPS) * g_ref[...]
    if vn_out:
        vn_out[0][...] = vn
    vb = vn.astype(BF16)
    gw = v.shape[1] // groups
    for g in range(groups):
        sl = slice(g * gw, (g + 1) * gw)
        s = jnp.dot(wm_ref[g], vb[:, sl], preferred_element_type=F32) + b_ref[:, sl]
        a_ref[:, sl] = (u_ref[:, sl] * s).astype(a_ref.dtype)


def gmlp(uv, g_norm, wm, bias, want_vn):
    m, d2 = uv.shape
    da = d2 // 2
    tc = GMLP_CHUNK
    out_shape = [jax.ShapeDtypeStruct((m, da), BF16)]
    out_specs = [pl.BlockSpec((tc, da), lambda i: (i, 0))]
    if want_vn:
        out_shape.append(jax.ShapeDtypeStruct((m, da), F32))
        out_specs.append(pl.BlockSpec((tc, da), lambda i: (i, 0)))
    res = pl.pallas_call(
        functools.partial(_gmlp_kernel, groups=wm.shape[0]),
        grid=(m // tc,),
        in_specs=[
            pl.BlockSpec((tc, da), lambda i: (i, 0)),
            pl.BlockSpec((tc, da), lambda i: (i, 1)),
            pl.BlockSpec((1, da), lambda i: (0, 0)),
            pl.BlockSpec(wm.shape, lambda i: (0, 0, 0)),
            pl.BlockSpec((tc, da), lambda i: (0, 0)),
        ],
        out_specs=out_specs,
        out_shape=out_shape,
        compiler_params=_params("parallel"),
        name="gmlp",
    )(uv, uv, g_norm.reshape(1, da), wm, bias)
    return res if want_vn else (res[0], None)


def _dsa_select_attend(qi_ref, kwq_ref, q_ref, klo_ref, khi_ref, ks_ref, vs_ref, o_ref, bias_ref, *,
                       s_c, tq, n_kv, topk, row0, n_valid, scale):
    n_tiles = s_c // LANE
    w = kwq_ref[:, IDX_DIM:IDX_DIM + N_IDX_HEADS]
    klo = klo_ref[0:s_c, :]
    khi = khi_ref[0:s_c, :]

    score = jnp.zeros((tq, s_c), F32)
    for p in range(N_IDX_HEADS // 2):
        qp = qi_ref[:, p * LANE:(p + 1) * LANE]
        rel0 = lax.dot_general(qp, klo, NT_DIMS, preferred_element_type=F32)
        rel1 = lax.dot_general(qp, khi, NT_DIMS, preferred_element_type=F32)
        score = (score + jnp.maximum(rel0, 0.0) * w[:, 2 * p:2 * p + 1]
                 + jnp.maximum(rel1, 0.0) * w[:, 2 * p + 1:2 * p + 2])

    kpos = lax.broadcasted_iota(jnp.int32, (tq, s_c), 1)
    if row0 is not None:
        row = lax.broadcasted_iota(jnp.int32, (tq, s_c), 0) + row0
        adm = kpos < (jnp.right_shift(row, CHUNK.bit_length() - 1) + 1) * CHUNK
    else:
        adm = kpos < n_valid
    score = jnp.where(adm, score, -jnp.inf)

    kf = float(topk)

    def count_ge(t):
        return jnp.sum(jnp.where(score >= t, 1.0, 0.0), axis=1, keepdims=True)

    smax = jnp.max(score, axis=1, keepdims=True)
    smin = jnp.min(jnp.where(adm, score, jnp.inf), axis=1, keepdims=True)
    c_max = count_ge(smax)
    c_min = count_ge(smin)
    big = c_max > kf
    lo0 = jnp.where(big, smax, smin)
    hi0 = jnp.where(big, jnp.inf, smax)
    c0 = jnp.where(big, c_max, c_min)

    def active_of(lo, hi, c_lo):
        mid = 0.5 * lo + 0.5 * hi
        return mid, (c_lo > kf) & (mid > lo) & (mid < hi)

    def cond(carry):
        _, act = active_of(*carry)
        return jnp.max(jnp.where(act, 1.0, 0.0)) > 0.0

    def body(carry):
        lo, hi, c_lo = carry
        mid, act = active_of(lo, hi, c_lo)
        cnt = count_ge(mid)
        take = act & (cnt >= kf)
        drop = act & (cnt < kf)
        return jnp.where(take, mid, lo), jnp.where(drop, mid, hi), jnp.where(take, cnt, c_lo)

    lo, hi, c_lo = lax.while_loop(cond, body, (lo0, hi0, c0))

    bias_ref[:, 0:s_c] = jnp.where(score >= lo, 0.0, -jnp.inf)
    tie_row = c_lo > kf

    @pl.when(jnp.max(jnp.where(tie_row, 1.0, 0.0)) > 0.0)
    def _():
        quota = kf - count_ge(hi)
        tri = (lax.broadcasted_iota(jnp.int32, (LANE, LANE), 0)
               <= lax.broadcasted_iota(jnp.int32, (LANE, LANE), 1)).astype(F32).astype(BF16)
        before = jnp.zeros((tq, 1), F32)
        for jt in range(n_tiles):
            sl = slice(jt * LANE, (jt + 1) * LANE)
            sc = score[:, sl]
            cand = jnp.where((sc >= lo) & (sc < hi), 1.0, 0.0)
            rank = jnp.dot(cand.astype(BF16), tri, preferred_element_type=F32) + before
            keep = (sc >= hi) | ((cand > 0.0) & (rank <= quota))
            bias_ref[:, sl] = jnp.where(tie_row, jnp.where(keep, 0.0, -jnp.inf),
                                        jnp.where(sc >= lo, 0.0, -jnp.inf))
            before = before + jnp.sum(cand, axis=1, keepdims=True)

    bias = bias_ref[:, 0:s_c]
    bias_g = jnp.concatenate([bias] * Q_PER_KV, axis=0)
    for n in range(n_kv):
        qn = jnp.concatenate(
            [q_ref[:, (n * Q_PER_KV + g) * HEAD_DIM:(n * Q_PER_KV + g + 1) * HEAD_DIM]
             for g in range(Q_PER_KV)], axis=0)
        ksl = slice(n * HEAD_DIM, (n + 1) * HEAD_DIM)
        logits = lax.dot_general(qn, ks_ref[0:s_c, ksl], NT_DIMS, preferred_element_type=F32) * scale + bias_g
        mx = jnp.max(logits, axis=1, keepdims=True)
        p = jnp.exp(logits - mx)
        den = jnp.sum(p, axis=1, keepdims=True)
        o = jnp.dot(p.astype(BF16), vs_ref[0:s_c, ksl], preferred_element_type=F32) / den
        for g in range(Q_PER_KV):
            hsl = slice((n * Q_PER_KV + g) * HEAD_DIM, (n * Q_PER_KV + g + 1) * HEAD_DIM)
            o_ref[:, hsl] = o[g * tq:(g + 1) * tq].astype(o_ref.dtype)


def _dsa_kernel(qi_ref, kwq_ref, q_ref, kw_ref, k_ref, v_ref, *rest, tq, t_new, p_len, n_classes, topk):
    if p_len:
        pki_ref, pk_ref, pv_ref, o_ref, klo_ref, khi_ref, ks_ref, vs_ref, bias_ref, tmp_ref = rest
    else:
        o_ref, klo_ref, khi_ref, ks_ref, vs_ref, bias_ref = rest
        pk_ref = pv_ref = None
    s_pad = ks_ref.shape[0]
    n_valid = p_len + t_new
    n_kv = ks_ref.shape[1] // HEAD_DIM
    j = pl.program_id(1)

    def head_cols(src_ref, rows, n):
        return src_ref[pl.ds(n, rows, stride=n_kv), :].astype(BF16)

    @pl.when(j == 0)
    def _():
        lane = lax.broadcasted_iota(jnp.int32, (t_new, LANE), 1)
        new_lo = jnp.where(lane < IDX_DIM, kw_ref[...], 0.0)
        if p_len:
            tmp_ref[...] = jnp.zeros(tmp_ref.shape, F32)
            tmp_ref[0:p_len, 0:IDX_DIM] = pki_ref[...]
            tmp_ref[p_len:n_valid, :] = new_lo
            lo = tmp_ref[...]
        else:
            lo = new_lo
        klo_ref[...] = lo.astype(BF16)
        khi_ref[...] = pltpu.roll(lo, IDX_DIM, 1).astype(BF16)
        for dst, past, new in ((ks_ref, pk_ref, k_ref), (vs_ref, pv_ref, v_ref)):
            for n in range(n_kv):
                csl = slice(n * HEAD_DIM, (n + 1) * HEAD_DIM)
                if p_len:
                    dst[0:p_len, csl] = head_cols(past, p_len, n)
                dst[p_len:n_valid, csl] = head_cols(new, t_new, n)
            if s_pad > n_valid:
                dst[n_valid:s_pad, :] = jnp.zeros((s_pad - n_valid, dst.shape[1]), BF16)

    common = dict(tq=tq, n_kv=n_kv, topk=topk, n_valid=n_valid, scale=HEAD_DIM ** -0.5)
    refs = (qi_ref, kwq_ref, q_ref, klo_ref, khi_ref, ks_ref, vs_ref, o_ref, bias_ref)
    if p_len:
        _dsa_select_attend(*refs, s_c=s_pad, row0=None, **common)
    else:
        per_class = pl.num_programs(1) // n_classes
        for c in range(n_classes):
            @pl.when(j // per_class == c)
            def _(c=c):
                _dsa_select_attend(*refs, s_c=(c + 1) * (s_pad // n_classes), row0=j * tq, **common)


def dsa(qi, kw, q, k, v, past, *, tq, topk):
    b, t, dq = q.shape
    n_kv = k.shape[1] // t
    dkv = n_kv * HEAD_DIM
    nq = t // tq
    row = lambda i, j: (i, j, 0)
    full = lambda i, j: (i, 0, 0)
    in_specs = [
        pl.BlockSpec((None, tq, qi.shape[2]), row),
        pl.BlockSpec((None, tq, LANE), row),
        pl.BlockSpec((None, tq, dq), row),
        pl.BlockSpec((None, t, LANE), full),
        pl.BlockSpec((None, t * n_kv, HEAD_DIM), full),
        pl.BlockSpec((None, t * n_kv, HEAD_DIM), full),
    ]
    args = [qi, kw, q, kw, k, v]
    if past is None:
        p_len = 0
        s_pad = t
        n_classes = DSA_PREFIX_CLASSES if (nq % DSA_PREFIX_CLASSES == 0
                                           and t % (DSA_PREFIX_CLASSES * LANE) == 0) else 1
        assert tq == CHUNK
    else:
        layer, pki, pk, pv = past
        p_len = pki.shape[2]
        s_pad = -(-(p_len + t) // LANE) * LANE
        n_classes = 1
        assert nq == 1
        cache = lambda i, j: (layer, i, 0, 0)
        in_specs += [pl.BlockSpec((None, None, p_len, IDX_DIM), cache),
                     pl.BlockSpec((None, None, p_len * n_kv, HEAD_DIM), cache),
                     pl.BlockSpec((None, None, p_len * n_kv, HEAD_DIM), cache)]
        args += [pki, pk, pv]
    scratch = [pltpu.VMEM((s_pad, LANE), BF16), pltpu.VMEM((s_pad, LANE), BF16),
               pltpu.VMEM((s_pad, dkv), BF16), pltpu.VMEM((s_pad, dkv), BF16),
               pltpu.VMEM((tq, s_pad), F32)]
    if past is not None:
        scratch.append(pltpu.VMEM((s_pad, LANE), F32))
    kern = functools.partial(_dsa_kernel, tq=tq, t_new=t, p_len=p_len, n_classes=n_classes, topk=topk)
    return pl.pallas_call(
        kern,
        grid=(b, nq),
        in_specs=in_specs,
        out_specs=pl.BlockSpec((None, tq, dq), row),
        out_shape=jax.ShapeDtypeStruct((b, t, dq), BF16),
        scratch_shapes=scratch,
        compiler_params=_params("arbitrary", "arbitrary"),
        name="dsa",
    )(*args)


def _merge_kernel(a_ref, b_ref, wa_ref, wb_ref, ga_ref, gb_ref, o_ref, waq_ref, wbq_ref):
    @pl.when(_first_row_tile())
    def _():
        waq_ref[...] = wa_ref[...].astype(BF16)
        wbq_ref[...] = wb_ref[...].astype(BF16)

    ya = jnp.dot(a_ref[...], waq_ref[...], preferred_element_type=F32)
    yb = jnp.dot(b_ref[...], wbq_ref[...], preferred_element_type=F32)
    y = jax.nn.sigmoid(ga_ref[...]) * ya + jax.nn.sigmoid(gb_ref[...]) * yb
    o_ref[...] = y.astype(o_ref.dtype)


def merge(a, b, wa, wb, gates):
    m, ka = a.shape
    kb = b.shape[1]
    d = wa.shape[1]
    tm = _tile(m, 1024, SUBLANE)
    tn = _tile(d, 512, LANE)
    nj = d // tn
    return pl.pallas_call(
        _merge_kernel,
        grid=(nj, m // tm),
        in_specs=[
            pl.BlockSpec((tm, ka), lambda j, i: (i, 0)),
            pl.BlockSpec((tm, kb), lambda j, i: (i, 0)),
            pl.BlockSpec((ka, tn), lambda j, i: (0, j)),
            pl.BlockSpec((kb, tn), lambda j, i: (0, j)),
            pl.BlockSpec((tm, tn), lambda j, i: (i, j)),
            pl.BlockSpec((tm, tn), lambda j, i: (i, j + nj)),
        ],
        out_specs=pl.BlockSpec((tm, tn), lambda j, i: (i, j)),
        out_shape=jax.ShapeDtypeStruct((m, d), BF16),
        scratch_shapes=[pltpu.VMEM((ka, tn), BF16), pltpu.VMEM((kb, tn), BF16)],
        compiler_params=_params("arbitrary", "arbitrary"),
        name="merge",
    )(a, b, wa, wb, gates, gates)


def _matmul_res_kernel(a_ref, b_ref, r_ref, o_ref, bq_ref):
    @pl.when(_first_row_tile())
    def _():
        bq_ref[...] = b_ref[...].astype(BF16)

    o_ref[...] = r_ref[...] + jnp.dot(a_ref[...], bq_ref[...], preferred_element_type=F32)


def matmul_residual(a, b, res):
    m, k = a.shape
    n = b.shape[1]
    tm = _tile(m, 1024, SUBLANE)
    tn = _tile(n, 512, LANE)
    return pl.pallas_call(
        _matmul_res_kernel,
        grid=(n // tn, m // tm),
        in_specs=[
            pl.BlockSpec((tm, k), lambda j, i: (i, 0)),
            pl.BlockSpec((k, tn), lambda j, i: (0, j)),
            pl.BlockSpec((tm, tn), lambda j, i: (i, j)),
        ],
        out_specs=pl.BlockSpec((tm, tn), lambda j, i: (i, j)),
        out_shape=jax.ShapeDtypeStruct((m, n), F32),
        scratch_shapes=[pltpu.VMEM((k, tn), BF16)],
        compiler_params=_params("arbitrary", "arbitrary"),
        name="matmul_residual",
    )(a, b, res)


def _ffn_up_kernel(h_ref, wg_ref, wu_ref, cwg_ref, cwu_ref, cbg_ref, cbu_ref, sg_ref, su_ref,
                   act_ref, zg_ref, zu_ref, wq_ref, buf_ref, *, seq_len, tiles_per_seq, n_sub):
    tm = h_ref.shape[0]
    tn = act_ref.shape[1]
    pad = SUBLANE

    @pl.when(_first_row_tile())
    def _():
        wq_ref[:, 0:tn] = wg_ref[...].astype(BF16)
        wq_ref[:, tn:2 * tn] = wu_ref[...].astype(BF16)

    cw = jnp.concatenate([cwg_ref[...], cwu_ref[...]], axis=1)
    cb = jnp.concatenate([cbg_ref[...], cbu_ref[...]], axis=1)

    def gated(c):
        gate, up = c[:, :tn], c[:, tn:]
        return (gate * jax.nn.sigmoid(gate) * up).astype(act_ref.dtype)

    if tiles_per_seq >= 1:
        @pl.when(pl.program_id(1) % tiles_per_seq == 0)
        def _():
            buf_ref[pad - 2:pad, 0:tn] = sg_ref[0]
            buf_ref[pad - 2:pad, tn:2 * tn] = su_ref[0]
        rs = tm // n_sub
        zs = [jnp.dot(h_ref[r * rs:(r + 1) * rs, :], wq_ref[...], preferred_element_type=F32)
              for r in range(n_sub)]
        for r, z in enumerate(zs):
            base = pad + r * rs
            buf_ref[base:base + rs, :] = z
            c = (cb + cw[0:1] * buf_ref[base - 2:base - 2 + rs, :]
                 + cw[1:2] * buf_ref[base - 1:base - 1 + rs, :] + cw[2:3] * z)
            act_ref[r * rs:(r + 1) * rs, :] = gated(c)
        tail = buf_ref[pad + tm - 2:pad + tm, :]
        zg_ref[0] = tail[:, :tn]
        zu_ref[0] = tail[:, tn:]
        buf_ref[pad - 2:pad, :] = tail
    else:
        buf_ref[0:pad, :] = jnp.zeros((pad, 2 * tn), F32)
        buf_ref[pad:pad + tm, :] = jnp.dot(h_ref[...], wq_ref[...], preferred_element_type=F32)
        rowi = lax.broadcasted_iota(jnp.int32, (seq_len, 2 * tn), 0)
        for s in range(tm // seq_len):
            base = pad + s * seq_len
            st0 = jnp.concatenate([sg_ref[s, 0:1, :], su_ref[s, 0:1, :]], axis=1)
            st1 = jnp.concatenate([sg_ref[s, 1:2, :], su_ref[s, 1:2, :]], axis=1)
            z0 = buf_ref[base:base + seq_len, :]
            p1 = jnp.where(rowi == 0, st1, buf_ref[base - 1:base - 1 + seq_len, :])
            p2 = jnp.where(rowi == 0, st0, jnp.where(rowi == 1, st1, buf_ref[base - 2:base - 2 + seq_len, :]))
            act_ref[s * seq_len:(s + 1) * seq_len, :] = gated(cb + cw[0:1] * p2 + cw[1:2] * p1 + cw[2:3] * z0)
            tail = buf_ref[base + seq_len - 2:base + seq_len, :]
            zg_ref[s] = tail[:, :tn]
            zu_ref[s] = tail[:, tn:]


def ffn_up(h, w_up, conv_w, conv_b, state, seq_len):
    m, d = h.shape
    f = w_up.shape[1] // 2
    tm = _tile(m, 1024, SUBLANE)
    tn = _tile(f, 256, LANE)
    nj = f // tn
    if seq_len >= tm:
        assert seq_len % tm == 0
        tiles_per_seq = seq_len // tm
        n_state = 1
        state_map_g = lambda j, i: (i // tiles_per_seq, 0, j)
        state_map_u = lambda j, i: (i // tiles_per_seq, 0, j + nj)
        n_last = m // tm
    else:
        assert tm % seq_len == 0 and seq_len % SUBLANE == 0
        tiles_per_seq = 0
        n_state = tm // seq_len
        state_map_g = lambda j, i: (i, 0, j)
        state_map_u = lambda j, i: (i, 0, j + nj)
        n_last = m // seq_len
    kern = functools.partial(_ffn_up_kernel, seq_len=seq_len, tiles_per_seq=tiles_per_seq,
                             n_sub=FFN_UP_ROW_SPLITS)
    cb = conv_b.reshape(1, 2 * f)
    act, zg, zu = pl.pallas_call(
        kern,
        grid=(nj, m // tm),
        in_specs=[
            pl.BlockSpec((tm, d), lambda j, i: (i, 0)),
            pl.BlockSpec((d, tn), lambda j, i: (0, j)),
            pl.BlockSpec((d, tn), lambda j, i: (0, j + nj)),
            pl.BlockSpec((CONV_WIDTH, tn), lambda j, i: (0, j)),
            pl.BlockSpec((CONV_WIDTH, tn), lambda j, i: (0, j + nj)),
            pl.BlockSpec((1, tn), lambda j, i: (0, j)),
            pl.BlockSpec((1, tn), lambda j, i: (0, j + nj)),
            pl.BlockSpec((n_state, 2, tn), state_map_g),
            pl.BlockSpec((n_state, 2, tn), state_map_u),
        ],
        out_specs=[
            pl.BlockSpec((tm, tn), lambda j, i: (i, j)),
            pl.BlockSpec((n_state, 2, tn), lambda j, i: (i, 0, j)),
            pl.BlockSpec((n_state, 2, tn), lambda j, i: (i, 0, j)),
        ],
        out_shape=[
            jax.ShapeDtypeStruct((m, f), BF16),
            jax.ShapeDtypeStruct((n_last, 2, f), F32),
            jax.ShapeDtypeStruct((n_last, 2, f), F32),
        ],
        scratch_shapes=[pltpu.VMEM((d, 2 * tn), BF16), pltpu.VMEM((tm + SUBLANE, 2 * tn), F32)],
        compiler_params=_params("arbitrary", "arbitrary"),
        name="ffn_up",
    )(h, w_up, w_up, conv_w, conv_w, cb, cb, state, state)
    zlast = jnp.concatenate([zg, zu], axis=-1)
    if tiles_per_seq > 1:
        zlast = zlast[tiles_per_seq - 1::tiles_per_seq]
    return act, zlast


def _ffn_down_kernel(a_ref, b_ref, r_ref, o_ref, acc_ref, *, nk):
    kk = pl.program_id(2)
    part = jnp.dot(a_ref[...], b_ref[...], preferred_element_type=F32)

    @pl.when(kk == 0)
    def _():
        acc_ref[...] = r_ref[...] + part

    @pl.when((kk > 0) & (kk < nk - 1))
    def _():
        acc_ref[...] = acc_ref[...] + part

    @pl.when(kk == nk - 1)
    def _():
        if nk == 1:
            o_ref[...] = r_ref[...] + part
        else:
            o_ref[...] = acc_ref[...] + part


def ffn_down(a, b, res):
    m, k = a.shape
    n = b.shape[1]
    tm = _tile(m, 1024, SUBLANE)
    tn = _tile(n, 512, LANE)
    tk = _tile(k, 5632, LANE)
    nk = k // tk
    return pl.pallas_call(
        functools.partial(_ffn_down_kernel, nk=nk),
        grid=(m // tm, n // tn, nk),
        in_specs=[
            pl.BlockSpec((tm, tk), lambda i, j, kk: (i, kk)),
            pl.BlockSpec((tk, tn), lambda i, j, kk: (kk, j)),
            pl.BlockSpec((tm, tn), lambda i, j, kk: (i, j)),
        ],
        out_specs=pl.BlockSpec((tm, tn), lambda i, j, kk: (i, j)),
        out_shape=jax.ShapeDtypeStruct((m, n), F32),
        scratch_shapes=[pltpu.VMEM((tm, tn), F32)],
        compiler_params=_params("parallel", "arbitrary", "arbitrary"),
        name="ffn_down",
    )(a, b, res)


def _in_offsets(d):
    da = d // 2
    n_heads = d // 256
    n_kv = n_heads // Q_PER_KV
    sizes = (da, da, n_heads * HEAD_DIM, n_kv * HEAD_DIM, n_kv * HEAD_DIM,
             N_IDX_HEADS * IDX_DIM, IDX_DIM, N_IDX_HEADS, d, d)
    offs = [0]
    for s in sizes:
        offs.append(offs[-1] + s)
    return offs


def _layer(x, pos, seq_len, w, past, conv_state, want_vn):
    bsz, t, d = x.shape
    m = bsz * t
    x2 = x.reshape(m, d)
    n_heads = d // 256
    n_kv = n_heads // Q_PER_KV
    offs = _in_offsets(d)
    w_in = w["in_t"]
    assert offs[-1] == w_in.shape[0] and offs[8] - offs[6] <= LANE

    h = rmsnorm(x2, w["norm_attn_g"], BF16)

    rows = max(t, _tile(m, 1024, SUBLANE))
    rope_h = (HEAD_DIM // 8,) + _rope_tables(pos, HEAD_DIM, HEAD_DIM // 4, rows)
    rope_i = (IDX_DIM // 8,) + _rope_tables(pos, IDX_DIM, IDX_DIM // 4, rows)
    ci, s1i, s2i = rope_i[1:]
    lane = jnp.arange(LANE)
    is_w = (lane >= IDX_DIM) & (lane < IDX_DIM + N_IDX_HEADS)
    w_scale = N_IDX_HEADS ** -0.5 * IDX_DIM ** -0.5
    rope_kw = (IDX_DIM // 8,
               jnp.where(is_w, w_scale, jnp.where(lane < IDX_DIM, ci, 1.0)).astype(F32),
               jnp.where(lane < IDX_DIM, s1i, 0.0), jnp.where(lane < IDX_DIM, s2i, 0.0))

    uv = project(h, w_in, offs[0], offs[2] - offs[0], F32)
    q = project(h, w_in, offs[2], offs[3] - offs[2], BF16, rope_h)
    k = project(h, w_in, offs[3], offs[4] - offs[3], F32, rope_h, head_rows=True)
    v = project(h, w_in, offs[4], offs[5] - offs[4], F32, head_rows=True)
    qi = project(h, w_in, offs[5], offs[6] - offs[5], BF16, rope_i)
    kw = project(h, w_in, offs[6], LANE, F32, rope_kw)
    gates = project(h, w["in_gates_t"], 0, 2 * d, F32)

    chunk_rows = min(t, GMLP_CHUNK)
    a_out, vn = gmlp(uv, w["gmlp_norm_g"], w["gmlp_wm"](chunk_rows), w["gmlp_bias"](chunk_rows), want_vn)

    n_valid = t if past is None else past[1].shape[2] + t
    b_out = dsa(qi.reshape(bsz, t, -1), kw.reshape(bsz, t, LANE), q.reshape(bsz, t, -1),
                k.reshape(bsz, t * n_kv, HEAD_DIM), v.reshape(bsz, t * n_kv, HEAD_DIM), past,
                tq=CHUNK if past is None else t, topk=min(TOPK_MAX, n_valid // 4))

    y = merge(a_out, b_out.reshape(m, n_heads * HEAD_DIM), w["a"], w["b"], gates)
    x2 = matmul_residual(y, w["o"], x2)

    hf = rmsnorm(x2, w["norm_ffn_g"], BF16)
    act, zlast = ffn_up(hf, w["up"], w["conv_w"], w["conv_b"], conv_state, seq_len)
    x2 = ffn_down(act, w["down"], x2)
    kidx = kw[:, :IDX_DIM].reshape(bsz, t, IDX_DIM)
    return (x2.reshape(bsz, t, d), k.reshape(bsz, t, n_kv, HEAD_DIM), v.reshape(bsz, t, n_kv, HEAD_DIM),
            kidx, zlast, vn)


def kernel(x_prompt, x_sample, cache_k, cache_v, cache_kidx, state_ffn_conv, norm_attn_g, w_in, gmlp_norm_g, gmlp_ws, gmlp_b, w_branch_a, w_branch_b, w_out, norm_ffn_g, w_up, conv_w, conv_b, w_down, norm_final_g):
    bsz, s, d = x_prompt.shape
    dbsz, t, _ = x_sample.shape
    depth = w_in.shape[0]
    p_len = cache_k.shape[2]
    da = d // 2
    f2 = w_up.shape[2]
    assert s % GMLP_CHUNK == 0 and GMLP_CHUNK % t == 0 and s % CHUNK == 0

    pos_p = jnp.arange(s, dtype=jnp.int32)
    pos_s = p_len + jnp.arange(t, dtype=jnp.int32)
    ci = jnp.arange(GMLP_CHUNK)
    chunk_mask = (ci[None, :] // CHUNK) <= (ci[:, None] // CHUNK)
    cache_k4 = cache_k.reshape(depth, dbsz, -1, HEAD_DIM)
    cache_v4 = cache_v.reshape(depth, dbsz, -1, HEAD_DIM)
    gate_col0 = _in_offsets(d)[8]

    xp, xs = x_prompt, x_sample
    outs = [[] for _ in range(9)]
    for l in range(depth):
        wm_full = jnp.where(chunk_mask[None], gmlp_ws[l], 0.0)
        bias_rows = jnp.repeat(jnp.transpose(gmlp_b[l]), da // G_A, axis=1)

        def gmlp_wm(rows, wm_full=wm_full):
            reps = GMLP_CHUNK // rows
            blk = wm_full[:, :rows, :rows]
            eye = jnp.eye(reps, dtype=F32)
            return jnp.einsum("ab,gij->gaibj", eye, blk).reshape(G_A, GMLP_CHUNK, GMLP_CHUNK).astype(BF16)

        def gmlp_bias(rows, bias_rows=bias_rows):
            return jnp.tile(bias_rows[:rows], (GMLP_CHUNK // rows, 1))

        w = dict(
            norm_attn_g=norm_attn_g[l], gmlp_norm_g=gmlp_norm_g[l], norm_ffn_g=norm_ffn_g[l],
            gmlp_wm=gmlp_wm, gmlp_bias=gmlp_bias,
            a=w_branch_a[l], b=w_branch_b[l], o=w_out[l], up=w_up[l], conv_w=conv_w[l], conv_b=conv_b[l],
            down=w_down[l].astype(BF16),
        )
        w["in_t"] = jnp.swapaxes(w_in[l], 0, 1)
        w["in_gates_t"] = w["in_t"][gate_col0:]

        xp, kp, vp, kip, cp, _ = _layer(xp, pos_p, s, w, None, jnp.zeros((bsz, CONV_WIDTH - 1, f2), F32),
                                        False)
        past = (l, cache_kidx, cache_k4, cache_v4)
        xs, ks, vs, kis, cs, gv = _layer(xs, pos_s, t, w, past, state_ffn_conv[l], True)
        for lst, val in zip(outs, (kp, vp, kip, cp, ks, vs, kis, cs, gv.reshape(dbsz, t, da))):
            lst.append(val)

    y_prompt = rmsnorm(xp.reshape(bsz * s, d), norm_final_g, F32).reshape(bsz, s, d)
    y_sample = rmsnorm(xs.reshape(dbsz * t, d), norm_final_g, F32).reshape(dbsz, t, d)
    stack = lambda o: o[0][None] if depth == 1 else jnp.stack(o)
    return (y_prompt, y_sample) + tuple(stack(o) for o in outs)
```

```python
import functools

import jax
import jax.numpy as jnp
from jax import lax
from jax.experimental import pallas as pl
from jax.experimental.pallas import tpu as pltpu

CHUNK = 64
GMLP_CHUNK = 128
G_A = 8
HEAD_DIM = 128
Q_PER_KV = 4
N_IDX_HEADS = 16
IDX_DIM = 64
TOPK_MAX = 256
ROPE_THETA = 500000.0
CONV_WIDTH = 3
EPS = 1e-6

LANE = 128
SUBLANE = 8
VMEM_LIMIT_BYTES = 56 * 1024 * 1024

FFN_UP_ROW_SPLITS = 4
DSA_PREFIX_CLASSES = 4
DSA_QUERY_BLOCK = 128
TOPK_STEP_SURPLUS = 2.0

BF16 = jnp.bfloat16
F32 = jnp.float32
NT_DIMS = (((1,), (1,)), ((), ()))


def _params(*semantics):
    return pltpu.CompilerParams(dimension_semantics=semantics, vmem_limit_bytes=VMEM_LIMIT_BYTES)


def _tile(n, pref, align):
    if n <= pref:
        return n
    t = (pref // align) * align
    while t >= align:
        if n % t == 0:
            return t
        t -= align
    raise ValueError(f"no {align}-aligned tile of {n} below {pref}")


def _first_row_tile():
    return pl.program_id(1) == 0


def _rmsnorm_kernel(x_ref, g_ref, o_ref):
    x = x_ref[...]
    ms = jnp.mean(x * x, axis=-1, keepdims=True)
    o_ref[...] = (x * lax.rsqrt(ms + EPS) * g_ref[...]).astype(o_ref.dtype)


def rmsnorm(x, g, out_dtype):
    n, d = x.shape
    tr = _tile(n, 256, SUBLANE)
    return pl.pallas_call(
        _rmsnorm_kernel,
        grid=(n // tr,),
        in_specs=[pl.BlockSpec((tr, d), lambda i: (i, 0)), pl.BlockSpec((1, d), lambda i: (0, 0))],
        out_specs=pl.BlockSpec((tr, d), lambda i: (i, 0)),
        out_shape=jax.ShapeDtypeStruct((n, d), out_dtype),
        compiler_params=_params("parallel"),
        name="rmsnorm",
    )(x, g.reshape(1, d))


def _rope_slab(x, c, s1, s2, shift):
    return x * c + pltpu.roll(x, LANE - shift, 1) * s1 + pltpu.roll(x, shift, 1) * s2


def _proj_kernel(a_ref, b_ref, *rest, rope_shift, head_rows):
    if rope_shift:
        c_ref, s1_ref, s2_ref, o_ref, bq_ref = rest
    else:
        o_ref, bq_ref = rest

    @pl.when(_first_row_tile())
    def _():
        bq_ref[...] = b_ref[...].astype(BF16)

    tm = a_ref.shape[0]
    acc = lax.dot_general(a_ref[...], bq_ref[...], NT_DIMS, preferred_element_type=F32)
    if rope_shift:
        c, s1, s2 = c_ref[...], s1_ref[...], s2_ref[...]
    n_slabs = acc.shape[1] // LANE
    for h in range(n_slabs):
        y = acc[:, h * LANE:(h + 1) * LANE]
        if rope_shift:
            y = _rope_slab(y, c, s1, s2, rope_shift)
        if head_rows:
            o_ref[pl.ds(h, tm, stride=n_slabs), :] = y.astype(o_ref.dtype)
        else:
            o_ref[:, h * LANE:(h + 1) * LANE] = y.astype(o_ref.dtype)


def project(a, wt, col0, ncols, out_dtype, rope=None, head_rows=False):
    m, k = a.shape
    tm = _tile(m, 1024, SUBLANE)
    tn = ncols if head_rows else _tile(ncols, 512, LANE)
    assert col0 % tn == 0
    j0 = col0 // tn
    in_specs = [pl.BlockSpec((tm, k), lambda j, i: (i, 0)), pl.BlockSpec((tn, k), lambda j, i: (j + j0, 0))]
    args = [a, wt]
    shift = 0
    if rope is not None:
        shift, tabs = rope[0], rope[1:]
        reps = tabs[0].shape[0] // tm
        assert reps * tm == tabs[0].shape[0]
        for t in tabs:
            in_specs.append(pl.BlockSpec((tm, LANE), lambda j, i: (i % reps, 0)))
            args.append(t)
    if head_rows:
        heads = ncols // LANE
        out_spec = pl.BlockSpec((tm * heads, LANE), lambda j, i: (i, 0))
        out_shape = jax.ShapeDtypeStruct((m * heads, LANE), out_dtype)
    else:
        out_spec = pl.BlockSpec((tm, tn), lambda j, i: (i, j))
        out_shape = jax.ShapeDtypeStruct((m, ncols), out_dtype)
    return pl.pallas_call(
        functools.partial(_proj_kernel, rope_shift=shift, head_rows=head_rows),
        grid=(ncols // tn, m // tm),
        in_specs=in_specs,
        out_specs=out_spec,
        out_shape=out_shape,
        scratch_shapes=[pltpu.VMEM((tn, k), BF16)],
        compiler_params=_params("arbitrary", "arbitrary"),
        name="project",
    )(*args)


def _rope_tables(pos, head_dim, rot_dim, rows):
    half = rot_dim // 2
    inv_freq = ROPE_THETA ** (-jnp.arange(half, dtype=F32) / half)
    ang = pos.astype(F32)[:, None] * inv_freq[None, :]
    cos, sin = jnp.cos(ang), jnp.sin(ang)
    t = pos.shape[0]
    zeros_h = jnp.zeros((t, half), F32)
    rest0 = jnp.zeros((t, head_dim - rot_dim), F32)
    c = jnp.concatenate([cos, cos, jnp.ones((t, head_dim - rot_dim), F32)], axis=1)
    s1 = jnp.concatenate([-sin, zeros_h, rest0], axis=1)
    s2 = jnp.concatenate([zeros_h, sin, rest0], axis=1)
    reps_l = LANE // head_dim
    reps_r = rows // t
    return tuple(jnp.tile(x, (reps_r, reps_l)) for x in (c, s1, s2))


def _gmlp_kernel(u_ref, v_ref, g_ref, wm_ref, b_ref, a_ref, *vn_out, groups):
    v = v_ref[...]
    vn = v * lax.rsqrt(jnp.mean(v * v, axis=-1, keepdims=True) + EPS) * g_ref[...]
    if vn_out:
        vn_out[0][...] = vn
    vb = vn.astype(BF16)
    gw = v.shape[1] // groups
    for g in range(groups):
        sl = slice(g * gw, (g + 1) * gw)
        s = jnp.dot(wm_ref[g], vb[:, sl], preferred_element_type=F32) + b_ref[:, sl]
        a_ref[:, sl] = (u_ref[:, sl] * s).astype(a_ref.dtype)


def gmlp(uv, g_norm, wm, bias, want_vn):
    m, d2 = uv.shape
    da = d2 // 2
    tc = GMLP_CHUNK
    out_shape = [jax.ShapeDtypeStruct((m, da), BF16)]
    out_specs = [pl.BlockSpec((tc, da), lambda i: (i, 0))]
    if want_vn:
        out_shape.append(jax.ShapeDtypeStruct((m, da), F32))
        out_specs.append(pl.BlockSpec((tc, da), lambda i: (i, 0)))
    res = pl.pallas_call(
        functools.partial(_gmlp_kernel, groups=wm.shape[0]),
        grid=(m // tc,),
        in_specs=[
            pl.BlockSpec((tc, da), lambda i: (i, 0)),
            pl.BlockSpec((tc, da), lambda i: (i, 1)),
            pl.BlockSpec((1, da), lambda i: (0, 0)),
            pl.BlockSpec(wm.shape, lambda i: (0, 0, 0)),
            pl.BlockSpec((tc, da), lambda i: (0, 0)),
        ],
        out_specs=out_specs,
        out_shape=out_shape,
        compiler_params=_params("parallel"),
        name="gmlp",
    )(uv, uv, g_norm.reshape(1, da), wm, bias)
    return res if want_vn else (res[0], None)


LOG2_E = 1.4426950408889634
ROW_REDUCE_GROUP = 64
ATTN_ROW_GROUP = 32


def _reduce_rows(x, op):
    pair = {jnp.sum: jnp.add, jnp.min: jnp.minimum, jnp.max: jnp.maximum}[op]
    rows = x.shape[0]
    if rows % ROW_REDUCE_GROUP == 0:
        parts = [x[i:i + ROW_REDUCE_GROUP] for i in range(0, rows, ROW_REDUCE_GROUP)]
        while len(parts) > 1:
            parts = [pair(parts[i], parts[i + 1]) if i + 1 < len(parts) else parts[i]
                     for i in range(0, len(parts), 2)]
        x = parts[0]
    return op(x, axis=0, keepdims=True)

def _dsa_select_attend(qi_ref, kwq_ref, q_ref, klo_ref, khi_ref, ks_ref, vt_ref, o_ref, bias_ref,
                       lg_ref, p_ref, *,
                       s_c, tq, n_kv, topk, row0, n_valid, scale):
    n_tiles = s_c // LANE
    reps = LANE // tq

    def rep_rows(x):
        return x if reps == 1 else jnp.concatenate([x] * reps, axis=0)

    w_t = jnp.transpose(rep_rows(kwq_ref[...]))
    klo = klo_ref[0:s_c, :]
    khi = khi_ref[0:s_c, :]

    score = jnp.zeros((s_c, LANE), F32)
    for pp in range(N_IDX_HEADS // 4):
        qp2 = jnp.concatenate([rep_rows(qi_ref[:, (2 * pp + i) * LANE:(2 * pp + i + 1) * LANE])
                               for i in range(2)], axis=0)
        rel_lo = lax.dot_general(klo, qp2, NT_DIMS, preferred_element_type=F32)
        rel_hi = lax.dot_general(khi, qp2, NT_DIMS, preferred_element_type=F32)
        for i in range(2):
            r0 = IDX_DIM + 2 * (2 * pp + i)
            score = (score + jnp.maximum(rel_lo[:, i * LANE:(i + 1) * LANE], 0.0) * w_t[r0:r0 + 1, :]
                     + jnp.maximum(rel_hi[:, i * LANE:(i + 1) * LANE], 0.0) * w_t[r0 + 1:r0 + 2, :])

    kpos = lax.broadcasted_iota(jnp.int32, (s_c, LANE), 0)
    if row0 is not None:
        qrow = row0 + jnp.bitwise_and(lax.broadcasted_iota(jnp.int32, (s_c, LANE), 1), tq - 1)
        adm = kpos < (jnp.right_shift(qrow, CHUNK.bit_length() - 1) + 1) * CHUNK
    else:
        adm = kpos < n_valid
    score = jnp.where(adm, score, -jnp.inf)
    bias_ref[0:s_c, :] = score

    kf = float(topk)
    lo0 = _reduce_rows(jnp.where(adm, score, jnp.inf), jnp.min)
    hi0 = _reduce_rows(score, jnp.max)
    c_lo0 = _reduce_rows(jnp.where(adm, 1.0, 0.0), jnp.sum)
    c_hi0 = jnp.zeros((1, LANE), F32)

    def probe(t):
        cnt = jnp.zeros((ROW_REDUCE_GROUP, LANE), F32)
        nxt = jnp.full((ROW_REDUCE_GROUP, LANE), jnp.inf, F32)
        for i in range(0, s_c, ROW_REDUCE_GROUP):
            sc = bias_ref[i:i + ROW_REDUCE_GROUP, :]
            above = sc > t
            cnt = cnt + jnp.where(above, 1.0, 0.0)
            nxt = jnp.minimum(nxt, jnp.where(above, sc, jnp.inf))
        return jnp.sum(cnt, axis=0, keepdims=True), jnp.min(nxt, axis=0, keepdims=True)

    def active_of(lo, hi, c_lo):
        return (c_lo > kf) & (hi > lo)

    def cond(carry):
        lo, hi, c_lo, _ = carry
        return jnp.max(jnp.where(active_of(lo, hi, c_lo), 1.0, 0.0)) > 0.0

    def body(carry):
        lo, hi, c_lo, c_hi = carry
        act = active_of(lo, hi, c_lo)
        mid = 0.5 * lo + 0.5 * hi
        step = (c_lo - kf <= TOPK_STEP_SURPLUS) | (mid <= lo) | (mid >= hi)
        t = jnp.where(step, lo, mid)
        cnt, nxt = probe(t)
        take = act & (cnt >= kf)
        drop = act & (cnt < kf)
        return (jnp.where(take, nxt, lo), jnp.where(drop, t, hi),
                jnp.where(take, cnt, c_lo), jnp.where(drop, cnt, c_hi))

    lo, hi, c_lo, c_hi = lax.while_loop(cond, body, (lo0, hi0, c_lo0, c_hi0))
    tie = c_lo > kf
    any_tie = jnp.max(jnp.where(tie, 1.0, 0.0)) > 0.0

    @pl.when(jnp.logical_not(any_tie))
    def _():
        for jt in range(n_tiles):
            sl = slice(jt * LANE, (jt + 1) * LANE)
            bias_ref[sl, :] = jnp.where(bias_ref[sl, :] >= lo, 0.0, -jnp.inf)

    @pl.when(any_tie)
    def _():
        quota = kf - c_hi
        tri = (lax.broadcasted_iota(jnp.int32, (LANE, LANE), 1)
               <= lax.broadcasted_iota(jnp.int32, (LANE, LANE), 0)).astype(F32).astype(BF16)
        before = jnp.zeros((1, LANE), F32)
        for jt in range(n_tiles):
            sl = slice(jt * LANE, (jt + 1) * LANE)
            sc = bias_ref[sl, :]
            cand = jnp.where((sc >= lo) & (sc <= hi), 1.0, 0.0)
            rank = jnp.dot(tri, cand.astype(BF16), preferred_element_type=F32) + before
            keep = (sc > hi) | ((cand > 0.0) & (rank <= quota))
            bias_ref[sl, :] = jnp.where(tie, jnp.where(keep, 0.0, -jnp.inf),
                                        jnp.where(sc >= lo, 0.0, -jnp.inf))
            before = before + jnp.sum(cand, axis=0, keepdims=True)

    width = Q_PER_KV * tq
    n_slabs = width // LANE
    grp = ATTN_ROW_GROUP
    def put_logits(n):
        qn = jnp.concatenate(
            [q_ref[:, (n * Q_PER_KV + g) * HEAD_DIM:(n * Q_PER_KV + g + 1) * HEAD_DIM]
             for g in range(Q_PER_KV)], axis=0)
        lg_ref[n % 2, 0:s_c, :] = lax.dot_general(
            ks_ref[0:s_c, n * HEAD_DIM:(n + 1) * HEAD_DIM], qn, NT_DIMS,
            preferred_element_type=F32) * (scale * LOG2_E)

    put_logits(0)
    for n in range(n_kv):
        if n + 1 < n_kv:
            put_logits(n + 1)
        ksl = slice(n * HEAD_DIM, (n + 1) * HEAD_DIM)
        lgn = lg_ref.at[n % 2]
        pn = p_ref.at[n % 2]
        mpart = jnp.full((grp, width), -jnp.inf, F32)
        for i in range(0, s_c, grp):
            b = bias_ref[i:i + grp, :]
            lg = lgn[i:i + grp, :] + (b if n_slabs == 1 else jnp.concatenate([b] * n_slabs, axis=1))
            lgn[i:i + grp, :] = lg
            mpart = jnp.maximum(mpart, lg)
        mx = jnp.max(mpart, axis=0, keepdims=True)
        dpart = jnp.zeros((grp, width), F32)
        for i in range(0, s_c, grp):
            p = jnp.exp2(lgn[i:i + grp, :] - mx)
            dpart = dpart + p
            pn[i:i + grp, :] = p.astype(BF16)
        den = jnp.sum(dpart, axis=0, keepdims=True)
        o_t = jnp.dot(vt_ref[ksl, 0:s_c], pn[0:s_c, :], preferred_element_type=F32) / den
        for sb in range(n_slabs):
            o = jnp.transpose(o_t[:, sb * LANE:(sb + 1) * LANE])
            for r in range(reps):
                g = sb * reps + r
                hsl = slice((n * Q_PER_KV + g) * HEAD_DIM, (n * Q_PER_KV + g + 1) * HEAD_DIM)
                o_ref[:, hsl] = o[r * tq:(r + 1) * tq].astype(o_ref.dtype)


def _dsa_kernel(qi_ref, kwq_ref, q_ref, kw_ref, k_ref, v_ref, *rest, tq, t_new, p_len, n_classes, topk):
    if p_len:
        (pki_ref, pk_ref, pv_ref, o_ref, klo_ref, khi_ref, ks_ref, vt_ref, bias_ref, lg_ref, p_ref,
         tmp_ref) = rest
    else:
        o_ref, klo_ref, khi_ref, ks_ref, vt_ref, bias_ref, lg_ref, p_ref = rest
        pk_ref = pv_ref = None
    s_pad = ks_ref.shape[0]
    n_valid = p_len + t_new
    n_kv = ks_ref.shape[1] // HEAD_DIM
    j = pl.program_id(1)

    def head_rows(src_ref, rows, n):
        return src_ref[pl.ds(n, rows, stride=n_kv), :]

    @pl.when(j == 0)
    def _():
        lane = lax.broadcasted_iota(jnp.int32, (t_new, LANE), 1)
        new_lo = jnp.where(lane < IDX_DIM, kw_ref[...], 0.0)
        if p_len:
            tmp_ref[...] = jnp.zeros(tmp_ref.shape, F32)
            tmp_ref[0:p_len, 0:IDX_DIM] = pki_ref[...]
            tmp_ref[p_len:n_valid, :] = new_lo
            lo = tmp_ref[...]
        else:
            lo = new_lo
        klo_ref[...] = lo.astype(BF16)
        khi_ref[...] = pltpu.roll(lo, IDX_DIM, 1).astype(BF16)
        tail = jnp.zeros((s_pad - n_valid, HEAD_DIM), F32)
        for n in range(n_kv):
            csl = slice(n * HEAD_DIM, (n + 1) * HEAD_DIM)
            new_k = head_rows(k_ref, t_new, n)
            new_v = head_rows(v_ref, t_new, n)
            if s_pad > n_valid:
                new_k = jnp.concatenate([new_k, tail], axis=0)
                new_v = jnp.concatenate([new_v, tail], axis=0)
            if p_len:
                ks_ref[0:p_len, csl] = head_rows(pk_ref, p_len, n).astype(BF16)
                vt_ref[csl, 0:p_len] = jnp.transpose(head_rows(pv_ref, p_len, n)).astype(BF16)
            ks_ref[p_len:s_pad, csl] = new_k.astype(BF16)
            vt_ref[csl, p_len:s_pad] = jnp.transpose(new_v).astype(BF16)

    common = dict(tq=tq, n_kv=n_kv, topk=topk, n_valid=n_valid, scale=HEAD_DIM ** -0.5)
    refs = (qi_ref, kwq_ref, q_ref, klo_ref, khi_ref, ks_ref, vt_ref, o_ref, bias_ref, lg_ref, p_ref)
    if p_len:
        _dsa_select_attend(*refs, s_c=s_pad, row0=None, **common)
    else:
        per_class = pl.num_programs(1) // n_classes
        for c in range(n_classes):
            @pl.when(j // per_class == c)
            def _(c=c):
                _dsa_select_attend(*refs, s_c=(c + 1) * (s_pad // n_classes), row0=j * tq, **common)


def dsa(qi, kw, q, k, v, past, *, tq, topk):
    b, t, dq = q.shape
    n_kv = k.shape[1] // t
    dkv = n_kv * HEAD_DIM
    nq = t // tq
    row = lambda i, j: (i, j, 0)
    full = lambda i, j: (i, 0, 0)
    in_specs = [
        pl.BlockSpec((None, tq, qi.shape[2]), row),
        pl.BlockSpec((None, tq, LANE), row),
        pl.BlockSpec((None, tq, dq), row),
        pl.BlockSpec((None, t, LANE), full),
        pl.BlockSpec((None, t * n_kv, HEAD_DIM), full),
        pl.BlockSpec((None, t * n_kv, HEAD_DIM), full),
    ]
    args = [qi, kw, q, kw, k, v]
    if past is None:
        p_len = 0
        s_pad = t
        n_classes = DSA_PREFIX_CLASSES if (nq % DSA_PREFIX_CLASSES == 0
                                           and t % (DSA_PREFIX_CLASSES * LANE) == 0) else 1
        assert tq % CHUNK == 0 and LANE % tq == 0
    else:
        layer, pki, pk, pv = past
        p_len = pki.shape[2]
        s_pad = -(-(p_len + t) // LANE) * LANE
        n_classes = 1
        assert nq == 1 and p_len % LANE == 0 and LANE % tq == 0 and Q_PER_KV * tq % LANE == 0
        cache = lambda i, j: (layer, i, 0, 0)
        in_specs += [pl.BlockSpec((None, None, p_len, IDX_DIM), cache),
                     pl.BlockSpec((None, None, p_len * n_kv, HEAD_DIM), cache),
                     pl.BlockSpec((None, None, p_len * n_kv, HEAD_DIM), cache)]
        args += [pki, pk, pv]
    scratch = [pltpu.VMEM((s_pad, LANE), BF16), pltpu.VMEM((s_pad, LANE), BF16),
               pltpu.VMEM((s_pad, dkv), BF16), pltpu.VMEM((dkv, s_pad), BF16),
               pltpu.VMEM((s_pad, LANE), F32),
               pltpu.VMEM((2, s_pad, Q_PER_KV * tq), F32), pltpu.VMEM((2, s_pad, Q_PER_KV * tq), BF16)]
    if past is not None:
        scratch.append(pltpu.VMEM((s_pad, LANE), F32))
    kern = functools.partial(_dsa_kernel, tq=tq, t_new=t, p_len=p_len, n_classes=n_classes, topk=topk)
    return pl.pallas_call(
        kern,
        grid=(b, nq),
        in_specs=in_specs,
        out_specs=pl.BlockSpec((None, tq, dq), row),
        out_shape=jax.ShapeDtypeStruct((b, t, dq), BF16),
        scratch_shapes=scratch,
        compiler_params=_params("arbitrary", "arbitrary"),
        name="dsa",
    )(*args)


def _merge_kernel(a_ref, b_ref, wa_ref, wb_ref, ga_ref, gb_ref, o_ref, waq_ref, wbq_ref):
    @pl.when(_first_row_tile())
    def _():
        waq_ref[...] = wa_ref[...].astype(BF16)
        wbq_ref[...] = wb_ref[...].astype(BF16)

    ya = jnp.dot(a_ref[...], waq_ref[...], preferred_element_type=F32)
    yb = jnp.dot(b_ref[...], wbq_ref[...], preferred_element_type=F32)
    y = jax.nn.sigmoid(ga_ref[...]) * ya + jax.nn.sigmoid(gb_ref[...]) * yb
    o_ref[...] = y.astype(o_ref.dtype)


def merge(a, b, wa, wb, gates):
    m, ka = a.shape
    kb = b.shape[1]
    d = wa.shape[1]
    tm = _tile(m, 1024, SUBLANE)
    tn = _tile(d, 512, LANE)
    nj = d // tn
    return pl.pallas_call(
        _merge_kernel,
        grid=(nj, m // tm),
        in_specs=[
            pl.BlockSpec((tm, ka), lambda j, i: (i, 0)),
            pl.BlockSpec((tm, kb), lambda j, i: (i, 0)),
            pl.BlockSpec((ka, tn), lambda j, i: (0, j)),
            pl.BlockSpec((kb, tn), lambda j, i: (0, j)),
            pl.BlockSpec((tm, tn), lambda j, i: (i, j)),
            pl.BlockSpec((tm, tn), lambda j, i: (i, j + nj)),
        ],
        out_specs=pl.BlockSpec((tm, tn), lambda j, i: (i, j)),
        out_shape=jax.ShapeDtypeStruct((m, d), BF16),
        scratch_shapes=[pltpu.VMEM((ka, tn), BF16), pltpu.VMEM((kb, tn), BF16)],
        compiler_params=_params("arbitrary", "arbitrary"),
        name="merge",
    )(a, b, wa, wb, gates, gates)


def _matmul_res_kernel(a_ref, b_ref, r_ref, o_ref, bq_ref):
    @pl.when(_first_row_tile())
    def _():
        bq_ref[...] = b_ref[...].astype(BF16)

    o_ref[...] = r_ref[...] + jnp.dot(a_ref[...], bq_ref[...], preferred_element_type=F32)


def matmul_residual(a, b, res):
    m, k = a.shape
    n = b.shape[1]
    tm = _tile(m, 1024, SUBLANE)
    tn = _tile(n, 512, LANE)
    return pl.pallas_call(
        _matmul_res_kernel,
        grid=(n // tn, m // tm),
        in_specs=[
            pl.BlockSpec((tm, k), lambda j, i: (i, 0)),
            pl.BlockSpec((k, tn), lambda j, i: (0, j)),
            pl.BlockSpec((tm, tn), lambda j, i: (i, j)),
        ],
        out_specs=pl.BlockSpec((tm, tn), lambda j, i: (i, j)),
        out_shape=jax.ShapeDtypeStruct((m, n), F32),
        scratch_shapes=[pltpu.VMEM((k, tn), BF16)],
        compiler_params=_params("arbitrary", "arbitrary"),
        name="matmul_residual",
    )(a, b, res)


def _ffn_up_kernel(h_ref, wg_ref, wu_ref, cwg_ref, cwu_ref, cbg_ref, cbu_ref, sg_ref, su_ref,
                   act_ref, zg_ref, zu_ref, wq_ref, buf_ref, *, seq_len, tiles_per_seq, n_sub):
    tm = h_ref.shape[0]
    tn = act_ref.shape[1]
    pad = SUBLANE

    @pl.when(_first_row_tile())
    def _():
        wq_ref[:, 0:tn] = wg_ref[...].astype(BF16)
        wq_ref[:, tn:2 * tn] = wu_ref[...].astype(BF16)

    cw = jnp.concatenate([cwg_ref[...], cwu_ref[...]], axis=1)
    cb = jnp.concatenate([cbg_ref[...], cbu_ref[...]], axis=1)

    def gated(c):
        gate, up = c[:, :tn], c[:, tn:]
        return (gate * jax.nn.sigmoid(gate) * up).astype(act_ref.dtype)

    if tiles_per_seq >= 1:
        @pl.when(pl.program_id(1) % tiles_per_seq == 0)
        def _():
            buf_ref[pad - 2:pad, 0:tn] = sg_ref[0]
            buf_ref[pad - 2:pad, tn:2 * tn] = su_ref[0]
        rs = tm // n_sub
        zs = [jnp.dot(h_ref[r * rs:(r + 1) * rs, :], wq_ref[...], preferred_element_type=F32)
              for r in range(n_sub)]
        for r, z in enumerate(zs):
            base = pad + r * rs
            buf_ref[base:base + rs, :] = z
            c = (cb + cw[0:1] * buf_ref[base - 2:base - 2 + rs, :]
                 + cw[1:2] * buf_ref[base - 1:base - 1 + rs, :] + cw[2:3] * z)
            act_ref[r * rs:(r + 1) * rs, :] = gated(c)
        tail = buf_ref[pad + tm - 2:pad + tm, :]
        zg_ref[0] = tail[:, :tn]
        zu_ref[0] = tail[:, tn:]
        buf_ref[pad - 2:pad, :] = tail
    else:
        buf_ref[0:pad, :] = jnp.zeros((pad, 2 * tn), F32)
        buf_ref[pad:pad + tm, :] = jnp.dot(h_ref[...], wq_ref[...], preferred_element_type=F32)
        rowi = lax.broadcasted_iota(jnp.int32, (seq_len, 2 * tn), 0)
        for s in range(tm // seq_len):
            base = pad + s * seq_len
            st0 = jnp.concatenate([sg_ref[s, 0:1, :], su_ref[s, 0:1, :]], axis=1)
            st1 = jnp.concatenate([sg_ref[s, 1:2, :], su_ref[s, 1:2, :]], axis=1)
            z0 = buf_ref[base:base + seq_len, :]
            p1 = jnp.where(rowi == 0, st1, buf_ref[base - 1:base - 1 + seq_len, :])
            p2 = jnp.where(rowi == 0, st0, jnp.where(rowi == 1, st1, buf_ref[base - 2:base - 2 + seq_len, :]))
            act_ref[s * seq_len:(s + 1) * seq_len, :] = gated(cb + cw[0:1] * p2 + cw[1:2] * p1 + cw[2:3] * z0)
            tail = buf_ref[base + seq_len - 2:base + seq_len, :]
            zg_ref[s] = tail[:, :tn]
            zu_ref[s] = tail[:, tn:]


def ffn_up(h, w_up, conv_w, conv_b, state, seq_len):
    m, d = h.shape
    f = w_up.shape[1] // 2
    tm = _tile(m, 1024, SUBLANE)
    tn = _tile(f, 256, LANE)
    nj = f // tn
    if seq_len >= tm:
        assert seq_len % tm == 0
        tiles_per_seq = seq_len // tm
        n_state = 1
        state_map_g = lambda j, i: (i // tiles_per_seq, 0, j)
        state_map_u = lambda j, i: (i // tiles_per_seq, 0, j + nj)
        n_last = m // tm
    else:
        assert tm % seq_len == 0 and seq_len % SUBLANE == 0
        tiles_per_seq = 0
        n_state = tm // seq_len
        state_map_g = lambda j, i: (i, 0, j)
        state_map_u = lambda j, i: (i, 0, j + nj)
        n_last = m // seq_len
    kern = functools.partial(_ffn_up_kernel, seq_len=seq_len, tiles_per_seq=tiles_per_seq,
                             n_sub=FFN_UP_ROW_SPLITS)
    cb = conv_b.reshape(1, 2 * f)
    act, zg, zu = pl.pallas_call(
        kern,
        grid=(nj, m // tm),
        in_specs=[
            pl.BlockSpec((tm, d), lambda j, i: (i, 0)),
            pl.BlockSpec((d, tn), lambda j, i: (0, j)),
            pl.BlockSpec((d, tn), lambda j, i: (0, j + nj)),
            pl.BlockSpec((CONV_WIDTH, tn), lambda j, i: (0, j)),
            pl.BlockSpec((CONV_WIDTH, tn), lambda j, i: (0, j + nj)),
            pl.BlockSpec((1, tn), lambda j, i: (0, j)),
            pl.BlockSpec((1, tn), lambda j, i: (0, j + nj)),
            pl.BlockSpec((n_state, 2, tn), state_map_g),
            pl.BlockSpec((n_state, 2, tn), state_map_u),
        ],
        out_specs=[
            pl.BlockSpec((tm, tn), lambda j, i: (i, j)),
            pl.BlockSpec((n_state, 2, tn), lambda j, i: (i, 0, j)),
            pl.BlockSpec((n_state, 2, tn), lambda j, i: (i, 0, j)),
        ],
        out_shape=[
            jax.ShapeDtypeStruct((m, f), BF16),
            jax.ShapeDtypeStruct((n_last, 2, f), F32),
            jax.ShapeDtypeStruct((n_last, 2, f), F32),
        ],
        scratch_shapes=[pltpu.VMEM((d, 2 * tn), BF16), pltpu.VMEM((tm + SUBLANE, 2 * tn), F32)],
        compiler_params=_params("arbitrary", "arbitrary"),
        name="ffn_up",
    )(h, w_up, w_up, conv_w, conv_w, cb, cb, state, state)
    zlast = jnp.concatenate([zg, zu], axis=-1)
    if tiles_per_seq > 1:
        zlast = zlast[tiles_per_seq - 1::tiles_per_seq]
    return act, zlast


def _ffn_down_kernel(a_ref, b_ref, r_ref, o_ref, acc_ref, *, nk):
    kk = pl.program_id(2)
    part = jnp.dot(a_ref[...], b_ref[...], preferred_element_type=F32)

    @pl.when(kk == 0)
    def _():
        acc_ref[...] = r_ref[...] + part

    @pl.when((kk > 0) & (kk < nk - 1))
    def _():
        acc_ref[...] = acc_ref[...] + part

    @pl.when(kk == nk - 1)
    def _():
        if nk == 1:
            o_ref[...] = r_ref[...] + part
        else:
            o_ref[...] = acc_ref[...] + part


def ffn_down(a, b, res):
    m, k = a.shape
    n = b.shape[1]
    tm = _tile(m, 1024, SUBLANE)
    tn = _tile(n, 512, LANE)
    tk = _tile(k, 5632, LANE)
    nk = k // tk
    return pl.pallas_call(
        functools.partial(_ffn_down_kernel, nk=nk),
        grid=(m // tm, n // tn, nk),
        in_specs=[
            pl.BlockSpec((tm, tk), lambda i, j, kk: (i, kk)),
            pl.BlockSpec((tk, tn), lambda i, j, kk: (kk, j)),
            pl.BlockSpec((tm, tn), lambda i, j, kk: (i, j)),
        ],
        out_specs=pl.BlockSpec((tm, tn), lambda i, j, kk: (i, j)),
        out_shape=jax.ShapeDtypeStruct((m, n), F32),
        scratch_shapes=[pltpu.VMEM((tm, tn), F32)],
        compiler_params=_params("parallel", "arbitrary", "arbitrary"),
        name="ffn_down",
    )(a, b, res)


def _in_offsets(d):
    da = d // 2
    n_heads = d // 256
    n_kv = n_heads // Q_PER_KV
    sizes = (da, da, n_heads * HEAD_DIM, n_kv * HEAD_DIM, n_kv * HEAD_DIM,
             N_IDX_HEADS * IDX_DIM, IDX_DIM, N_IDX_HEADS, d, d)
    offs = [0]
    for s in sizes:
        offs.append(offs[-1] + s)
    return offs


def _layer(x, pos, seq_len, w, past, conv_state, want_vn):
    bsz, t, d = x.shape
    m = bsz * t
    x2 = x.reshape(m, d)
    n_heads = d // 256
    n_kv = n_heads // Q_PER_KV
    offs = _in_offsets(d)
    w_in = w["in_t"]
    assert offs[-1] == w_in.shape[0] and offs[8] - offs[6] <= LANE

    h = rmsnorm(x2, w["norm_attn_g"], BF16)

    rows = max(t, _tile(m, 1024, SUBLANE))
    rope_h = (HEAD_DIM // 8,) + _rope_tables(pos, HEAD_DIM, HEAD_DIM // 4, rows)
    rope_i = (IDX_DIM // 8,) + _rope_tables(pos, IDX_DIM, IDX_DIM // 4, rows)
    ci, s1i, s2i = rope_i[1:]
    lane = jnp.arange(LANE)
    is_w = (lane >= IDX_DIM) & (lane < IDX_DIM + N_IDX_HEADS)
    w_scale = N_IDX_HEADS ** -0.5 * IDX_DIM ** -0.5
    rope_kw = (IDX_DIM // 8,
               jnp.where(is_w, w_scale, jnp.where(lane < IDX_DIM, ci, 1.0)).astype(F32),
               jnp.where(lane < IDX_DIM, s1i, 0.0), jnp.where(lane < IDX_DIM, s2i, 0.0))

    uv = project(h, w_in, offs[0], offs[2] - offs[0], F32)
    q = project(h, w_in, offs[2], offs[3] - offs[2], BF16, rope_h)
    k = project(h, w_in, offs[3], offs[4] - offs[3], F32, rope_h, head_rows=True)
    v = project(h, w_in, offs[4], offs[5] - offs[4], F32, head_rows=True)
    qi = project(h, w_in, offs[5], offs[6] - offs[5], BF16, rope_i)
    kw = project(h, w_in, offs[6], LANE, F32, rope_kw)
    gates = project(h, w["in_gates_t"], 0, 2 * d, F32)

    chunk_rows = min(t, GMLP_CHUNK)
    a_out, vn = gmlp(uv, w["gmlp_norm_g"], w["gmlp_wm"](chunk_rows), w["gmlp_bias"](chunk_rows), want_vn)

    n_valid = t if past is None else past[1].shape[2] + t
    b_out = dsa(qi.reshape(bsz, t, -1), kw.reshape(bsz, t, LANE), q.reshape(bsz, t, -1),
                k.reshape(bsz, t * n_kv, HEAD_DIM), v.reshape(bsz, t * n_kv, HEAD_DIM), past,
                tq=min(DSA_QUERY_BLOCK, t), topk=min(TOPK_MAX, n_valid // 4))

    y = merge(a_out, b_out.reshape(m, n_heads * HEAD_DIM), w["a"], w["b"], gates)
    x2 = matmul_residual(y, w["o"], x2)

    hf = rmsnorm(x2, w["norm_ffn_g"], BF16)
    act, zlast = ffn_up(hf, w["up"], w["conv_w"], w["conv_b"], conv_state, seq_len)
    x2 = ffn_down(act, w["down"], x2)
    kidx = kw[:, :IDX_DIM].reshape(bsz, t, IDX_DIM)
    return (x2.reshape(bsz, t, d), k.reshape(bsz, t, n_kv, HEAD_DIM), v.reshape(bsz, t, n_kv, HEAD_DIM),
            kidx, zlast, vn)


def kernel(x_prompt, x_sample, cache_k, cache_v, cache_kidx, state_ffn_conv, norm_attn_g, w_in, gmlp_norm_g, gmlp_ws, gmlp_b, w_branch_a, w_branch_b, w_out, norm_ffn_g, w_up, conv_w, conv_b, w_down, norm_final_g):
    bsz, s, d = x_prompt.shape
    dbsz, t, _ = x_sample.shape
    depth = w_in.shape[0]
    p_len = cache_k.shape[2]
    da = d // 2
    f2 = w_up.shape[2]
    assert s % GMLP_CHUNK == 0 and GMLP_CHUNK % t == 0 and s % CHUNK == 0

    pos_p = jnp.arange(s, dtype=jnp.int32)
    pos_s = p_len + jnp.arange(t, dtype=jnp.int32)
    ci = jnp.arange(GMLP_CHUNK)
    chunk_mask = (ci[None, :] // CHUNK) <= (ci[:, None] // CHUNK)
    cache_k4 = cache_k.reshape(depth, dbsz, -1, HEAD_DIM)
    cache_v4 = cache_v.reshape(depth, dbsz, -1, HEAD_DIM)
    gate_col0 = _in_offsets(d)[8]

    xp, xs = x_prompt, x_sample
    outs = [[] for _ in range(9)]
    for l in range(depth):
        wm_full = jnp.where(chunk_mask[None], gmlp_ws[l], 0.0)
        bias_rows = jnp.repeat(jnp.transpose(gmlp_b[l]), da // G_A, axis=1)

        def gmlp_wm(rows, wm_full=wm_full):
            reps = GMLP_CHUNK // rows
            blk = wm_full[:, :rows, :rows]
            eye = jnp.eye(reps, dtype=F32)
            return jnp.einsum("ab,gij->gaibj", eye, blk).reshape(G_A, GMLP_CHUNK, GMLP_CHUNK).astype(BF16)

        def gmlp_bias(rows, bias_rows=bias_rows):
            return jnp.tile(bias_rows[:rows], (GMLP_CHUNK // rows, 1))

        w = dict(
            norm_attn_g=norm_attn_g[l], gmlp_norm_g=gmlp_norm_g[l], norm_ffn_g=norm_ffn_g[l],
            gmlp_wm=gmlp_wm, gmlp_bias=gmlp_bias,
            a=w_branch_a[l], b=w_branch_b[l], o=w_out[l], up=w_up[l], conv_w=conv_w[l], conv_b=conv_b[l],
            down=w_down[l].astype(BF16),
        )
        w["in_t"] = jnp.swapaxes(w_in[l], 0, 1)
        w["in_gates_t"] = w["in_t"][gate_col0:]

        xp, kp, vp, kip, cp, _ = _layer(xp, pos_p, s, w, None, jnp.zeros((bsz, CONV_WIDTH - 1, f2), F32),
                                        False)
        past = (l, cache_kidx, cache_k4, cache_v4)
        xs, ks, vs, kis, cs, gv = _layer(xs, pos_s, t, w, past, state_ffn_conv[l], True)
        for lst, val in zip(outs, (kp, vp, kip, cp, ks, vs, kis, cs, gv.reshape(dbsz, t, da))):
            lst.append(val)

    y_prompt = rmsnorm(xp.reshape(bsz * s, d), norm_final_g, F32).reshape(bsz, s, d)
    y_sample = rmsnorm(xs.reshape(dbsz * t, d), norm_final_g, F32).reshape(dbsz, t, d)
    stack = lambda o: o[0][None] if depth == 1 else jnp.stack(o)
    return (y_prompt, y_sample) + tuple(stack(o) for o in outs)
```

```python
import functools
import math

import jax
import jax.numpy as jnp
from jax import lax
from jax.experimental import pallas as pl
from jax.experimental.pallas import tpu as pltpu

CHUNK = 64
GMLP_CHUNK = 128
G_A = 8
HEAD_DIM = 128
Q_PER_KV = 4
N_IDX_HEADS = 16
IDX_DIM = 64
TOPK_MAX = 256
ROPE_THETA = 500000.0
CONV_WIDTH = 3
EPS = 1e-6

LANE = 128
SUBLANE = 8
VMEM_LIMIT_BYTES = 56 * 1024 * 1024

FFN_UP_ROW_SPLITS = 4
DSA_PREFIX_CLASSES = 8
DSA_QUERY_BLOCK = 128
TOPK_STEP_SURPLUS = 2.0

BF16 = jnp.bfloat16
F32 = jnp.float32
NT_DIMS = (((1,), (1,)), ((), ()))


def _params(*semantics):
    return pltpu.CompilerParams(dimension_semantics=semantics, vmem_limit_bytes=VMEM_LIMIT_BYTES)


def _tile(n, pref, align):
    if n <= pref:
        return n
    t = (pref // align) * align
    while t >= align:
        if n % t == 0:
            return t
        t -= align
    raise ValueError(f"no {align}-aligned tile of {n} below {pref}")


def _first_row_tile():
    return pl.program_id(1) == 0


def _sigmoid(x):
    return 0.5 * jnp.tanh(0.5 * x) + 0.5


def _rmsnorm_kernel(x_ref, g_ref, o_ref):
    x = x_ref[...]
    ms = jnp.mean(x * x, axis=-1, keepdims=True)
    o_ref[...] = (x * lax.rsqrt(ms + EPS) * g_ref[...]).astype(o_ref.dtype)


def rmsnorm(x, g, out_dtype):
    n, d = x.shape
    tr = _tile(n, 256, SUBLANE)
    return pl.pallas_call(
        _rmsnorm_kernel,
        grid=(n // tr,),
        in_specs=[pl.BlockSpec((tr, d), lambda i: (i, 0)), pl.BlockSpec((1, d), lambda i: (0, 0))],
        out_specs=pl.BlockSpec((tr, d), lambda i: (i, 0)),
        out_shape=jax.ShapeDtypeStruct((n, d), out_dtype),
        compiler_params=_params("parallel"),
        name="rmsnorm",
    )(x, g.reshape(1, d))


def _rope_slab(x, c, s1, s2, shift):
    return x * c + pltpu.roll(x, LANE - shift, 1) * s1 + pltpu.roll(x, shift, 1) * s2


def _proj_kernel(a_ref, b_ref, *rest, rope_shift, head_rows):
    if rope_shift:
        c_ref, s1_ref, s2_ref, o_ref = rest
        c, s1, s2 = c_ref[...], s1_ref[...], s2_ref[...]
    else:
        (o_ref,) = rest
    tm = a_ref.shape[0]
    acc = lax.dot_general(a_ref[...], b_ref[...], NT_DIMS, preferred_element_type=F32)
    n_slabs = acc.shape[1] // LANE
    for h in range(n_slabs):
        y = acc[:, h * LANE:(h + 1) * LANE]
        if rope_shift:
            y = _rope_slab(y, c, s1, s2, rope_shift)
        if head_rows:
            o_ref[pl.ds(h, tm, stride=n_slabs), :] = y.astype(o_ref.dtype)
        else:
            o_ref[:, h * LANE:(h + 1) * LANE] = y.astype(o_ref.dtype)


def project(a, wt, col0, ncols, out_dtype, rope=None, head_rows=False):
    m, k = a.shape
    tm = _tile(m, 1024, SUBLANE)
    tn = ncols if head_rows else _tile(math.gcd(ncols, col0), 1024, LANE)
    assert col0 % tn == 0 and ncols % tn == 0
    j0 = col0 // tn
    in_specs = [pl.BlockSpec((tm, k), lambda j, i: (i, 0)), pl.BlockSpec((tn, k), lambda j, i: (j + j0, 0))]
    args = [a, wt]
    shift = 0
    if rope is not None:
        shift, tabs = rope[0], rope[1:]
        reps = tabs[0].shape[0] // tm
        assert reps * tm == tabs[0].shape[0]
        for t in tabs:
            in_specs.append(pl.BlockSpec((tm, LANE), lambda j, i: (i % reps, 0)))
            args.append(t)
    if head_rows:
        heads = ncols // LANE
        out_spec = pl.BlockSpec((tm * heads, LANE), lambda j, i: (i, 0))
        out_shape = jax.ShapeDtypeStruct((m * heads, LANE), out_dtype)
    else:
        out_spec = pl.BlockSpec((tm, tn), lambda j, i: (i, j))
        out_shape = jax.ShapeDtypeStruct((m, ncols), out_dtype)
    return pl.pallas_call(
        functools.partial(_proj_kernel, rope_shift=shift, head_rows=head_rows),
        grid=(ncols // tn, m // tm),
        in_specs=in_specs,
        out_specs=out_spec,
        out_shape=out_shape,
        compiler_params=_params("parallel", "arbitrary"),
        name="project",
    )(*args)


def _rope_tables(pos, head_dim, rot_dim, rows):
    half = rot_dim // 2
    inv_freq = ROPE_THETA ** (-jnp.arange(half, dtype=F32) / half)
    ang = pos.astype(F32)[:, None] * inv_freq[None, :]
    cos, sin = jnp.cos(ang), jnp.sin(ang)
    t = pos.shape[0]
    zeros_h = jnp.zeros((t, half), F32)
    rest0 = jnp.zeros((t, head_dim - rot_dim), F32)
    c = jnp.concatenate([cos, cos, jnp.ones((t, head_dim - rot_dim), F32)], axis=1)
    s1 = jnp.concatenate([-sin, zeros_h, rest0], axis=1)
    s2 = jnp.concatenate([zeros_h, sin, rest0], axis=1)
    reps_l = LANE // head_dim
    reps_r = rows // t
    return tuple(jnp.tile(x, (reps_r, reps_l)) for x in (c, s1, s2))


def _gmlp_kernel(u_ref, v_ref, g_ref, wm_ref, b_ref, a_ref, *vn_out, groups):
    v = v_ref[...]
    vn = v * lax.rsqrt(jnp.mean(v * v, axis=-1, keepdims=True) + EPS) * g_ref[...]
    if vn_out:
        vn_out[0][...] = vn
    vb = vn.astype(BF16)
    gw = v.shape[1] // groups
    for g in range(groups):
        sl = slice(g * gw, (g + 1) * gw)
        s = jnp.dot(wm_ref[g], vb[:, sl], preferred_element_type=F32) + b_ref[:, sl]
        a_ref[:, sl] = (u_ref[:, sl] * s).astype(a_ref.dtype)


def gmlp(uv, g_norm, wm, bias, want_vn):
    m, d2 = uv.shape
    da = d2 // 2
    tc = GMLP_CHUNK
    out_shape = [jax.ShapeDtypeStruct((m, da), BF16)]
    out_specs = [pl.BlockSpec((tc, da), lambda i: (i, 0))]
    if want_vn:
        out_shape.append(jax.ShapeDtypeStruct((m, da), F32))
        out_specs.append(pl.BlockSpec((tc, da), lambda i: (i, 0)))
    res = pl.pallas_call(
        functools.partial(_gmlp_kernel, groups=wm.shape[0]),
        grid=(m // tc,),
        in_specs=[
            pl.BlockSpec((tc, da), lambda i: (i, 0)),
            pl.BlockSpec((tc, da), lambda i: (i, 1)),
            pl.BlockSpec((1, da), lambda i: (0, 0)),
            pl.BlockSpec(wm.shape, lambda i: (0, 0, 0)),
            pl.BlockSpec((tc, da), lambda i: (0, 0)),
        ],
        out_specs=out_specs,
        out_shape=out_shape,
        compiler_params=_params("parallel"),
        name="gmlp",
    )(uv, uv, g_norm.reshape(1, da), wm, bias)
    return res if want_vn else (res[0], None)


LOG2_E = 1.4426950408889634
ROW_REDUCE_GROUP = 64
ATTN_ROW_GROUP = 32


def _reduce_rows(x, op):
    pair = {jnp.sum: jnp.add, jnp.min: jnp.minimum, jnp.max: jnp.maximum}[op]
    rows = x.shape[0]
    if rows % ROW_REDUCE_GROUP == 0:
        parts = [x[i:i + ROW_REDUCE_GROUP] for i in range(0, rows, ROW_REDUCE_GROUP)]
        while len(parts) > 1:
            parts = [pair(parts[i], parts[i + 1]) if i + 1 < len(parts) else parts[i]
                     for i in range(0, len(parts), 2)]
        x = parts[0]
    return op(x, axis=0, keepdims=True)

def _dsa_select_attend(qi_ref, kwq_ref, q_ref, klo_ref, khi_ref, ks_ref, vt_ref, o_ref, bias_ref,
                       lg_ref, p_ref, *,
                       s_c, tq, n_kv, topk, row0, n_valid, scale):
    n_tiles = s_c // LANE
    reps = LANE // tq

    def rep_rows(x):
        return x if reps == 1 else jnp.concatenate([x] * reps, axis=0)

    w_t = jnp.transpose(rep_rows(kwq_ref[...]))
    klo = klo_ref[0:s_c, :]
    khi = khi_ref[0:s_c, :]

    score = jnp.zeros((s_c, LANE), F32)
    for pp in range(N_IDX_HEADS // 4):
        qp2 = jnp.concatenate([rep_rows(qi_ref[:, (2 * pp + i) * LANE:(2 * pp + i + 1) * LANE])
                               for i in range(2)], axis=0)
        rel_lo = lax.dot_general(klo, qp2, NT_DIMS, preferred_element_type=F32)
        rel_hi = lax.dot_general(khi, qp2, NT_DIMS, preferred_element_type=F32)
        for i in range(2):
            r0 = IDX_DIM + 2 * (2 * pp + i)
            score = (score + jnp.maximum(rel_lo[:, i * LANE:(i + 1) * LANE], 0.0) * w_t[r0:r0 + 1, :]
                     + jnp.maximum(rel_hi[:, i * LANE:(i + 1) * LANE], 0.0) * w_t[r0 + 1:r0 + 2, :])

    kpos = lax.broadcasted_iota(jnp.int32, (s_c, LANE), 0)
    if row0 is not None:
        qrow = row0 + jnp.bitwise_and(lax.broadcasted_iota(jnp.int32, (s_c, LANE), 1), tq - 1)
        adm = kpos < (jnp.right_shift(qrow, CHUNK.bit_length() - 1) + 1) * CHUNK
    else:
        adm = kpos < n_valid
    score = jnp.where(adm, score, -jnp.inf)
    bias_ref[0:s_c, :] = score

    kf = float(topk)
    lo0 = _reduce_rows(jnp.where(adm, score, jnp.inf), jnp.min)
    hi0 = _reduce_rows(score, jnp.max)
    c_lo0 = _reduce_rows(jnp.where(adm, 1.0, 0.0), jnp.sum)
    c_hi0 = jnp.zeros((1, LANE), F32)

    def probe(t):
        cnt = jnp.zeros((ROW_REDUCE_GROUP, LANE), F32)
        nxt = jnp.full((ROW_REDUCE_GROUP, LANE), jnp.inf, F32)
        for i in range(0, s_c, ROW_REDUCE_GROUP):
            sc = bias_ref[i:i + ROW_REDUCE_GROUP, :]
            above = sc > t
            cnt = cnt + jnp.where(above, 1.0, 0.0)
            nxt = jnp.minimum(nxt, jnp.where(above, sc, jnp.inf))
        return jnp.sum(cnt, axis=0, keepdims=True), jnp.min(nxt, axis=0, keepdims=True)

    def active_of(lo, hi, c_lo):
        return (c_lo > kf) & (hi > lo)

    def cond(carry):
        lo, hi, c_lo, _ = carry
        return jnp.max(jnp.where(active_of(lo, hi, c_lo), 1.0, 0.0)) > 0.0

    def body(carry):
        lo, hi, c_lo, c_hi = carry
        act = active_of(lo, hi, c_lo)
        mid = 0.5 * lo + 0.5 * hi
        step = (c_lo - kf <= TOPK_STEP_SURPLUS) | (mid <= lo) | (mid >= hi)
        t = jnp.where(step, lo, mid)
        cnt, nxt = probe(t)
        take = act & (cnt >= kf)
        drop = act & (cnt < kf)
        return (jnp.where(take, nxt, lo), jnp.where(drop, t, hi),
                jnp.where(take, cnt, c_lo), jnp.where(drop, cnt, c_hi))

    lo, hi, c_lo, c_hi = lax.while_loop(cond, body, (lo0, hi0, c_lo0, c_hi0))
    tie = c_lo > kf
    any_tie = jnp.max(jnp.where(tie, 1.0, 0.0)) > 0.0

    @pl.when(jnp.logical_not(any_tie))
    def _():
        for jt in range(n_tiles):
            sl = slice(jt * LANE, (jt + 1) * LANE)
            bias_ref[sl, :] = jnp.where(bias_ref[sl, :] >= lo, 0.0, -jnp.inf)

    @pl.when(any_tie)
    def _():
        quota = kf - c_hi
        tri = (lax.broadcasted_iota(jnp.int32, (LANE, LANE), 1)
               <= lax.broadcasted_iota(jnp.int32, (LANE, LANE), 0)).astype(F32).astype(BF16)
        before = jnp.zeros((1, LANE), F32)
        for jt in range(n_tiles):
            sl = slice(jt * LANE, (jt + 1) * LANE)
            sc = bias_ref[sl, :]
            cand = jnp.where((sc >= lo) & (sc <= hi), 1.0, 0.0)
            rank = jnp.dot(tri, cand.astype(BF16), preferred_element_type=F32) + before
            keep = (sc > hi) | ((cand > 0.0) & (rank <= quota))
            bias_ref[sl, :] = jnp.where(tie, jnp.where(keep, 0.0, -jnp.inf),
                                        jnp.where(sc >= lo, 0.0, -jnp.inf))
            before = before + jnp.sum(cand, axis=0, keepdims=True)

    width = Q_PER_KV * tq
    n_slabs = width // LANE
    grp = ATTN_ROW_GROUP
    def put_logits(n):
        qn = jnp.concatenate(
            [q_ref[:, (n * Q_PER_KV + g) * HEAD_DIM:(n * Q_PER_KV + g + 1) * HEAD_DIM]
             for g in range(Q_PER_KV)], axis=0)
        lg_ref[n % 2, 0:s_c, :] = lax.dot_general(
            ks_ref[0:s_c, n * HEAD_DIM:(n + 1) * HEAD_DIM], qn, NT_DIMS,
            preferred_element_type=F32) * (scale * LOG2_E)

    put_logits(0)
    for n in range(n_kv):
        if n + 1 < n_kv:
            put_logits(n + 1)
        ksl = slice(n * HEAD_DIM, (n + 1) * HEAD_DIM)
        lgn = lg_ref.at[n % 2]
        pn = p_ref.at[n % 2]
        mpart = jnp.full((grp, width), -jnp.inf, F32)
        for i in range(0, s_c, grp):
            b = bias_ref[i:i + grp, :]
            lg = lgn[i:i + grp, :] + (b if n_slabs == 1 else jnp.concatenate([b] * n_slabs, axis=1))
            lgn[i:i + grp, :] = lg
            mpart = jnp.maximum(mpart, lg)
        mx = jnp.max(mpart, axis=0, keepdims=True)
        dpart = jnp.zeros((grp, width), F32)
        for i in range(0, s_c, grp):
            p = jnp.exp2(lgn[i:i + grp, :] - mx)
            dpart = dpart + p
            pn[i:i + grp, :] = p.astype(BF16)
        den = jnp.sum(dpart, axis=0, keepdims=True)
        o_t = jnp.dot(vt_ref[ksl, 0:s_c], pn[0:s_c, :], preferred_element_type=F32) / den
        for sb in range(n_slabs):
            o = jnp.transpose(o_t[:, sb * LANE:(sb + 1) * LANE])
            for r in range(reps):
                g = sb * reps + r
                hsl = slice((n * Q_PER_KV + g) * HEAD_DIM, (n * Q_PER_KV + g + 1) * HEAD_DIM)
                o_ref[:, hsl] = o[r * tq:(r + 1) * tq].astype(o_ref.dtype)


def _dsa_kernel(qi_ref, kwq_ref, q_ref, kw_ref, k_ref, v_ref, *rest, tq, t_new, p_len, n_classes, topk):
    if p_len:
        (pki_ref, pk_ref, pv_ref, o_ref, klo_ref, khi_ref, ks_ref, vt_ref, bias_ref, lg_ref, p_ref,
         tmp_ref) = rest
    else:
        o_ref, klo_ref, khi_ref, ks_ref, vt_ref, bias_ref, lg_ref, p_ref = rest
        pk_ref = pv_ref = None
    s_pad = ks_ref.shape[0]
    n_valid = p_len + t_new
    n_kv = ks_ref.shape[1] // HEAD_DIM
    j = pl.program_id(1)

    def head_rows(src_ref, rows, n):
        return src_ref[pl.ds(n, rows, stride=n_kv), :]

    @pl.when(j == 0)
    def _():
        lane = lax.broadcasted_iota(jnp.int32, (t_new, LANE), 1)
        new_lo = jnp.where(lane < IDX_DIM, kw_ref[...], 0.0)
        if p_len:
            tmp_ref[...] = jnp.zeros(tmp_ref.shape, F32)
            tmp_ref[0:p_len, 0:IDX_DIM] = pki_ref[...]
            tmp_ref[p_len:n_valid, :] = new_lo
            lo = tmp_ref[...]
        else:
            lo = new_lo
        klo_ref[...] = lo.astype(BF16)
        khi_ref[...] = pltpu.roll(lo, IDX_DIM, 1).astype(BF16)
        tail = jnp.zeros((s_pad - n_valid, HEAD_DIM), F32)
        for n in range(n_kv):
            csl = slice(n * HEAD_DIM, (n + 1) * HEAD_DIM)
            new_k = head_rows(k_ref, t_new, n)
            new_v = head_rows(v_ref, t_new, n)
            if s_pad > n_valid:
                new_k = jnp.concatenate([new_k, tail], axis=0)
                new_v = jnp.concatenate([new_v, tail], axis=0)
            if p_len:
                ks_ref[0:p_len, csl] = head_rows(pk_ref, p_len, n).astype(BF16)
                vt_ref[csl, 0:p_len] = jnp.transpose(head_rows(pv_ref, p_len, n)).astype(BF16)
            ks_ref[p_len:s_pad, csl] = new_k.astype(BF16)
            vt_ref[csl, p_len:s_pad] = jnp.transpose(new_v).astype(BF16)

    common = dict(tq=tq, n_kv=n_kv, topk=topk, n_valid=n_valid, scale=HEAD_DIM ** -0.5)
    refs = (qi_ref, kwq_ref, q_ref, klo_ref, khi_ref, ks_ref, vt_ref, o_ref, bias_ref, lg_ref, p_ref)
    if p_len:
        _dsa_select_attend(*refs, s_c=s_pad, row0=None, **common)
    else:
        per_class = pl.num_programs(1) // n_classes
        for c in range(n_classes):
            @pl.when(j // per_class == c)
            def _(c=c):
                _dsa_select_attend(*refs, s_c=(c + 1) * (s_pad // n_classes), row0=j * tq, **common)


def dsa(qi, kw, q, k, v, past, *, tq, topk):
    b, t, dq = q.shape
    n_kv = k.shape[1] // t
    dkv = n_kv * HEAD_DIM
    nq = t // tq
    row = lambda i, j: (i, j, 0)
    full = lambda i, j: (i, 0, 0)
    in_specs = [
        pl.BlockSpec((None, tq, qi.shape[2]), row),
        pl.BlockSpec((None, tq, LANE), row),
        pl.BlockSpec((None, tq, dq), row),
        pl.BlockSpec((None, t, LANE), full),
        pl.BlockSpec((None, t * n_kv, HEAD_DIM), full),
        pl.BlockSpec((None, t * n_kv, HEAD_DIM), full),
    ]
    args = [qi, kw, q, kw, k, v]
    if past is None:
        p_len = 0
        s_pad = t
        n_classes = DSA_PREFIX_CLASSES if (nq % DSA_PREFIX_CLASSES == 0
                                           and t % (DSA_PREFIX_CLASSES * LANE) == 0) else 1
        assert tq % CHUNK == 0 and LANE % tq == 0
    else:
        layer, pki, pk, pv = past
        p_len = pki.shape[2]
        s_pad = -(-(p_len + t) // LANE) * LANE
        n_classes = 1
        assert nq == 1 and p_len % LANE == 0 and LANE % tq == 0 and Q_PER_KV * tq % LANE == 0
        cache = lambda i, j: (layer, i, 0, 0)
        in_specs += [pl.BlockSpec((None, None, p_len, IDX_DIM), cache),
                     pl.BlockSpec((None, None, p_len * n_kv, HEAD_DIM), cache),
                     pl.BlockSpec((None, None, p_len * n_kv, HEAD_DIM), cache)]
        args += [pki, pk, pv]
    scratch = [pltpu.VMEM((s_pad, LANE), BF16), pltpu.VMEM((s_pad, LANE), BF16),
               pltpu.VMEM((s_pad, dkv), BF16), pltpu.VMEM((dkv, s_pad), BF16),
               pltpu.VMEM((s_pad, LANE), F32),
               pltpu.VMEM((2, s_pad, Q_PER_KV * tq), F32), pltpu.VMEM((2, s_pad, Q_PER_KV * tq), BF16)]
    if past is not None:
        scratch.append(pltpu.VMEM((s_pad, LANE), F32))
    kern = functools.partial(_dsa_kernel, tq=tq, t_new=t, p_len=p_len, n_classes=n_classes, topk=topk)
    return pl.pallas_call(
        kern,
        grid=(b, nq),
        in_specs=in_specs,
        out_specs=pl.BlockSpec((None, tq, dq), row),
        out_shape=jax.ShapeDtypeStruct((b, t, dq), BF16),
        scratch_shapes=scratch,
        compiler_params=_params("arbitrary", "arbitrary"),
        name="dsa",
    )(*args)


def _merge_kernel(a_ref, b_ref, wa_ref, wb_ref, ga_ref, gb_ref, o_ref, waq_ref, wbq_ref):
    @pl.when(_first_row_tile())
    def _():
        waq_ref[...] = wa_ref[...].astype(BF16)
        wbq_ref[...] = wb_ref[...].astype(BF16)

    ya = jnp.dot(a_ref[...], waq_ref[...], preferred_element_type=F32)
    yb = jnp.dot(b_ref[...], wbq_ref[...], preferred_element_type=F32)
    y = _sigmoid(ga_ref[...]) * ya + _sigmoid(gb_ref[...]) * yb
    o_ref[...] = y.astype(o_ref.dtype)


def merge(a, b, wa, wb, gates):
    m, ka = a.shape
    kb = b.shape[1]
    d = wa.shape[1]
    tm = _tile(m, 1024, SUBLANE)
    tn = _tile(d, 512, LANE)
    nj = d // tn
    return pl.pallas_call(
        _merge_kernel,
        grid=(nj, m // tm),
        in_specs=[
            pl.BlockSpec((tm, ka), lambda j, i: (i, 0)),
            pl.BlockSpec((tm, kb), lambda j, i: (i, 0)),
            pl.BlockSpec((ka, tn), lambda j, i: (0, j)),
            pl.BlockSpec((kb, tn), lambda j, i: (0, j)),
            pl.BlockSpec((tm, tn), lambda j, i: (i, j)),
            pl.BlockSpec((tm, tn), lambda j, i: (i, j + nj)),
        ],
        out_specs=pl.BlockSpec((tm, tn), lambda j, i: (i, j)),
        out_shape=jax.ShapeDtypeStruct((m, d), BF16),
        scratch_shapes=[pltpu.VMEM((ka, tn), BF16), pltpu.VMEM((kb, tn), BF16)],
        compiler_params=_params("arbitrary", "arbitrary"),
        name="merge",
    )(a, b, wa, wb, gates, gates)


def _matmul_res_kernel(a_ref, b_ref, r_ref, o_ref, bq_ref):
    @pl.when(_first_row_tile())
    def _():
        bq_ref[...] = b_ref[...].astype(BF16)

    o_ref[...] = r_ref[...] + jnp.dot(a_ref[...], bq_ref[...], preferred_element_type=F32)


def matmul_residual(a, b, res):
    m, k = a.shape
    n = b.shape[1]
    tm = _tile(m, 1024, SUBLANE)
    tn = _tile(n, 512, LANE)
    return pl.pallas_call(
        _matmul_res_kernel,
        grid=(n // tn, m // tm),
        in_specs=[
            pl.BlockSpec((tm, k), lambda j, i: (i, 0)),
            pl.BlockSpec((k, tn), lambda j, i: (0, j)),
            pl.BlockSpec((tm, tn), lambda j, i: (i, j)),
        ],
        out_specs=pl.BlockSpec((tm, tn), lambda j, i: (i, j)),
        out_shape=jax.ShapeDtypeStruct((m, n), F32),
        scratch_shapes=[pltpu.VMEM((k, tn), BF16)],
        compiler_params=_params("arbitrary", "arbitrary"),
        name="matmul_residual",
    )(a, b, res)


def _ffn_up_kernel(h_ref, wg_ref, wu_ref, cwg_ref, cwu_ref, cbg_ref, cbu_ref, sg_ref, su_ref,
                   act_ref, zg_ref, zu_ref, wq_ref, buf_ref, *, seq_len, tiles_per_seq, n_sub):
    tm = h_ref.shape[0]
    tn = act_ref.shape[1]
    pad = SUBLANE

    @pl.when(_first_row_tile())
    def _():
        wq_ref[:, 0:tn] = wg_ref[...].astype(BF16)
        wq_ref[:, tn:2 * tn] = wu_ref[...].astype(BF16)

    cw = jnp.concatenate([cwg_ref[...], cwu_ref[...]], axis=1)
    cb = jnp.concatenate([cbg_ref[...], cbu_ref[...]], axis=1)

    def gated(c):
        gate, up = c[:, :tn], c[:, tn:]
        return (gate * _sigmoid(gate) * up).astype(act_ref.dtype)

    if tiles_per_seq >= 1:
        @pl.when(pl.program_id(1) % tiles_per_seq == 0)
        def _():
            buf_ref[pad - 2:pad, 0:tn] = sg_ref[0]
            buf_ref[pad - 2:pad, tn:2 * tn] = su_ref[0]
        rs = tm // n_sub
        zs = [jnp.dot(h_ref[r * rs:(r + 1) * rs, :], wq_ref[...], preferred_element_type=F32)
              for r in range(n_sub)]
        for r, z in enumerate(zs):
            base = pad + r * rs
            buf_ref[base:base + rs, :] = z
            c = (cb + cw[0:1] * buf_ref[base - 2:base - 2 + rs, :]
                 + cw[1:2] * buf_ref[base - 1:base - 1 + rs, :] + cw[2:3] * z)
            act_ref[r * rs:(r + 1) * rs, :] = gated(c)
        tail = buf_ref[pad + tm - 2:pad + tm, :]
        zg_ref[0] = tail[:, :tn]
        zu_ref[0] = tail[:, tn:]
        buf_ref[pad - 2:pad, :] = tail
    else:
        buf_ref[0:pad, :] = jnp.zeros((pad, 2 * tn), F32)
        buf_ref[pad:pad + tm, :] = jnp.dot(h_ref[...], wq_ref[...], preferred_element_type=F32)
        rowi = lax.broadcasted_iota(jnp.int32, (seq_len, 2 * tn), 0)
        for s in range(tm // seq_len):
            base = pad + s * seq_len
            st0 = jnp.concatenate([sg_ref[s, 0:1, :], su_ref[s, 0:1, :]], axis=1)
            st1 = jnp.concatenate([sg_ref[s, 1:2, :], su_ref[s, 1:2, :]], axis=1)
            z0 = buf_ref[base:base + seq_len, :]
            p1 = jnp.where(rowi == 0, st1, buf_ref[base - 1:base - 1 + seq_len, :])
            p2 = jnp.where(rowi == 0, st0, jnp.where(rowi == 1, st1, buf_ref[base - 2:base - 2 + seq_len, :]))
            act_ref[s * seq_len:(s + 1) * seq_len, :] = gated(cb + cw[0:1] * p2 + cw[1:2] * p1 + cw[2:3] * z0)
            tail = buf_ref[base + seq_len - 2:base + seq_len, :]
            zg_ref[s] = tail[:, :tn]
            zu_ref[s] = tail[:, tn:]


def ffn_up(h, w_up, conv_w, conv_b, state, seq_len):
    m, d = h.shape
    f = w_up.shape[1] // 2
    tm = _tile(m, 1024, SUBLANE)
    tn = _tile(f, 256, LANE)
    nj = f // tn
    if seq_len >= tm:
        assert seq_len % tm == 0
        tiles_per_seq = seq_len // tm
        n_state = 1
        state_map_g = lambda j, i: (i // tiles_per_seq, 0, j)
        state_map_u = lambda j, i: (i // tiles_per_seq, 0, j + nj)
        n_last = m // tm
    else:
        assert tm % seq_len == 0 and seq_len % SUBLANE == 0
        tiles_per_seq = 0
        n_state = tm // seq_len
        state_map_g = lambda j, i: (i, 0, j)
        state_map_u = lambda j, i: (i, 0, j + nj)
        n_last = m // seq_len
    kern = functools.partial(_ffn_up_kernel, seq_len=seq_len, tiles_per_seq=tiles_per_seq,
                             n_sub=FFN_UP_ROW_SPLITS)
    cb = conv_b.reshape(1, 2 * f)
    act, zg, zu = pl.pallas_call(
        kern,
        grid=(nj, m // tm),
        in_specs=[
            pl.BlockSpec((tm, d), lambda j, i: (i, 0)),
            pl.BlockSpec((d, tn), lambda j, i: (0, j)),
            pl.BlockSpec((d, tn), lambda j, i: (0, j + nj)),
            pl.BlockSpec((CONV_WIDTH, tn), lambda j, i: (0, j)),
            pl.BlockSpec((CONV_WIDTH, tn), lambda j, i: (0, j + nj)),
            pl.BlockSpec((1, tn), lambda j, i: (0, j)),
            pl.BlockSpec((1, tn), lambda j, i: (0, j + nj)),
            pl.BlockSpec((n_state, 2, tn), state_map_g),
            pl.BlockSpec((n_state, 2, tn), state_map_u),
        ],
        out_specs=[
            pl.BlockSpec((tm, tn), lambda j, i: (i, j)),
            pl.BlockSpec((n_state, 2, tn), lambda j, i: (i, 0, j)),
            pl.BlockSpec((n_state, 2, tn), lambda j, i: (i, 0, j)),
        ],
        out_shape=[
            jax.ShapeDtypeStruct((m, f), BF16),
            jax.ShapeDtypeStruct((n_last, 2, f), F32),
            jax.ShapeDtypeStruct((n_last, 2, f), F32),
        ],
        scratch_shapes=[pltpu.VMEM((d, 2 * tn), BF16), pltpu.VMEM((tm + SUBLANE, 2 * tn), F32)],
        compiler_params=_params("arbitrary", "arbitrary"),
        name="ffn_up",
    )(h, w_up, w_up, conv_w, conv_w, cb, cb, state, state)
    zlast = jnp.concatenate([zg, zu], axis=-1)
    if tiles_per_seq > 1:
        zlast = zlast[tiles_per_seq - 1::tiles_per_seq]
    return act, zlast


def _ffn_down_kernel(a_ref, b_ref, r_ref, o_ref, acc_ref, *, nk):
    kk = pl.program_id(2)
    part = jnp.dot(a_ref[...], b_ref[...], preferred_element_type=F32)

    @pl.when(kk == 0)
    def _():
        acc_ref[...] = r_ref[...] + part

    @pl.when((kk > 0) & (kk < nk - 1))
    def _():
        acc_ref[...] = acc_ref[...] + part

    @pl.when(kk == nk - 1)
    def _():
        if nk == 1:
            o_ref[...] = r_ref[...] + part
        else:
            o_ref[...] = acc_ref[...] + part


def ffn_down(a, b, res):
    m, k = a.shape
    n = b.shape[1]
    tm = _tile(m, 1024, SUBLANE)
    tn = _tile(n, 512, LANE)
    tk = _tile(k, 5632, LANE)
    nk = k // tk
    return pl.pallas_call(
        functools.partial(_ffn_down_kernel, nk=nk),
        grid=(m // tm, n // tn, nk),
        in_specs=[
            pl.BlockSpec((tm, tk), lambda i, j, kk: (i, kk)),
            pl.BlockSpec((tk, tn), lambda i, j, kk: (kk, j)),
            pl.BlockSpec((tm, tn), lambda i, j, kk: (i, j)),
        ],
        out_specs=pl.BlockSpec((tm, tn), lambda i, j, kk: (i, j)),
        out_shape=jax.ShapeDtypeStruct((m, n), F32),
        scratch_shapes=[pltpu.VMEM((tm, tn), F32)],
        compiler_params=_params("parallel", "arbitrary", "arbitrary"),
        name="ffn_down",
    )(a, b, res)


def _in_offsets(d):
    da = d // 2
    n_heads = d // 256
    n_kv = n_heads // Q_PER_KV
    sizes = (da, da, n_heads * HEAD_DIM, n_kv * HEAD_DIM, n_kv * HEAD_DIM,
             N_IDX_HEADS * IDX_DIM, IDX_DIM, N_IDX_HEADS, d, d)
    offs = [0]
    for s in sizes:
        offs.append(offs[-1] + s)
    return offs


def _layer(x, pos, seq_len, w, past, conv_state, want_vn):
    bsz, t, d = x.shape
    m = bsz * t
    x2 = x.reshape(m, d)
    n_heads = d // 256
    n_kv = n_heads // Q_PER_KV
    offs = _in_offsets(d)
    w_in = w["in_t"]
    assert offs[-1] == w_in.shape[0] and offs[8] - offs[6] <= LANE

    h = rmsnorm(x2, w["norm_attn_g"], BF16)

    rows = max(t, _tile(m, 1024, SUBLANE))
    rope_h = (HEAD_DIM // 8,) + _rope_tables(pos, HEAD_DIM, HEAD_DIM // 4, rows)
    rope_i = (IDX_DIM // 8,) + _rope_tables(pos, IDX_DIM, IDX_DIM // 4, rows)
    ci, s1i, s2i = rope_i[1:]
    lane = jnp.arange(LANE)
    is_w = (lane >= IDX_DIM) & (lane < IDX_DIM + N_IDX_HEADS)
    w_scale = N_IDX_HEADS ** -0.5 * IDX_DIM ** -0.5
    rope_kw = (IDX_DIM // 8,
               jnp.where(is_w, w_scale, jnp.where(lane < IDX_DIM, ci, 1.0)).astype(F32),
               jnp.where(lane < IDX_DIM, s1i, 0.0), jnp.where(lane < IDX_DIM, s2i, 0.0))

    uv = project(h, w_in, offs[0], offs[2] - offs[0], F32)
    q = project(h, w_in, offs[2], offs[3] - offs[2], BF16, rope_h)
    k = project(h, w_in, offs[3], offs[4] - offs[3], F32, rope_h, head_rows=True)
    v = project(h, w_in, offs[4], offs[5] - offs[4], F32, head_rows=True)
    qi = project(h, w_in, offs[5], offs[6] - offs[5], BF16, rope_i)
    kw = project(h, w_in, offs[6], LANE, F32, rope_kw)
    gates = project(h, w["in_gates_t"], 0, 2 * d, F32)

    chunk_rows = min(t, GMLP_CHUNK)
    a_out, vn = gmlp(uv, w["gmlp_norm_g"], w["gmlp_wm"](chunk_rows), w["gmlp_bias"](chunk_rows), want_vn)

    n_valid = t if past is None else past[1].shape[2] + t
    b_out = dsa(qi.reshape(bsz, t, -1), kw.reshape(bsz, t, LANE), q.reshape(bsz, t, -1),
                k.reshape(bsz, t * n_kv, HEAD_DIM), v.reshape(bsz, t * n_kv, HEAD_DIM), past,
                tq=min(DSA_QUERY_BLOCK, t), topk=min(TOPK_MAX, n_valid // 4))

    y = merge(a_out, b_out.reshape(m, n_heads * HEAD_DIM), w["a"], w["b"], gates)
    x2 = matmul_residual(y, w["o"], x2)

    hf = rmsnorm(x2, w["norm_ffn_g"], BF16)
    act, zlast = ffn_up(hf, w["up"], w["conv_w"], w["conv_b"], conv_state, seq_len)
    x2 = ffn_down(act, w["down"], x2)
    kidx = kw[:, :IDX_DIM].reshape(bsz, t, IDX_DIM)
    return (x2.reshape(bsz, t, d), k.reshape(bsz, t, n_kv, HEAD_DIM), v.reshape(bsz, t, n_kv, HEAD_DIM),
            kidx, zlast, vn)


def kernel(x_prompt, x_sample, cache_k, cache_v, cache_kidx, state_ffn_conv, norm_attn_g, w_in, gmlp_norm_g, gmlp_ws, gmlp_b, w_branch_a, w_branch_b, w_out, norm_ffn_g, w_up, conv_w, conv_b, w_down, norm_final_g):
    bsz, s, d = x_prompt.shape
    dbsz, t, _ = x_sample.shape
    depth = w_in.shape[0]
    p_len = cache_k.shape[2]
    da = d // 2
    f2 = w_up.shape[2]
    assert s % GMLP_CHUNK == 0 and GMLP_CHUNK % t == 0 and s % CHUNK == 0

    pos_p = jnp.arange(s, dtype=jnp.int32)
    pos_s = p_len + jnp.arange(t, dtype=jnp.int32)
    ci = jnp.arange(GMLP_CHUNK)
    chunk_mask = (ci[None, :] // CHUNK) <= (ci[:, None] // CHUNK)
    cache_k4 = cache_k.reshape(depth, dbsz, -1, HEAD_DIM)
    cache_v4 = cache_v.reshape(depth, dbsz, -1, HEAD_DIM)
    gate_col0 = _in_offsets(d)[8]

    xp, xs = x_prompt, x_sample
    outs = [[] for _ in range(9)]
    for l in range(depth):
        wm_full = jnp.where(chunk_mask[None], gmlp_ws[l], 0.0)
        bias_rows = jnp.repeat(jnp.transpose(gmlp_b[l]), da // G_A, axis=1)

        def gmlp_wm(rows, wm_full=wm_full):
            reps = GMLP_CHUNK // rows
            blk = wm_full[:, :rows, :rows]
            eye = jnp.eye(reps, dtype=F32)
            return jnp.einsum("ab,gij->gaibj", eye, blk).reshape(G_A, GMLP_CHUNK, GMLP_CHUNK).astype(BF16)

        def gmlp_bias(rows, bias_rows=bias_rows):
            return jnp.tile(bias_rows[:rows], (GMLP_CHUNK // rows, 1))

        w = dict(
            norm_attn_g=norm_attn_g[l], gmlp_norm_g=gmlp_norm_g[l], norm_ffn_g=norm_ffn_g[l],
            gmlp_wm=gmlp_wm, gmlp_bias=gmlp_bias,
            a=w_branch_a[l], b=w_branch_b[l], o=w_out[l], up=w_up[l], conv_w=conv_w[l], conv_b=conv_b[l],
            down=w_down[l].astype(BF16),
        )
        w["in_t"] = jnp.swapaxes(w_in[l], 0, 1).astype(BF16)
        w["in_gates_t"] = jnp.swapaxes(w_in[l], 0, 1)[gate_col0:].astype(BF16)

        xp, kp, vp, kip, cp, _ = _layer(xp, pos_p, s, w, None, jnp.zeros((bsz, CONV_WIDTH - 1, f2), F32),
                                        False)
        past = (l, cache_kidx, cache_k4, cache_v4)
        xs, ks, vs, kis, cs, gv = _layer(xs, pos_s, t, w, past, state_ffn_conv[l], True)
        for lst, val in zip(outs, (kp, vp, kip, cp, ks, vs, kis, cs, gv.reshape(dbsz, t, da))):
            lst.append(val)

    y_prompt = rmsnorm(xp.reshape(bsz * s, d), norm_final_g, F32).reshape(bsz, s, d)
    y_sample = rmsnorm(xs.reshape(dbsz * t, d), norm_final_g, F32).reshape(dbsz, t, d)
    stack = lambda o: o[0][None] if depth == 1 else jnp.stack(o)
    return (y_prompt, y_sample) + tuple(stack(o) for o in outs)
```

```python
import functools
import math

import jax
import jax.numpy as jnp
from jax import lax
from jax.experimental import pallas as pl
from jax.experimental.pallas import tpu as pltpu

CHUNK = 64
GMLP_CHUNK = 128
G_A = 8
HEAD_DIM = 128
Q_PER_KV = 4
N_IDX_HEADS = 16
IDX_DIM = 64
TOPK_MAX = 256
ROPE_THETA = 500000.0
CONV_WIDTH = 3
EPS = 1e-6

LANE = 128
SUBLANE = 8
VMEM_LIMIT_BYTES = 56 * 1024 * 1024

FFN_UP_ROW_SPLITS = 4
DSA_PREFIX_CLASSES = 4
DSA_QUERY_BLOCK = 128
TOPK_STEP_SURPLUS = 2.0

BF16 = jnp.bfloat16
F32 = jnp.float32
NT_DIMS = (((1,), (1,)), ((), ()))


def _params(*semantics):
    return pltpu.CompilerParams(dimension_semantics=semantics, vmem_limit_bytes=VMEM_LIMIT_BYTES)


def _tile(n, pref, align):
    if n <= pref:
        return n
    t = (pref // align) * align
    while t >= align:
        if n % t == 0:
            return t
        t -= align
    raise ValueError(f"no {align}-aligned tile of {n} below {pref}")


def _first_row_tile():
    return pl.program_id(1) == 0


def _sigmoid(x):
    return 0.5 * jnp.tanh(0.5 * x) + 0.5


def _rmsnorm_kernel(x_ref, g_ref, o_ref):
    x = x_ref[...]
    ms = jnp.mean(x * x, axis=-1, keepdims=True)
    o_ref[...] = (x * lax.rsqrt(ms + EPS) * g_ref[...]).astype(o_ref.dtype)


def rmsnorm(x, g, out_dtype):
    n, d = x.shape
    tr = _tile(n, 256, SUBLANE)
    return pl.pallas_call(
        _rmsnorm_kernel,
        grid=(n // tr,),
        in_specs=[pl.BlockSpec((tr, d), lambda i: (i, 0)), pl.BlockSpec((1, d), lambda i: (0, 0))],
        out_specs=pl.BlockSpec((tr, d), lambda i: (i, 0)),
        out_shape=jax.ShapeDtypeStruct((n, d), out_dtype),
        compiler_params=_params("parallel"),
        name="rmsnorm",
    )(x, g.reshape(1, d))


def _rope_slab(x, c, s1, s2, shift):
    return x * c + pltpu.roll(x, LANE - shift, 1) * s1 + pltpu.roll(x, shift, 1) * s2


def _proj_kernel(a_ref, b_ref, *rest, rope_shift, head_rows):
    if rope_shift:
        c_ref, s1_ref, s2_ref, o_ref = rest
        c, s1, s2 = c_ref[...], s1_ref[...], s2_ref[...]
    else:
        (o_ref,) = rest
    tm = a_ref.shape[0]
    acc = lax.dot_general(a_ref[...], b_ref[...], NT_DIMS, preferred_element_type=F32)
    n_slabs = acc.shape[1] // LANE
    for h in range(n_slabs):
        y = acc[:, h * LANE:(h + 1) * LANE]
        if rope_shift:
            y = _rope_slab(y, c, s1, s2, rope_shift)
        if head_rows:
            o_ref[pl.ds(h, tm, stride=n_slabs), :] = y.astype(o_ref.dtype)
        else:
            o_ref[:, h * LANE:(h + 1) * LANE] = y.astype(o_ref.dtype)


def project(a, wt, col0, ncols, out_dtype, rope=None, head_rows=False):
    m, k = a.shape
    tm = _tile(m, 1024, SUBLANE)
    tn = ncols if head_rows else _tile(math.gcd(ncols, col0), 1024, LANE)
    assert col0 % tn == 0 and ncols % tn == 0
    j0 = col0 // tn
    in_specs = [pl.BlockSpec((tm, k), lambda j, i: (i, 0)), pl.BlockSpec((tn, k), lambda j, i: (j + j0, 0))]
    args = [a, wt]
    shift = 0
    if rope is not None:
        shift, tabs = rope[0], rope[1:]
        reps = tabs[0].shape[0] // tm
        assert reps * tm == tabs[0].shape[0]
        for t in tabs:
            in_specs.append(pl.BlockSpec((tm, LANE), lambda j, i: (i % reps, 0)))
            args.append(t)
    if head_rows:
        heads = ncols // LANE
        out_spec = pl.BlockSpec((tm * heads, LANE), lambda j, i: (i, 0))
        out_shape = jax.ShapeDtypeStruct((m * heads, LANE), out_dtype)
    else:
        out_spec = pl.BlockSpec((tm, tn), lambda j, i: (i, j))
        out_shape = jax.ShapeDtypeStruct((m, ncols), out_dtype)
    return pl.pallas_call(
        functools.partial(_proj_kernel, rope_shift=shift, head_rows=head_rows),
        grid=(ncols // tn, m // tm),
        in_specs=in_specs,
        out_specs=out_spec,
        out_shape=out_shape,
        compiler_params=_params("parallel", "arbitrary"),
        name="project",
    )(*args)


def _rope_tables(pos, head_dim, rot_dim, rows):
    half = rot_dim // 2
    inv_freq = ROPE_THETA ** (-jnp.arange(half, dtype=F32) / half)
    ang = pos.astype(F32)[:, None] * inv_freq[None, :]
    cos, sin = jnp.cos(ang), jnp.sin(ang)
    t = pos.shape[0]
    zeros_h = jnp.zeros((t, half), F32)
    rest0 = jnp.zeros((t, head_dim - rot_dim), F32)
    c = jnp.concatenate([cos, cos, jnp.ones((t, head_dim - rot_dim), F32)], axis=1)
    s1 = jnp.concatenate([-sin, zeros_h, rest0], axis=1)
    s2 = jnp.concatenate([zeros_h, sin, rest0], axis=1)
    reps_l = LANE // head_dim
    reps_r = rows // t
    return tuple(jnp.tile(x, (reps_r, reps_l)) for x in (c, s1, s2))


def _gmlp_kernel(u_ref, v_ref, g_ref, wm_ref, b_ref, a_ref, *vn_out, groups):
    v = v_ref[...]
    vn = v * lax.rsqrt(jnp.mean(v * v, axis=-1, keepdims=True) + EPS) * g_ref[...]
    if vn_out:
        vn_out[0][...] = vn
    vb = vn.astype(BF16)
    gw = v.shape[1] // groups
    for g in range(groups):
        sl = slice(g * gw, (g + 1) * gw)
        s = jnp.dot(wm_ref[g], vb[:, sl], preferred_element_type=F32) + b_ref[:, sl]
        a_ref[:, sl] = (u_ref[:, sl] * s).astype(a_ref.dtype)


def gmlp(uv, g_norm, wm, bias, want_vn):
    m, d2 = uv.shape
    da = d2 // 2
    tc = GMLP_CHUNK
    out_shape = [jax.ShapeDtypeStruct((m, da), BF16)]
    out_specs = [pl.BlockSpec((tc, da), lambda i: (i, 0))]
    if want_vn:
        out_shape.append(jax.ShapeDtypeStruct((m, da), F32))
        out_specs.append(pl.BlockSpec((tc, da), lambda i: (i, 0)))
    res = pl.pallas_call(
        functools.partial(_gmlp_kernel, groups=wm.shape[0]),
        grid=(m // tc,),
        in_specs=[
            pl.BlockSpec((tc, da), lambda i: (i, 0)),
            pl.BlockSpec((tc, da), lambda i: (i, 1)),
            pl.BlockSpec((1, da), lambda i: (0, 0)),
            pl.BlockSpec(wm.shape, lambda i: (0, 0, 0)),
            pl.BlockSpec((tc, da), lambda i: (0, 0)),
        ],
        out_specs=out_specs,
        out_shape=out_shape,
        compiler_params=_params("parallel"),
        name="gmlp",
    )(uv, uv, g_norm.reshape(1, da), wm, bias)
    return res if want_vn else (res[0], None)


LOG2_E = 1.4426950408889634
ROW_REDUCE_GROUP = 64
ATTN_ROW_GROUP = 32


def _reduce_rows(x, op):
    pair = {jnp.sum: jnp.add, jnp.min: jnp.minimum, jnp.max: jnp.maximum}[op]
    rows = x.shape[0]
    if rows % ROW_REDUCE_GROUP == 0:
        parts = [x[i:i + ROW_REDUCE_GROUP] for i in range(0, rows, ROW_REDUCE_GROUP)]
        while len(parts) > 1:
            parts = [pair(parts[i], parts[i + 1]) if i + 1 < len(parts) else parts[i]
                     for i in range(0, len(parts), 2)]
        x = parts[0]
    return op(x, axis=0, keepdims=True)

def _dsa_select_attend(qi_ref, kwq_ref, q_ref, klo_ref, khi_ref, ks_ref, vt_ref, o_ref, bias_ref,
                       lg_ref, p_ref, *,
                       s_c, tq, n_kv, topk, row0, n_valid, scale):
    n_tiles = s_c // LANE
    reps = LANE // tq

    def rep_rows(x):
        return x if reps == 1 else jnp.concatenate([x] * reps, axis=0)

    w_t = jnp.transpose(rep_rows(kwq_ref[...]))
    klo = klo_ref[0:s_c, :]
    khi = khi_ref[0:s_c, :]

    score = jnp.zeros((s_c, LANE), F32)
    for pp in range(N_IDX_HEADS // 4):
        qp2 = jnp.concatenate([rep_rows(qi_ref[:, (2 * pp + i) * LANE:(2 * pp + i + 1) * LANE])
                               for i in range(2)], axis=0)
        rel_lo = lax.dot_general(klo, qp2, NT_DIMS, preferred_element_type=F32)
        rel_hi = lax.dot_general(khi, qp2, NT_DIMS, preferred_element_type=F32)
        for i in range(2):
            r0 = IDX_DIM + 2 * (2 * pp + i)
            score = (score + jnp.maximum(rel_lo[:, i * LANE:(i + 1) * LANE], 0.0) * w_t[r0:r0 + 1, :]
                     + jnp.maximum(rel_hi[:, i * LANE:(i + 1) * LANE], 0.0) * w_t[r0 + 1:r0 + 2, :])

    kpos = lax.broadcasted_iota(jnp.int32, (s_c, LANE), 0)
    if row0 is not None:
        qrow = row0 + jnp.bitwise_and(lax.broadcasted_iota(jnp.int32, (s_c, LANE), 1), tq - 1)
        adm = kpos < (jnp.right_shift(qrow, CHUNK.bit_length() - 1) + 1) * CHUNK
    else:
        adm = kpos < n_valid
    score = jnp.where(adm, score, -jnp.inf)
    bias_ref[0:s_c, :] = score

    kf = float(topk)
    lo0 = _reduce_rows(jnp.where(adm, score, jnp.inf), jnp.min)
    hi0 = _reduce_rows(score, jnp.max)
    c_lo0 = _reduce_rows(jnp.where(adm, 1.0, 0.0), jnp.sum)
    c_hi0 = jnp.zeros((1, LANE), F32)

    def probe(t):
        cnt = jnp.zeros((ROW_REDUCE_GROUP, LANE), F32)
        nxt = jnp.full((ROW_REDUCE_GROUP, LANE), jnp.inf, F32)
        for i in range(0, s_c, ROW_REDUCE_GROUP):
            sc = bias_ref[i:i + ROW_REDUCE_GROUP, :]
            above = sc > t
            cnt = cnt + jnp.where(above, 1.0, 0.0)
            nxt = jnp.minimum(nxt, jnp.where(above, sc, jnp.inf))
        return jnp.sum(cnt, axis=0, keepdims=True), jnp.min(nxt, axis=0, keepdims=True)

    def active_of(lo, hi, c_lo):
        return (c_lo > kf) & (hi > lo)

    def cond(carry):
        lo, hi, c_lo, _ = carry
        return jnp.max(jnp.where(active_of(lo, hi, c_lo), 1.0, 0.0)) > 0.0

    def body(carry):
        lo, hi, c_lo, c_hi = carry
        act = active_of(lo, hi, c_lo)
        mid = 0.5 * lo + 0.5 * hi
        step = (c_lo - kf <= TOPK_STEP_SURPLUS) | (mid <= lo) | (mid >= hi)
        t = jnp.where(step, lo, mid)
        cnt, nxt = probe(t)
        take = act & (cnt >= kf)
        drop = act & (cnt < kf)
        return (jnp.where(take, nxt, lo), jnp.where(drop, t, hi),
                jnp.where(take, cnt, c_lo), jnp.where(drop, cnt, c_hi))

    lo, hi, c_lo, c_hi = lax.while_loop(cond, body, (lo0, hi0, c_lo0, c_hi0))
    tie = c_lo > kf
    any_tie = jnp.max(jnp.where(tie, 1.0, 0.0)) > 0.0

    @pl.when(jnp.logical_not(any_tie))
    def _():
        for jt in range(n_tiles):
            sl = slice(jt * LANE, (jt + 1) * LANE)
            bias_ref[sl, :] = jnp.where(bias_ref[sl, :] >= lo, 0.0, -jnp.inf)

    @pl.when(any_tie)
    def _():
        quota = kf - c_hi
        tri = (lax.broadcasted_iota(jnp.int32, (LANE, LANE), 1)
               <= lax.broadcasted_iota(jnp.int32, (LANE, LANE), 0)).astype(F32).astype(BF16)
        before = jnp.zeros((1, LANE), F32)
        for jt in range(n_tiles):
            sl = slice(jt * LANE, (jt + 1) * LANE)
            sc = bias_ref[sl, :]
            cand = jnp.where((sc >= lo) & (sc <= hi), 1.0, 0.0)
            rank = jnp.dot(tri, cand.astype(BF16), preferred_element_type=F32) + before
            keep = (sc > hi) | ((cand > 0.0) & (rank <= quota))
            bias_ref[sl, :] = jnp.where(tie, jnp.where(keep, 0.0, -jnp.inf),
                                        jnp.where(sc >= lo, 0.0, -jnp.inf))
            before = before + jnp.sum(cand, axis=0, keepdims=True)

    width = Q_PER_KV * tq
    n_slabs = width // LANE
    grp = ATTN_ROW_GROUP
    def put_logits(n):
        qn = jnp.concatenate(
            [q_ref[:, (n * Q_PER_KV + g) * HEAD_DIM:(n * Q_PER_KV + g + 1) * HEAD_DIM]
             for g in range(Q_PER_KV)], axis=0)
        lg_ref[n % 2, 0:s_c, :] = lax.dot_general(
            ks_ref[0:s_c, n * HEAD_DIM:(n + 1) * HEAD_DIM], qn, NT_DIMS,
            preferred_element_type=F32) * (scale * LOG2_E)

    put_logits(0)
    for n in range(n_kv):
        if n + 1 < n_kv:
            put_logits(n + 1)
        ksl = slice(n * HEAD_DIM, (n + 1) * HEAD_DIM)
        lgn = lg_ref.at[n % 2]
        pn = p_ref.at[n % 2]
        mpart = jnp.full((grp, width), -jnp.inf, F32)
        for i in range(0, s_c, grp):
            b = bias_ref[i:i + grp, :]
            lg = lgn[i:i + grp, :] + (b if n_slabs == 1 else jnp.concatenate([b] * n_slabs, axis=1))
            lgn[i:i + grp, :] = lg
            mpart = jnp.maximum(mpart, lg)
        mx = jnp.max(mpart, axis=0, keepdims=True)
        dpart = jnp.zeros((grp, width), F32)
        for i in range(0, s_c, grp):
            p = jnp.exp2(lgn[i:i + grp, :] - mx)
            dpart = dpart + p
            pn[i:i + grp, :] = p.astype(BF16)
        den = jnp.sum(dpart, axis=0, keepdims=True)
        o_t = jnp.dot(vt_ref[ksl, 0:s_c], pn[0:s_c, :], preferred_element_type=F32) / den
        for sb in range(n_slabs):
            o = jnp.transpose(o_t[:, sb * LANE:(sb + 1) * LANE])
            for r in range(reps):
                g = sb * reps + r
                hsl = slice((n * Q_PER_KV + g) * HEAD_DIM, (n * Q_PER_KV + g + 1) * HEAD_DIM)
                o_ref[:, hsl] = o[r * tq:(r + 1) * tq].astype(o_ref.dtype)


def _dsa_kernel(qi_ref, kwq_ref, q_ref, kw_ref, k_ref, v_ref, *rest, tq, t_new, p_len, n_classes, topk):
    if p_len:
        (pki_ref, pk_ref, pv_ref, o_ref, klo_ref, khi_ref, ks_ref, vt_ref, bias_ref, lg_ref, p_ref,
         tmp_ref) = rest
    else:
        o_ref, klo_ref, khi_ref, ks_ref, vt_ref, bias_ref, lg_ref, p_ref = rest
        pk_ref = pv_ref = None
    s_pad = ks_ref.shape[0]
    n_valid = p_len + t_new
    n_kv = ks_ref.shape[1] // HEAD_DIM
    j = pl.program_id(1)

    def head_rows(src_ref, rows, n):
        return src_ref[pl.ds(n, rows, stride=n_kv), :]

    @pl.when(j == 0)
    def _():
        lane = lax.broadcasted_iota(jnp.int32, (t_new, LANE), 1)
        new_lo = jnp.where(lane < IDX_DIM, kw_ref[...], 0.0)
        if p_len:
            tmp_ref[...] = jnp.zeros(tmp_ref.shape, F32)
            tmp_ref[0:p_len, 0:IDX_DIM] = pki_ref[...]
            tmp_ref[p_len:n_valid, :] = new_lo
            lo = tmp_ref[...]
        else:
            lo = new_lo
        klo_ref[...] = lo.astype(BF16)
        khi_ref[...] = pltpu.roll(lo, IDX_DIM, 1).astype(BF16)
        tail = jnp.zeros((s_pad - n_valid, HEAD_DIM), F32)
        for n in range(n_kv):
            csl = slice(n * HEAD_DIM, (n + 1) * HEAD_DIM)
            new_k = head_rows(k_ref, t_new, n)
            new_v = head_rows(v_ref, t_new, n)
            if s_pad > n_valid:
                new_k = jnp.concatenate([new_k, tail], axis=0)
                new_v = jnp.concatenate([new_v, tail], axis=0)
            if p_len:
                ks_ref[0:p_len, csl] = head_rows(pk_ref, p_len, n).astype(BF16)
                vt_ref[csl, 0:p_len] = jnp.transpose(head_rows(pv_ref, p_len, n)).astype(BF16)
            ks_ref[p_len:s_pad, csl] = new_k.astype(BF16)
            vt_ref[csl, p_len:s_pad] = jnp.transpose(new_v).astype(BF16)

    common = dict(tq=tq, n_kv=n_kv, topk=topk, n_valid=n_valid, scale=HEAD_DIM ** -0.5)
    refs = (qi_ref, kwq_ref, q_ref, klo_ref, khi_ref, ks_ref, vt_ref, o_ref, bias_ref, lg_ref, p_ref)
    if p_len:
        _dsa_select_attend(*refs, s_c=s_pad, row0=None, **common)
    else:
        per_class = pl.num_programs(1) // n_classes
        for c in range(n_classes):
            @pl.when(j // per_class == c)
            def _(c=c):
                _dsa_select_attend(*refs, s_c=(c + 1) * (s_pad // n_classes), row0=j * tq, **common)


def dsa(qi, kw, q, k, v, past, *, tq, topk):
    b, t, dq = q.shape
    n_kv = k.shape[1] // t
    dkv = n_kv * HEAD_DIM
    nq = t // tq
    row = lambda i, j: (i, j, 0)
    full = lambda i, j: (i, 0, 0)
    in_specs = [
        pl.BlockSpec((None, tq, qi.shape[2]), row),
        pl.BlockSpec((None, tq, LANE), row),
        pl.BlockSpec((None, tq, dq), row),
        pl.BlockSpec((None, t, LANE), full),
        pl.BlockSpec((None, t * n_kv, HEAD_DIM), full),
        pl.BlockSpec((None, t * n_kv, HEAD_DIM), full),
    ]
    args = [qi, kw, q, kw, k, v]
    if past is None:
        p_len = 0
        s_pad = t
        n_classes = DSA_PREFIX_CLASSES if (nq % DSA_PREFIX_CLASSES == 0
                                           and t % (DSA_PREFIX_CLASSES * LANE) == 0) else 1
        assert tq % CHUNK == 0 and LANE % tq == 0
    else:
        layer, pki, pk, pv = past
        p_len = pki.shape[2]
        s_pad = -(-(p_len + t) // LANE) * LANE
        n_classes = 1
        assert nq == 1 and p_len % LANE == 0 and LANE % tq == 0 and Q_PER_KV * tq % LANE == 0
        cache = lambda i, j: (layer, i, 0, 0)
        in_specs += [pl.BlockSpec((None, None, p_len, IDX_DIM), cache),
                     pl.BlockSpec((None, None, p_len * n_kv, HEAD_DIM), cache),
                     pl.BlockSpec((None, None, p_len * n_kv, HEAD_DIM), cache)]
        args += [pki, pk, pv]
    scratch = [pltpu.VMEM((s_pad, LANE), BF16), pltpu.VMEM((s_pad, LANE), BF16),
               pltpu.VMEM((s_pad, dkv), BF16), pltpu.VMEM((dkv, s_pad), BF16),
               pltpu.VMEM((s_pad, LANE), F32),
               pltpu.VMEM((2, s_pad, Q_PER_KV * tq), F32), pltpu.VMEM((2, s_pad, Q_PER_KV * tq), BF16)]
    if past is not None:
        scratch.append(pltpu.VMEM((s_pad, LANE), F32))
    kern = functools.partial(_dsa_kernel, tq=tq, t_new=t, p_len=p_len, n_classes=n_classes, topk=topk)
    return pl.pallas_call(
        kern,
        grid=(b, nq),
        in_specs=in_specs,
        out_specs=pl.BlockSpec((None, tq, dq), row),
        out_shape=jax.ShapeDtypeStruct((b, t, dq), BF16),
        scratch_shapes=scratch,
        compiler_params=_params("arbitrary", "arbitrary"),
        name="dsa",
    )(*args)


def _merge_kernel(a_ref, b_ref, wa_ref, wb_ref, ga_ref, gb_ref, o_ref, waq_ref, wbq_ref):
    @pl.when(_first_row_tile())
    def _():
        waq_ref[...] = wa_ref[...].astype(BF16)
        wbq_ref[...] = wb_ref[...].astype(BF16)

    ya = jnp.dot(a_ref[...], waq_ref[...], preferred_element_type=F32)
    yb = jnp.dot(b_ref[...], wbq_ref[...], preferred_element_type=F32)
    y = _sigmoid(ga_ref[...]) * ya + _sigmoid(gb_ref[...]) * yb
    o_ref[...] = y.astype(o_ref.dtype)


def merge(a, b, wa, wb, gates):
    m, ka = a.shape
    kb = b.shape[1]
    d = wa.shape[1]
    tm = _tile(m, 1024, SUBLANE)
    tn = _tile(d, 512, LANE)
    nj = d // tn
    return pl.pallas_call(
        _merge_kernel,
        grid=(nj, m // tm),
        in_specs=[
            pl.BlockSpec((tm, ka), lambda j, i: (i, 0)),
            pl.BlockSpec((tm, kb), lambda j, i: (i, 0)),
            pl.BlockSpec((ka, tn), lambda j, i: (0, j)),
            pl.BlockSpec((kb, tn), lambda j, i: (0, j)),
            pl.BlockSpec((tm, tn), lambda j, i: (i, j)),
            pl.BlockSpec((tm, tn), lambda j, i: (i, j + nj)),
        ],
        out_specs=pl.BlockSpec((tm, tn), lambda j, i: (i, j)),
        out_shape=jax.ShapeDtypeStruct((m, d), BF16),
        scratch_shapes=[pltpu.VMEM((ka, tn), BF16), pltpu.VMEM((kb, tn), BF16)],
        compiler_params=_params("arbitrary", "arbitrary"),
        name="merge",
    )(a, b, wa, wb, gates, gates)


def _matmul_res_kernel(a_ref, b_ref, r_ref, o_ref, bq_ref):
    @pl.when(_first_row_tile())
    def _():
        bq_ref[...] = b_ref[...].astype(BF16)

    o_ref[...] = r_ref[...] + jnp.dot(a_ref[...], bq_ref[...], preferred_element_type=F32)


def matmul_residual(a, b, res):
    m, k = a.shape
    n = b.shape[1]
    tm = _tile(m, 1024, SUBLANE)
    tn = _tile(n, 512, LANE)
    return pl.pallas_call(
        _matmul_res_kernel,
        grid=(n // tn, m // tm),
        in_specs=[
            pl.BlockSpec((tm, k), lambda j, i: (i, 0)),
            pl.BlockSpec((k, tn), lambda j, i: (0, j)),
            pl.BlockSpec((tm, tn), lambda j, i: (i, j)),
        ],
        out_specs=pl.BlockSpec((tm, tn), lambda j, i: (i, j)),
        out_shape=jax.ShapeDtypeStruct((m, n), F32),
        scratch_shapes=[pltpu.VMEM((k, tn), BF16)],
        compiler_params=_params("arbitrary", "arbitrary"),
        name="matmul_residual",
    )(a, b, res)


def _ffn_up_kernel(h_ref, wg_ref, wu_ref, cwg_ref, cwu_ref, cbg_ref, cbu_ref, sg_ref, su_ref,
                   act_ref, zg_ref, zu_ref, wq_ref, buf_ref, *, seq_len, tiles_per_seq, n_sub):
    tm = h_ref.shape[0]
    tn = act_ref.shape[1]
    pad = SUBLANE

    @pl.when(_first_row_tile())
    def _():
        wq_ref[:, 0:tn] = wg_ref[...].astype(BF16)
        wq_ref[:, tn:2 * tn] = wu_ref[...].astype(BF16)

    cw = jnp.concatenate([cwg_ref[...], cwu_ref[...]], axis=1)
    cb = jnp.concatenate([cbg_ref[...], cbu_ref[...]], axis=1)

    def gated(c):
        gate, up = c[:, :tn], c[:, tn:]
        return (gate * _sigmoid(gate) * up).astype(act_ref.dtype)

    if tiles_per_seq >= 1:
        @pl.when(pl.program_id(1) % tiles_per_seq == 0)
        def _():
            buf_ref[pad - 2:pad, 0:tn] = sg_ref[0]
            buf_ref[pad - 2:pad, tn:2 * tn] = su_ref[0]
        rs = tm // n_sub
        zs = [jnp.dot(h_ref[r * rs:(r + 1) * rs, :], wq_ref[...], preferred_element_type=F32)
              for r in range(n_sub)]
        for r, z in enumerate(zs):
            base = pad + r * rs
            buf_ref[base:base + rs, :] = z
            c = (cb + cw[0:1] * buf_ref[base - 2:base - 2 + rs, :]
                 + cw[1:2] * buf_ref[base - 1:base - 1 + rs, :] + cw[2:3] * z)
            act_ref[r * rs:(r + 1) * rs, :] = gated(c)
        tail = buf_ref[pad + tm - 2:pad + tm, :]
        zg_ref[0] = tail[:, :tn]
        zu_ref[0] = tail[:, tn:]
        buf_ref[pad - 2:pad, :] = tail
    else:
        buf_ref[0:pad, :] = jnp.zeros((pad, 2 * tn), F32)
        buf_ref[pad:pad + tm, :] = jnp.dot(h_ref[...], wq_ref[...], preferred_element_type=F32)
        rowi = lax.broadcasted_iota(jnp.int32, (seq_len, 2 * tn), 0)
        for s in range(tm // seq_len):
            base = pad + s * seq_len
            st0 = jnp.concatenate([sg_ref[s, 0:1, :], su_ref[s, 0:1, :]], axis=1)
            st1 = jnp.concatenate([sg_ref[s, 1:2, :], su_ref[s, 1:2, :]], axis=1)
            z0 = buf_ref[base:base + seq_len, :]
            p1 = jnp.where(rowi == 0, st1, buf_ref[base - 1:base - 1 + seq_len, :])
            p2 = jnp.where(rowi == 0, st0, jnp.where(rowi == 1, st1, buf_ref[base - 2:base - 2 + seq_len, :]))
            act_ref[s * seq_len:(s + 1) * seq_len, :] = gated(cb + cw[0:1] * p2 + cw[1:2] * p1 + cw[2:3] * z0)
            tail = buf_ref[base + seq_len - 2:base + seq_len, :]
            zg_ref[s] = tail[:, :tn]
            zu_ref[s] = tail[:, tn:]


def ffn_up(h, w_up, conv_w, conv_b, state, seq_len):
    m, d = h.shape
    f = w_up.shape[1] // 2
    tm = _tile(m, 1024, SUBLANE)
    tn = _tile(f, 256, LANE)
    nj = f // tn
    if seq_len >= tm:
        assert seq_len % tm == 0
        tiles_per_seq = seq_len // tm
        n_state = 1
        state_map_g = lambda j, i: (i // tiles_per_seq, 0, j)
        state_map_u = lambda j, i: (i // tiles_per_seq, 0, j + nj)
        n_last = m // tm
    else:
        assert tm % seq_len == 0 and seq_len % SUBLANE == 0
        tiles_per_seq = 0
        n_state = tm // seq_len
        state_map_g = lambda j, i: (i, 0, j)
        state_map_u = lambda j, i: (i, 0, j + nj)
        n_last = m // seq_len
    kern = functools.partial(_ffn_up_kernel, seq_len=seq_len, tiles_per_seq=tiles_per_seq,
                             n_sub=FFN_UP_ROW_SPLITS)
    cb = conv_b.reshape(1, 2 * f)
    act, zg, zu = pl.pallas_call(
        kern,
        grid=(nj, m // tm),
        in_specs=[
            pl.BlockSpec((tm, d), lambda j, i: (i, 0)),
            pl.BlockSpec((d, tn), lambda j, i: (0, j)),
            pl.BlockSpec((d, tn), lambda j, i: (0, j + nj)),
            pl.BlockSpec((CONV_WIDTH, tn), lambda j, i: (0, j)),
            pl.BlockSpec((CONV_WIDTH, tn), lambda j, i: (0, j + nj)),
            pl.BlockSpec((1, tn), lambda j, i: (0, j)),
            pl.BlockSpec((1, tn), lambda j, i: (0, j + nj)),
            pl.BlockSpec((n_state, 2, tn), state_map_g),
            pl.BlockSpec((n_state, 2, tn), state_map_u),
        ],
        out_specs=[
            pl.BlockSpec((tm, tn), lambda j, i: (i, j)),
            pl.BlockSpec((n_state, 2, tn), lambda j, i: (i, 0, j)),
            pl.BlockSpec((n_state, 2, tn), lambda j, i: (i, 0, j)),
        ],
        out_shape=[
            jax.ShapeDtypeStruct((m, f), BF16),
            jax.ShapeDtypeStruct((n_last, 2, f), F32),
            jax.ShapeDtypeStruct((n_last, 2, f), F32),
        ],
        scratch_shapes=[pltpu.VMEM((d, 2 * tn), BF16), pltpu.VMEM((tm + SUBLANE, 2 * tn), F32)],
        compiler_params=_params("arbitrary", "arbitrary"),
        name="ffn_up",
    )(h, w_up, w_up, conv_w, conv_w, cb, cb, state, state)
    zlast = jnp.concatenate([zg, zu], axis=-1)
    if tiles_per_seq > 1:
        zlast = zlast[tiles_per_seq - 1::tiles_per_seq]
    return act, zlast


def _ffn_down_kernel(a_ref, b_ref, r_ref, o_ref, acc_ref, *, nk):
    kk = pl.program_id(2)
    part = jnp.dot(a_ref[...], b_ref[...], preferred_element_type=F32)

    @pl.when(kk == 0)
    def _():
        acc_ref[...] = r_ref[...] + part

    @pl.when((kk > 0) & (kk < nk - 1))
    def _():
        acc_ref[...] = acc_ref[...] + part

    @pl.when(kk == nk - 1)
    def _():
        if nk == 1:
            o_ref[...] = r_ref[...] + part
        else:
            o_ref[...] = acc_ref[...] + part


def ffn_down(a, b, res):
    m, k = a.shape
    n = b.shape[1]
    tm = _tile(m, 1024, SUBLANE)
    tn = _tile(n, 512, LANE)
    tk = _tile(k, 5632, LANE)
    nk = k // tk
    return pl.pallas_call(
        functools.partial(_ffn_down_kernel, nk=nk),
        grid=(m // tm, n // tn, nk),
        in_specs=[
            pl.BlockSpec((tm, tk), lambda i, j, kk: (i, kk)),
            pl.BlockSpec((tk, tn), lambda i, j, kk: (kk, j)),
            pl.BlockSpec((tm, tn), lambda i, j, kk: (i, j)),
        ],
        out_specs=pl.BlockSpec((tm, tn), lambda i, j, kk: (i, j)),
        out_shape=jax.ShapeDtypeStruct((m, n), F32),
        scratch_shapes=[pltpu.VMEM((tm, tn), F32)],
        compiler_params=_params("parallel", "arbitrary", "arbitrary"),
        name="ffn_down",
    )(a, b, res)


def _in_offsets(d):
    da = d // 2
    n_heads = d // 256
    n_kv = n_heads // Q_PER_KV
    sizes = (da, da, n_heads * HEAD_DIM, n_kv * HEAD_DIM, n_kv * HEAD_DIM,
             N_IDX_HEADS * IDX_DIM, IDX_DIM, N_IDX_HEADS, d, d)
    offs = [0]
    for s in sizes:
        offs.append(offs[-1] + s)
    return offs


def _layer(x, pos, seq_len, w, past, conv_state, want_vn):
    bsz, t, d = x.shape
    m = bsz * t
    x2 = x.reshape(m, d)
    n_heads = d // 256
    n_kv = n_heads // Q_PER_KV
    offs = _in_offsets(d)
    w_in = w["in_t"]
    assert offs[-1] == w_in.shape[0] and offs[8] - offs[6] <= LANE

    h = rmsnorm(x2, w["norm_attn_g"], BF16)

    rows = max(t, _tile(m, 1024, SUBLANE))
    rope_h = (HEAD_DIM // 8,) + _rope_tables(pos, HEAD_DIM, HEAD_DIM // 4, rows)
    rope_i = (IDX_DIM // 8,) + _rope_tables(pos, IDX_DIM, IDX_DIM // 4, rows)
    ci, s1i, s2i = rope_i[1:]
    lane = jnp.arange(LANE)
    is_w = (lane >= IDX_DIM) & (lane < IDX_DIM + N_IDX_HEADS)
    w_scale = N_IDX_HEADS ** -0.5 * IDX_DIM ** -0.5
    rope_kw = (IDX_DIM // 8,
               jnp.where(is_w, w_scale, jnp.where(lane < IDX_DIM, ci, 1.0)).astype(F32),
               jnp.where(lane < IDX_DIM, s1i, 0.0), jnp.where(lane < IDX_DIM, s2i, 0.0))

    uv = project(h, w_in, offs[0], offs[2] - offs[0], F32)
    q = project(h, w_in, offs[2], offs[3] - offs[2], BF16, rope_h)
    k = project(h, w_in, offs[3], offs[4] - offs[3], F32, rope_h, head_rows=True)
    v = project(h, w_in, offs[4], offs[5] - offs[4], F32, head_rows=True)
    qi = project(h, w_in, offs[5], offs[6] - offs[5], BF16, rope_i)
    kw = project(h, w_in, offs[6], LANE, F32, rope_kw)
    gates = project(h, w["in_gates_t"], 0, 2 * d, F32)

    chunk_rows = min(t, GMLP_CHUNK)
    a_out, vn = gmlp(uv, w["gmlp_norm_g"], w["gmlp_wm"](chunk_rows), w["gmlp_bias"](chunk_rows), want_vn)

    n_valid = t if past is None else past[1].shape[2] + t
    b_out = dsa(qi.reshape(bsz, t, -1), kw.reshape(bsz, t, LANE), q.reshape(bsz, t, -1),
                k.reshape(bsz, t * n_kv, HEAD_DIM), v.reshape(bsz, t * n_kv, HEAD_DIM), past,
                tq=min(DSA_QUERY_BLOCK, t), topk=min(TOPK_MAX, n_valid // 4))

    y = merge(a_out, b_out.reshape(m, n_heads * HEAD_DIM), w["a"], w["b"], gates)
    x2 = matmul_residual(y, w["o"], x2)

    hf = rmsnorm(x2, w["norm_ffn_g"], BF16)
    act, zlast = ffn_up(hf, w["up"], w["conv_w"], w["conv_b"], conv_state, seq_len)
    x2 = ffn_down(act, w["down"], x2)
    kidx = kw[:, :IDX_DIM].reshape(bsz, t, IDX_DIM)
    return (x2.reshape(bsz, t, d), k.reshape(bsz, t, n_kv, HEAD_DIM), v.reshape(bsz, t, n_kv, HEAD_DIM),
            kidx, zlast, vn)


def kernel(x_prompt, x_sample, cache_k, cache_v, cache_kidx, state_ffn_conv, norm_attn_g, w_in, gmlp_norm_g, gmlp_ws, gmlp_b, w_branch_a, w_branch_b, w_out, norm_ffn_g, w_up, conv_w, conv_b, w_down, norm_final_g):
    bsz, s, d = x_prompt.shape
    dbsz, t, _ = x_sample.shape
    depth = w_in.shape[0]
    p_len = cache_k.shape[2]
    da = d // 2
    f2 = w_up.shape[2]
    assert s % GMLP_CHUNK == 0 and GMLP_CHUNK % t == 0 and s % CHUNK == 0

    pos_p = jnp.arange(s, dtype=jnp.int32)
    pos_s = p_len + jnp.arange(t, dtype=jnp.int32)
    ci = jnp.arange(GMLP_CHUNK)
    chunk_mask = (ci[None, :] // CHUNK) <= (ci[:, None] // CHUNK)
    cache_k4 = cache_k.reshape(depth, dbsz, -1, HEAD_DIM)
    cache_v4 = cache_v.reshape(depth, dbsz, -1, HEAD_DIM)
    gate_col0 = _in_offsets(d)[8]

    xp, xs = x_prompt, x_sample
    outs = [[] for _ in range(9)]
    for l in range(depth):
        wm_full = jnp.where(chunk_mask[None], gmlp_ws[l], 0.0)
        bias_rows = jnp.repeat(jnp.transpose(gmlp_b[l]), da // G_A, axis=1)

        def gmlp_wm(rows, wm_full=wm_full):
            reps = GMLP_CHUNK // rows
            blk = wm_full[:, :rows, :rows]
            eye = jnp.eye(reps, dtype=F32)
            return jnp.einsum("ab,gij->gaibj", eye, blk).reshape(G_A, GMLP_CHUNK, GMLP_CHUNK).astype(BF16)

        def gmlp_bias(rows, bias_rows=bias_rows):
            return jnp.tile(bias_rows[:rows], (GMLP_CHUNK // rows, 1))

        w = dict(
            norm_attn_g=norm_attn_g[l], gmlp_norm_g=gmlp_norm_g[l], norm_ffn_g=norm_ffn_g[l],
            gmlp_wm=gmlp_wm, gmlp_bias=gmlp_bias,
            a=w_branch_a[l], b=w_branch_b[l], o=w_out[l], up=w_up[l], conv_w=conv_w[l], conv_b=conv_b[l],
            down=w_down[l].astype(BF16),
        )
        w["in_t"] = jnp.swapaxes(w_in[l], 0, 1).astype(BF16)
        w["in_gates_t"] = jnp.swapaxes(w_in[l], 0, 1)[gate_col0:].astype(BF16)

        xp, kp, vp, kip, cp, _ = _layer(xp, pos_p, s, w, None, jnp.zeros((bsz, CONV_WIDTH - 1, f2), F32),
                                        False)
        past = (l, cache_kidx, cache_k4, cache_v4)
        xs, ks, vs, kis, cs, gv = _layer(xs, pos_s, t, w, past, state_ffn_conv[l], True)
        for lst, val in zip(outs, (kp, vp, kip, cp, ks, vs, kis, cs, gv.reshape(dbsz, t, da))):
            lst.append(val)

    y_prompt = rmsnorm(xp.reshape(bsz * s, d), norm_final_g, F32).reshape(bsz, s, d)
    y_sample = rmsnorm(xs.reshape(dbsz * t, d), norm_final_g, F32).reshape(dbsz, t, d)
    stack = lambda o: o[0][None] if depth == 1 else jnp.stack(o)
    return (y_prompt, y_sample) + tuple(stack(o) for o in outs)
```

```python
import functools
import math

import jax
import jax.numpy as jnp
from jax import lax
from jax.experimental import pallas as pl
from jax.experimental.pallas import tpu as pltpu

CHUNK = 64
GMLP_CHUNK = 128
G_A = 8
HEAD_DIM = 128
Q_PER_KV = 4
N_IDX_HEADS = 16
IDX_DIM = 64
TOPK_MAX = 256
ROPE_THETA = 500000.0
CONV_WIDTH = 3
EPS = 1e-6

LANE = 128
SUBLANE = 8
VMEM_LIMIT_BYTES = 56 * 1024 * 1024

FFN_UP_ROW_SPLITS = 4
DSA_PREFIX_CLASSES = 4
DSA_QUERY_BLOCK = 128
TOPK_STEP_SURPLUS = 2.0
TOPK_PROBES_PER_CHECK = 2

BF16 = jnp.bfloat16
F32 = jnp.float32
NT_DIMS = (((1,), (1,)), ((), ()))


def _params(*semantics):
    return pltpu.CompilerParams(dimension_semantics=semantics, vmem_limit_bytes=VMEM_LIMIT_BYTES)


def _tile(n, pref, align):
    if n <= pref:
        return n
    t = (pref // align) * align
    while t >= align:
        if n % t == 0:
            return t
        t -= align
    raise ValueError(f"no {align}-aligned tile of {n} below {pref}")


def _first_row_tile():
    return pl.program_id(1) == 0


def _sigmoid(x):
    return 0.5 * jnp.tanh(0.5 * x) + 0.5


def _rmsnorm_kernel(x_ref, g_ref, o_ref):
    x = x_ref[...]
    ms = jnp.mean(x * x, axis=-1, keepdims=True)
    o_ref[...] = (x * lax.rsqrt(ms + EPS) * g_ref[...]).astype(o_ref.dtype)


def rmsnorm(x, g, out_dtype):
    n, d = x.shape
    tr = _tile(n, 256, SUBLANE)
    return pl.pallas_call(
        _rmsnorm_kernel,
        grid=(n // tr,),
        in_specs=[pl.BlockSpec((tr, d), lambda i: (i, 0)), pl.BlockSpec((1, d), lambda i: (0, 0))],
        out_specs=pl.BlockSpec((tr, d), lambda i: (i, 0)),
        out_shape=jax.ShapeDtypeStruct((n, d), out_dtype),
        compiler_params=_params("parallel"),
        name="rmsnorm",
    )(x, g.reshape(1, d))


def _rope_slab(x, c, s1, s2, shift):
    return x * c + pltpu.roll(x, LANE - shift, 1) * s1 + pltpu.roll(x, shift, 1) * s2


def _proj_kernel(a_ref, b_ref, *rest, rope_shift, head_rows):
    if rope_shift:
        c_ref, s1_ref, s2_ref, o_ref = rest
        c, s1, s2 = c_ref[...], s1_ref[...], s2_ref[...]
    else:
        (o_ref,) = rest
    tm = a_ref.shape[0]
    acc = lax.dot_general(a_ref[...], b_ref[...], NT_DIMS, preferred_element_type=F32)
    n_slabs = acc.shape[1] // LANE
    for h in range(n_slabs):
        y = acc[:, h * LANE:(h + 1) * LANE]
        if rope_shift:
            y = _rope_slab(y, c, s1, s2, rope_shift)
        if head_rows:
            o_ref[pl.ds(h, tm, stride=n_slabs), :] = y.astype(o_ref.dtype)
        else:
            o_ref[:, h * LANE:(h + 1) * LANE] = y.astype(o_ref.dtype)


def project(a, wt, col0, ncols, out_dtype, rope=None, head_rows=False):
    m, k = a.shape
    tm = _tile(m, 1024, SUBLANE)
    tn = ncols if head_rows else _tile(math.gcd(ncols, col0), 1024, LANE)
    assert col0 % tn == 0 and ncols % tn == 0
    j0 = col0 // tn
    in_specs = [pl.BlockSpec((tm, k), lambda j, i: (i, 0)), pl.BlockSpec((tn, k), lambda j, i: (j + j0, 0))]
    args = [a, wt]
    shift = 0
    if rope is not None:
        shift, tabs = rope[0], rope[1:]
        reps = tabs[0].shape[0] // tm
        assert reps * tm == tabs[0].shape[0]
        for t in tabs:
            in_specs.append(pl.BlockSpec((tm, LANE), lambda j, i: (i % reps, 0)))
            args.append(t)
    if head_rows:
        heads = ncols // LANE
        out_spec = pl.BlockSpec((tm * heads, LANE), lambda j, i: (i, 0))
        out_shape = jax.ShapeDtypeStruct((m * heads, LANE), out_dtype)
    else:
        out_spec = pl.BlockSpec((tm, tn), lambda j, i: (i, j))
        out_shape = jax.ShapeDtypeStruct((m, ncols), out_dtype)
    return pl.pallas_call(
        functools.partial(_proj_kernel, rope_shift=shift, head_rows=head_rows),
        grid=(ncols // tn, m // tm),
        in_specs=in_specs,
        out_specs=out_spec,
        out_shape=out_shape,
        compiler_params=_params("parallel", "arbitrary"),
        name="project",
    )(*args)


def _rope_tables(pos, head_dim, rot_dim, rows):
    half = rot_dim // 2
    inv_freq = ROPE_THETA ** (-jnp.arange(half, dtype=F32) / half)
    ang = pos.astype(F32)[:, None] * inv_freq[None, :]
    cos, sin = jnp.cos(ang), jnp.sin(ang)
    t = pos.shape[0]
    zeros_h = jnp.zeros((t, half), F32)
    rest0 = jnp.zeros((t, head_dim - rot_dim), F32)
    c = jnp.concatenate([cos, cos, jnp.ones((t, head_dim - rot_dim), F32)], axis=1)
    s1 = jnp.concatenate([-sin, zeros_h, rest0], axis=1)
    s2 = jnp.concatenate([zeros_h, sin, rest0], axis=1)
    reps_l = LANE // head_dim
    reps_r = rows // t
    return tuple(jnp.tile(x, (reps_r, reps_l)) for x in (c, s1, s2))


def _gmlp_kernel(u_ref, v_ref, g_ref, wm_ref, b_ref, a_ref, *vn_out, groups):
    v = v_ref[...]
    vn = v * lax.rsqrt(jnp.mean(v * v, axis=-1, keepdims=True) + EPS) * g_ref[...]
    if vn_out:
        vn_out[0][...] = vn
    vb = vn.astype(BF16)
    gw = v.shape[1] // groups
    for g in range(groups):
        sl = slice(g * gw, (g + 1) * gw)
        s = jnp.dot(wm_ref[g], vb[:, sl], preferred_element_type=F32) + b_ref[:, sl]
        a_ref[:, sl] = (u_ref[:, sl] * s).astype(a_ref.dtype)


def gmlp(uv, g_norm, wm, bias, want_vn):
    m, d2 = uv.shape
    da = d2 // 2
    tc = GMLP_CHUNK
    out_shape = [jax.ShapeDtypeStruct((m, da), BF16)]
    out_specs = [pl.BlockSpec((tc, da), lambda i: (i, 0))]
    if want_vn:
        out_shape.append(jax.ShapeDtypeStruct((m, da), F32))
        out_specs.append(pl.BlockSpec((tc, da), lambda i: (i, 0)))
    res = pl.pallas_call(
        functools.partial(_gmlp_kernel, groups=wm.shape[0]),
        grid=(m // tc,),
        in_specs=[
            pl.BlockSpec((tc, da), lambda i: (i, 0)),
            pl.BlockSpec((tc, da), lambda i: (i, 1)),
            pl.BlockSpec((1, da), lambda i: (0, 0)),
            pl.BlockSpec(wm.shape, lambda i: (0, 0, 0)),
            pl.BlockSpec((tc, da), lambda i: (0, 0)),
        ],
        out_specs=out_specs,
        out_shape=out_shape,
        compiler_params=_params("parallel"),
        name="gmlp",
    )(uv, uv, g_norm.reshape(1, da), wm, bias)
    return res if want_vn else (res[0], None)


LOG2_E = 1.4426950408889634
ROW_REDUCE_GROUP = 64
ATTN_ROW_GROUP = 32


def _reduce_rows(x, op):
    pair = {jnp.sum: jnp.add, jnp.min: jnp.minimum, jnp.max: jnp.maximum}[op]
    rows = x.shape[0]
    if rows % ROW_REDUCE_GROUP == 0:
        parts = [x[i:i + ROW_REDUCE_GROUP] for i in range(0, rows, ROW_REDUCE_GROUP)]
        while len(parts) > 1:
            parts = [pair(parts[i], parts[i + 1]) if i + 1 < len(parts) else parts[i]
                     for i in range(0, len(parts), 2)]
        x = parts[0]
    return op(x, axis=0, keepdims=True)

def _dsa_select_attend(qi_ref, kwq_ref, q_ref, klo_ref, khi_ref, ks_ref, vt_ref, o_ref, bias_ref,
                       lg_ref, p_ref, *,
                       s_c, tq, n_kv, topk, row0, n_valid, scale):
    n_tiles = s_c // LANE
    reps = LANE // tq

    def rep_rows(x):
        return x if reps == 1 else jnp.concatenate([x] * reps, axis=0)

    w_t = jnp.transpose(rep_rows(kwq_ref[...]))
    klo = klo_ref[0:s_c, :]
    khi = khi_ref[0:s_c, :]

    score = jnp.zeros((s_c, LANE), F32)
    for pp in range(N_IDX_HEADS // 4):
        qp2 = jnp.concatenate([rep_rows(qi_ref[:, (2 * pp + i) * LANE:(2 * pp + i + 1) * LANE])
                               for i in range(2)], axis=0)
        rel_lo = lax.dot_general(klo, qp2, NT_DIMS, preferred_element_type=F32)
        rel_hi = lax.dot_general(khi, qp2, NT_DIMS, preferred_element_type=F32)
        for i in range(2):
            r0 = IDX_DIM + 2 * (2 * pp + i)
            score = (score + jnp.maximum(rel_lo[:, i * LANE:(i + 1) * LANE], 0.0) * w_t[r0:r0 + 1, :]
                     + jnp.maximum(rel_hi[:, i * LANE:(i + 1) * LANE], 0.0) * w_t[r0 + 1:r0 + 2, :])

    kpos = lax.broadcasted_iota(jnp.int32, (s_c, LANE), 0)
    if row0 is not None:
        qrow = row0 + jnp.bitwise_and(lax.broadcasted_iota(jnp.int32, (s_c, LANE), 1), tq - 1)
        adm = kpos < (jnp.right_shift(qrow, CHUNK.bit_length() - 1) + 1) * CHUNK
    else:
        adm = kpos < n_valid
    score = jnp.where(adm, score, -jnp.inf)
    bias_ref[0:s_c, :] = score

    kf = float(topk)
    lo0 = _reduce_rows(jnp.where(adm, score, jnp.inf), jnp.min)
    hi0 = _reduce_rows(score, jnp.max)
    c_lo0 = _reduce_rows(jnp.where(adm, 1.0, 0.0), jnp.sum)
    c_hi0 = jnp.zeros((1, LANE), F32)

    def probe(t):
        cnt = jnp.zeros((ROW_REDUCE_GROUP, LANE), F32)
        nxt = jnp.full((ROW_REDUCE_GROUP, LANE), jnp.inf, F32)
        for i in range(0, s_c, ROW_REDUCE_GROUP):
            sc = bias_ref[i:i + ROW_REDUCE_GROUP, :]
            above = sc > t
            cnt = cnt + jnp.where(above, 1.0, 0.0)
            nxt = jnp.minimum(nxt, jnp.where(above, sc, jnp.inf))
        return jnp.sum(cnt, axis=0, keepdims=True), jnp.min(nxt, axis=0, keepdims=True)

    def active_of(lo, hi, c_lo):
        return (c_lo > kf) & (hi > lo)

    def cond(carry):
        lo, hi, c_lo, _ = carry
        return jnp.max(jnp.where(active_of(lo, hi, c_lo), 1.0, 0.0)) > 0.0

    def body(carry):
        for _ in range(TOPK_PROBES_PER_CHECK):
            carry = advance(carry)
        return carry

    def advance(carry):
        lo, hi, c_lo, c_hi = carry
        act = active_of(lo, hi, c_lo)
        mid = 0.5 * lo + 0.5 * hi
        step = (c_lo - kf <= TOPK_STEP_SURPLUS) | (mid <= lo) | (mid >= hi)
        t = jnp.where(step, lo, mid)
        cnt, nxt = probe(t)
        take = act & (cnt >= kf)
        drop = act & (cnt < kf)
        return (jnp.where(take, nxt, lo), jnp.where(drop, t, hi),
                jnp.where(take, cnt, c_lo), jnp.where(drop, cnt, c_hi))

    lo, hi, c_lo, c_hi = lax.while_loop(cond, body, (lo0, hi0, c_lo0, c_hi0))
    tie = c_lo > kf
    any_tie = jnp.max(jnp.where(tie, 1.0, 0.0)) > 0.0

    @pl.when(jnp.logical_not(any_tie))
    def _():
        for jt in range(n_tiles):
            sl = slice(jt * LANE, (jt + 1) * LANE)
            bias_ref[sl, :] = jnp.where(bias_ref[sl, :] >= lo, 0.0, -jnp.inf)

    @pl.when(any_tie)
    def _():
        quota = kf - c_hi
        tri = (lax.broadcasted_iota(jnp.int32, (LANE, LANE), 1)
               <= lax.broadcasted_iota(jnp.int32, (LANE, LANE), 0)).astype(F32).astype(BF16)
        before = jnp.zeros((1, LANE), F32)
        for jt in range(n_tiles):
            sl = slice(jt * LANE, (jt + 1) * LANE)
            sc = bias_ref[sl, :]
            cand = jnp.where((sc >= lo) & (sc <= hi), 1.0, 0.0)
            rank = jnp.dot(tri, cand.astype(BF16), preferred_element_type=F32) + before
            keep = (sc > hi) | ((cand > 0.0) & (rank <= quota))
            bias_ref[sl, :] = jnp.where(tie, jnp.where(keep, 0.0, -jnp.inf),
                                        jnp.where(sc >= lo, 0.0, -jnp.inf))
            before = before + jnp.sum(cand, axis=0, keepdims=True)

    width = Q_PER_KV * tq
    n_slabs = width // LANE
    grp = ATTN_ROW_GROUP
    def put_logits(n):
        qn = jnp.concatenate(
            [q_ref[:, (n * Q_PER_KV + g) * HEAD_DIM:(n * Q_PER_KV + g + 1) * HEAD_DIM]
             for g in range(Q_PER_KV)], axis=0)
        lg_ref[n % 2, 0:s_c, :] = lax.dot_general(
            ks_ref[0:s_c, n * HEAD_DIM:(n + 1) * HEAD_DIM], qn, NT_DIMS,
            preferred_element_type=F32) * (scale * LOG2_E)

    put_logits(0)
    for n in range(n_kv):
        if n + 1 < n_kv:
            put_logits(n + 1)
        ksl = slice(n * HEAD_DIM, (n + 1) * HEAD_DIM)
        lgn = lg_ref.at[n % 2]
        pn = p_ref.at[n % 2]
        mpart = jnp.full((grp, width), -jnp.inf, F32)
        for i in range(0, s_c, grp):
            b = bias_ref[i:i + grp, :]
            lg = lgn[i:i + grp, :] + (b if n_slabs == 1 else jnp.concatenate([b] * n_slabs, axis=1))
            lgn[i:i + grp, :] = lg
            mpart = jnp.maximum(mpart, lg)
        mx = jnp.max(mpart, axis=0, keepdims=True)
        dpart = jnp.zeros((grp, width), F32)
        for i in range(0, s_c, grp):
            p = jnp.exp2(lgn[i:i + grp, :] - mx)
            dpart = dpart + p
            pn[i:i + grp, :] = p.astype(BF16)
        den = jnp.sum(dpart, axis=0, keepdims=True)
        o_t = jnp.dot(vt_ref[ksl, 0:s_c], pn[0:s_c, :], preferred_element_type=F32) / den
        for sb in range(n_slabs):
            o = jnp.transpose(o_t[:, sb * LANE:(sb + 1) * LANE])
            for r in range(reps):
                g = sb * reps + r
                hsl = slice((n * Q_PER_KV + g) * HEAD_DIM, (n * Q_PER_KV + g + 1) * HEAD_DIM)
                o_ref[:, hsl] = o[r * tq:(r + 1) * tq].astype(o_ref.dtype)


def _dsa_kernel(qi_ref, kwq_ref, q_ref, kw_ref, k_ref, v_ref, *rest, tq, t_new, p_len, n_classes, topk):
    if p_len:
        (pki_ref, pk_ref, pv_ref, o_ref, klo_ref, khi_ref, ks_ref, vt_ref, bias_ref, lg_ref, p_ref,
         tmp_ref) = rest
    else:
        o_ref, klo_ref, khi_ref, ks_ref, vt_ref, bias_ref, lg_ref, p_ref = rest
        pk_ref = pv_ref = None
    s_pad = ks_ref.shape[0]
    n_valid = p_len + t_new
    n_kv = ks_ref.shape[1] // HEAD_DIM
    j = pl.program_id(1)

    def head_rows(src_ref, rows, n):
        return src_ref[pl.ds(n, rows, stride=n_kv), :]

    @pl.when(j == 0)
    def _():
        lane = lax.broadcasted_iota(jnp.int32, (t_new, LANE), 1)
        new_lo = jnp.where(lane < IDX_DIM, kw_ref[...], 0.0)
        if p_len:
            tmp_ref[...] = jnp.zeros(tmp_ref.shape, F32)
            tmp_ref[0:p_len, 0:IDX_DIM] = pki_ref[...]
            tmp_ref[p_len:n_valid, :] = new_lo
            lo = tmp_ref[...]
        else:
            lo = new_lo
        klo_ref[...] = lo.astype(BF16)
        khi_ref[...] = pltpu.roll(lo, IDX_DIM, 1).astype(BF16)
        tail = jnp.zeros((s_pad - n_valid, HEAD_DIM), F32)
        for n in range(n_kv):
            csl = slice(n * HEAD_DIM, (n + 1) * HEAD_DIM)
            new_k = head_rows(k_ref, t_new, n)
            new_v = head_rows(v_ref, t_new, n)
            if s_pad > n_valid:
                new_k = jnp.concatenate([new_k, tail], axis=0)
                new_v = jnp.concatenate([new_v, tail], axis=0)
            if p_len:
                ks_ref[0:p_len, csl] = head_rows(pk_ref, p_len, n).astype(BF16)
                vt_ref[csl, 0:p_len] = jnp.transpose(head_rows(pv_ref, p_len, n)).astype(BF16)
            ks_ref[p_len:s_pad, csl] = new_k.astype(BF16)
            vt_ref[csl, p_len:s_pad] = jnp.transpose(new_v).astype(BF16)

    common = dict(tq=tq, n_kv=n_kv, topk=topk, n_valid=n_valid, scale=HEAD_DIM ** -0.5)
    refs = (qi_ref, kwq_ref, q_ref, klo_ref, khi_ref, ks_ref, vt_ref, o_ref, bias_ref, lg_ref, p_ref)
    if p_len:
        _dsa_select_attend(*refs, s_c=s_pad, row0=None, **common)
    else:
        per_class = pl.num_programs(1) // n_classes
        for c in range(n_classes):
            @pl.when(j // per_class == c)
            def _(c=c):
                _dsa_select_attend(*refs, s_c=(c + 1) * (s_pad // n_classes), row0=j * tq, **common)


def dsa(qi, kw, q, k, v, past, *, tq, topk):
    b, t, dq = q.shape
    n_kv = k.shape[1] // t
    dkv = n_kv * HEAD_DIM
    nq = t // tq
    row = lambda i, j: (i, j, 0)
    full = lambda i, j: (i, 0, 0)
    in_specs = [
        pl.BlockSpec((None, tq, qi.shape[2]), row),
        pl.BlockSpec((None, tq, LANE), row),
        pl.BlockSpec((None, tq, dq), row),
        pl.BlockSpec((None, t, LANE), full),
        pl.BlockSpec((None, t * n_kv, HEAD_DIM), full),
        pl.BlockSpec((None, t * n_kv, HEAD_DIM), full),
    ]
    args = [qi, kw, q, kw, k, v]
    if past is None:
        p_len = 0
        s_pad = t
        n_classes = DSA_PREFIX_CLASSES if (nq % DSA_PREFIX_CLASSES == 0
                                           and t % (DSA_PREFIX_CLASSES * LANE) == 0) else 1
        assert tq % CHUNK == 0 and LANE % tq == 0
    else:
        layer, pki, pk, pv = past
        p_len = pki.shape[2]
        s_pad = -(-(p_len + t) // LANE) * LANE
        n_classes = 1
        assert nq == 1 and p_len % LANE == 0 and LANE % tq == 0 and Q_PER_KV * tq % LANE == 0
        cache = lambda i, j: (layer, i, 0, 0)
        in_specs += [pl.BlockSpec((None, None, p_len, IDX_DIM), cache),
                     pl.BlockSpec((None, None, p_len * n_kv, HEAD_DIM), cache),
                     pl.BlockSpec((None, None, p_len * n_kv, HEAD_DIM), cache)]
        args += [pki, pk, pv]
    scratch = [pltpu.VMEM((s_pad, LANE), BF16), pltpu.VMEM((s_pad, LANE), BF16),
               pltpu.VMEM((s_pad, dkv), BF16), pltpu.VMEM((dkv, s_pad), BF16),
               pltpu.VMEM((s_pad, LANE), F32),
               pltpu.VMEM((2, s_pad, Q_PER_KV * tq), F32), pltpu.VMEM((2, s_pad, Q_PER_KV * tq), BF16)]
    if past is not None:
        scratch.append(pltpu.VMEM((s_pad, LANE), F32))
    kern = functools.partial(_dsa_kernel, tq=tq, t_new=t, p_len=p_len, n_classes=n_classes, topk=topk)
    return pl.pallas_call(
        kern,
        grid=(b, nq),
        in_specs=in_specs,
        out_specs=pl.BlockSpec((None, tq, dq), row),
        out_shape=jax.ShapeDtypeStruct((b, t, dq), BF16),
        scratch_shapes=scratch,
        compiler_params=_params("arbitrary", "arbitrary"),
        name="dsa",
    )(*args)


def _merge_kernel(a_ref, b_ref, wa_ref, wb_ref, ga_ref, gb_ref, o_ref, waq_ref, wbq_ref):
    @pl.when(_first_row_tile())
    def _():
        waq_ref[...] = wa_ref[...].astype(BF16)
        wbq_ref[...] = wb_ref[...].astype(BF16)

    ya = jnp.dot(a_ref[...], waq_ref[...], preferred_element_type=F32)
    yb = jnp.dot(b_ref[...], wbq_ref[...], preferred_element_type=F32)
    y = _sigmoid(ga_ref[...]) * ya + _sigmoid(gb_ref[...]) * yb
    o_ref[...] = y.astype(o_ref.dtype)


def merge(a, b, wa, wb, gates):
    m, ka = a.shape
    kb = b.shape[1]
    d = wa.shape[1]
    tm = _tile(m, 1024, SUBLANE)
    tn = _tile(d, 512, LANE)
    nj = d // tn
    return pl.pallas_call(
        _merge_kernel,
        grid=(nj, m // tm),
        in_specs=[
            pl.BlockSpec((tm, ka), lambda j, i: (i, 0)),
            pl.BlockSpec((tm, kb), lambda j, i: (i, 0)),
            pl.BlockSpec((ka, tn), lambda j, i: (0, j)),
            pl.BlockSpec((kb, tn), lambda j, i: (0, j)),
            pl.BlockSpec((tm, tn), lambda j, i: (i, j)),
            pl.BlockSpec((tm, tn), lambda j, i: (i, j + nj)),
        ],
        out_specs=pl.BlockSpec((tm, tn), lambda j, i: (i, j)),
        out_shape=jax.ShapeDtypeStruct((m, d), BF16),
        scratch_shapes=[pltpu.VMEM((ka, tn), BF16), pltpu.VMEM((kb, tn), BF16)],
        compiler_params=_params("arbitrary", "arbitrary"),
        name="merge",
    )(a, b, wa, wb, gates, gates)


def _matmul_res_kernel(a_ref, b_ref, r_ref, o_ref, bq_ref):
    @pl.when(_first_row_tile())
    def _():
        bq_ref[...] = b_ref[...].astype(BF16)

    o_ref[...] = r_ref[...] + jnp.dot(a_ref[...], bq_ref[...], preferred_element_type=F32)


def matmul_residual(a, b, res):
    m, k = a.shape
    n = b.shape[1]
    tm = _tile(m, 1024, SUBLANE)
    tn = _tile(n, 512, LANE)
    return pl.pallas_call(
        _matmul_res_kernel,
        grid=(n // tn, m // tm),
        in_specs=[
            pl.BlockSpec((tm, k), lambda j, i: (i, 0)),
            pl.BlockSpec((k, tn), lambda j, i: (0, j)),
            pl.BlockSpec((tm, tn), lambda j, i: (i, j)),
        ],
        out_specs=pl.BlockSpec((tm, tn), lambda j, i: (i, j)),
        out_shape=jax.ShapeDtypeStruct((m, n), F32),
        scratch_shapes=[pltpu.VMEM((k, tn), BF16)],
        compiler_params=_params("arbitrary", "arbitrary"),
        name="matmul_residual",
    )(a, b, res)


def _ffn_up_kernel(h_ref, wg_ref, wu_ref, cwg_ref, cwu_ref, cbg_ref, cbu_ref, sg_ref, su_ref,
                   act_ref, zg_ref, zu_ref, wq_ref, buf_ref, *, seq_len, tiles_per_seq, n_sub):
    tm = h_ref.shape[0]
    tn = act_ref.shape[1]
    pad = SUBLANE

    @pl.when(_first_row_tile())
    def _():
        wq_ref[:, 0:tn] = wg_ref[...].astype(BF16)
        wq_ref[:, tn:2 * tn] = wu_ref[...].astype(BF16)

    cw = jnp.concatenate([cwg_ref[...], cwu_ref[...]], axis=1)
    cb = jnp.concatenate([cbg_ref[...], cbu_ref[...]], axis=1)

    def gated(c):
        gate, up = c[:, :tn], c[:, tn:]
        return (gate * _sigmoid(gate) * up).astype(act_ref.dtype)

    if tiles_per_seq >= 1:
        @pl.when(pl.program_id(1) % tiles_per_seq == 0)
        def _():
            buf_ref[pad - 2:pad, 0:tn] = sg_ref[0]
            buf_ref[pad - 2:pad, tn:2 * tn] = su_ref[0]
        rs = tm // n_sub
        zs = [jnp.dot(h_ref[r * rs:(r + 1) * rs, :], wq_ref[...], preferred_element_type=F32)
              for r in range(n_sub)]
        for r, z in enumerate(zs):
            base = pad + r * rs
            buf_ref[base:base + rs, :] = z
            c = (cb + cw[0:1] * buf_ref[base - 2:base - 2 + rs, :]
                 + cw[1:2] * buf_ref[base - 1:base - 1 + rs, :] + cw[2:3] * z)
            act_ref[r * rs:(r + 1) * rs, :] = gated(c)
        tail = buf_ref[pad + tm - 2:pad + tm, :]
        zg_ref[0] = tail[:, :tn]
        zu_ref[0] = tail[:, tn:]
        buf_ref[pad - 2:pad, :] = tail
    else:
        buf_ref[0:pad, :] = jnp.zeros((pad, 2 * tn), F32)
        buf_ref[pad:pad + tm, :] = jnp.dot(h_ref[...], wq_ref[...], preferred_element_type=F32)
        rowi = lax.broadcasted_iota(jnp.int32, (seq_len, 2 * tn), 0)
        for s in range(tm // seq_len):
            base = pad + s * seq_len
            st0 = jnp.concatenate([sg_ref[s, 0:1, :], su_ref[s, 0:1, :]], axis=1)
            st1 = jnp.concatenate([sg_ref[s, 1:2, :], su_ref[s, 1:2, :]], axis=1)
            z0 = buf_ref[base:base + seq_len, :]
            p1 = jnp.where(rowi == 0, st1, buf_ref[base - 1:base - 1 + seq_len, :])
            p2 = jnp.where(rowi == 0, st0, jnp.where(rowi == 1, st1, buf_ref[base - 2:base - 2 + seq_len, :]))
            act_ref[s * seq_len:(s + 1) * seq_len, :] = gated(cb + cw[0:1] * p2 + cw[1:2] * p1 + cw[2:3] * z0)
            tail = buf_ref[base + seq_len - 2:base + seq_len, :]
            zg_ref[s] = tail[:, :tn]
            zu_ref[s] = tail[:, tn:]


def ffn_up(h, w_up, conv_w, conv_b, state, seq_len):
    m, d = h.shape
    f = w_up.shape[1] // 2
    tm = _tile(m, 1024, SUBLANE)
    tn = _tile(f, 256, LANE)
    nj = f // tn
    if seq_len >= tm:
        assert seq_len % tm == 0
        tiles_per_seq = seq_len // tm
        n_state = 1
        state_map_g = lambda j, i: (i // tiles_per_seq, 0, j)
        state_map_u = lambda j, i: (i // tiles_per_seq, 0, j + nj)
        n_last = m // tm
    else:
        assert tm % seq_len == 0 and seq_len % SUBLANE == 0
        tiles_per_seq = 0
        n_state = tm // seq_len
        state_map_g = lambda j, i: (i, 0, j)
        state_map_u = lambda j, i: (i, 0, j + nj)
        n_last = m // seq_len
    kern = functools.partial(_ffn_up_kernel, seq_len=seq_len, tiles_per_seq=tiles_per_seq,
                             n_sub=FFN_UP_ROW_SPLITS)
    cb = conv_b.reshape(1, 2 * f)
    act, zg, zu = pl.pallas_call(
        kern,
        grid=(nj, m // tm),
        in_specs=[
            pl.BlockSpec((tm, d), lambda j, i: (i, 0)),
            pl.BlockSpec((d, tn), lambda j, i: (0, j)),
            pl.BlockSpec((d, tn), lambda j, i: (0, j + nj)),
            pl.BlockSpec((CONV_WIDTH, tn), lambda j, i: (0, j)),
            pl.BlockSpec((CONV_WIDTH, tn), lambda j, i: (0, j + nj)),
            pl.BlockSpec((1, tn), lambda j, i: (0, j)),
            pl.BlockSpec((1, tn), lambda j, i: (0, j + nj)),
            pl.BlockSpec((n_state, 2, tn), state_map_g),
            pl.BlockSpec((n_state, 2, tn), state_map_u),
        ],
        out_specs=[
            pl.BlockSpec((tm, tn), lambda j, i: (i, j)),
            pl.BlockSpec((n_state, 2, tn), lambda j, i: (i, 0, j)),
            pl.BlockSpec((n_state, 2, tn), lambda j, i: (i, 0, j)),
        ],
        out_shape=[
            jax.ShapeDtypeStruct((m, f), BF16),
            jax.ShapeDtypeStruct((n_last, 2, f), F32),
            jax.ShapeDtypeStruct((n_last, 2, f), F32),
        ],
        scratch_shapes=[pltpu.VMEM((d, 2 * tn), BF16), pltpu.VMEM((tm + SUBLANE, 2 * tn), F32)],
        compiler_params=_params("arbitrary", "arbitrary"),
        name="ffn_up",
    )(h, w_up, w_up, conv_w, conv_w, cb, cb, state, state)
    zlast = jnp.concatenate([zg, zu], axis=-1)
    if tiles_per_seq > 1:
        zlast = zlast[tiles_per_seq - 1::tiles_per_seq]
    return act, zlast


def _ffn_down_kernel(a_ref, b_ref, r_ref, o_ref):
    o_ref[...] = r_ref[...] + jnp.dot(a_ref[...], b_ref[...], preferred_element_type=F32)


def ffn_down(a, b, res):
    m, k = a.shape
    n = b.shape[1]
    tm = _tile(m, 512, SUBLANE)
    tn = _tile(n, 512, LANE)
    return pl.pallas_call(
        _ffn_down_kernel,
        grid=(m // tm, n // tn),
        in_specs=[
            pl.BlockSpec((tm, k), lambda i, j: (i, 0)),
            pl.BlockSpec((k, tn), lambda i, j: (0, j)),
            pl.BlockSpec((tm, tn), lambda i, j: (i, j)),
        ],
        out_specs=pl.BlockSpec((tm, tn), lambda i, j: (i, j)),
        out_shape=jax.ShapeDtypeStruct((m, n), F32),
        compiler_params=_params("parallel", "arbitrary"),
        name="ffn_down",
    )(a, b, res)


def _in_offsets(d):
    da = d // 2
    n_heads = d // 256
    n_kv = n_heads // Q_PER_KV
    sizes = (da, da, n_heads * HEAD_DIM, n_kv * HEAD_DIM, n_kv * HEAD_DIM,
             N_IDX_HEADS * IDX_DIM, IDX_DIM, N_IDX_HEADS, d, d)
    offs = [0]
    for s in sizes:
        offs.append(offs[-1] + s)
    return offs


def _layer(x, pos, seq_len, w, past, conv_state, want_vn):
    bsz, t, d = x.shape
    m = bsz * t
    x2 = x.reshape(m, d)
    n_heads = d // 256
    n_kv = n_heads // Q_PER_KV
    offs = _in_offsets(d)
    w_in = w["in_t"]
    assert offs[-1] == w_in.shape[0] and offs[8] - offs[6] <= LANE

    h = rmsnorm(x2, w["norm_attn_g"], BF16)

    rows = max(t, _tile(m, 1024, SUBLANE))
    rope_h = (HEAD_DIM // 8,) + _rope_tables(pos, HEAD_DIM, HEAD_DIM // 4, rows)
    rope_i = (IDX_DIM // 8,) + _rope_tables(pos, IDX_DIM, IDX_DIM // 4, rows)
    ci, s1i, s2i = rope_i[1:]
    lane = jnp.arange(LANE)
    is_w = (lane >= IDX_DIM) & (lane < IDX_DIM + N_IDX_HEADS)
    w_scale = N_IDX_HEADS ** -0.5 * IDX_DIM ** -0.5
    rope_kw = (IDX_DIM // 8,
               jnp.where(is_w, w_scale, jnp.where(lane < IDX_DIM, ci, 1.0)).astype(F32),
               jnp.where(lane < IDX_DIM, s1i, 0.0), jnp.where(lane < IDX_DIM, s2i, 0.0))

    uv = project(h, w_in, offs[0], offs[2] - offs[0], F32)
    q = project(h, w_in, offs[2], offs[3] - offs[2], BF16, rope_h)
    k = project(h, w_in, offs[3], offs[4] - offs[3], F32, rope_h, head_rows=True)
    v = project(h, w_in, offs[4], offs[5] - offs[4], F32, head_rows=True)
    qi = project(h, w_in, offs[5], offs[6] - offs[5], BF16, rope_i)
    kw = project(h, w_in, offs[6], LANE, F32, rope_kw)
    gates = project(h, w["in_gates_t"], 0, 2 * d, F32)

    chunk_rows = min(t, GMLP_CHUNK)
    a_out, vn = gmlp(uv, w["gmlp_norm_g"], w["gmlp_wm"](chunk_rows), w["gmlp_bias"](chunk_rows), want_vn)

    n_valid = t if past is None else past[1].shape[2] + t
    b_out = dsa(qi.reshape(bsz, t, -1), kw.reshape(bsz, t, LANE), q.reshape(bsz, t, -1),
                k.reshape(bsz, t * n_kv, HEAD_DIM), v.reshape(bsz, t * n_kv, HEAD_DIM), past,
                tq=min(DSA_QUERY_BLOCK, t), topk=min(TOPK_MAX, n_valid // 4))

    y = merge(a_out, b_out.reshape(m, n_heads * HEAD_DIM), w["a"], w["b"], gates)
    x2 = matmul_residual(y, w["o"], x2)

    hf = rmsnorm(x2, w["norm_ffn_g"], BF16)
    act, zlast = ffn_up(hf, w["up"], w["conv_w"], w["conv_b"], conv_state, seq_len)
    x2 = ffn_down(act, w["down"], x2)
    kidx = kw[:, :IDX_DIM].reshape(bsz, t, IDX_DIM)
    return (x2.reshape(bsz, t, d), k.reshape(bsz, t, n_kv, HEAD_DIM), v.reshape(bsz, t, n_kv, HEAD_DIM),
            kidx, zlast, vn)


def kernel(x_prompt, x_sample, cache_k, cache_v, cache_kidx, state_ffn_conv, norm_attn_g, w_in, gmlp_norm_g, gmlp_ws, gmlp_b, w_branch_a, w_branch_b, w_out, norm_ffn_g, w_up, conv_w, conv_b, w_down, norm_final_g):
    bsz, s, d = x_prompt.shape
    dbsz, t, _ = x_sample.shape
    depth = w_in.shape[0]
    p_len = cache_k.shape[2]
    da = d // 2
    f2 = w_up.shape[2]
    assert s % GMLP_CHUNK == 0 and GMLP_CHUNK % t == 0 and s % CHUNK == 0

    pos_p = jnp.arange(s, dtype=jnp.int32)
    pos_s = p_len + jnp.arange(t, dtype=jnp.int32)
    ci = jnp.arange(GMLP_CHUNK)
    chunk_mask = (ci[None, :] // CHUNK) <= (ci[:, None] // CHUNK)
    cache_k4 = cache_k.reshape(depth, dbsz, -1, HEAD_DIM)
    cache_v4 = cache_v.reshape(depth, dbsz, -1, HEAD_DIM)
    gate_col0 = _in_offsets(d)[8]

    xp, xs = x_prompt, x_sample
    outs = [[] for _ in range(9)]
    for l in range(depth):
        wm_full = jnp.where(chunk_mask[None], gmlp_ws[l], 0.0)
        bias_rows = jnp.repeat(jnp.transpose(gmlp_b[l]), da // G_A, axis=1)

        def gmlp_wm(rows, wm_full=wm_full):
            reps = GMLP_CHUNK // rows
            blk = wm_full[:, :rows, :rows]
            eye = jnp.eye(reps, dtype=F32)
            return jnp.einsum("ab,gij->gaibj", eye, blk).reshape(G_A, GMLP_CHUNK, GMLP_CHUNK).astype(BF16)

        def gmlp_bias(rows, bias_rows=bias_rows):
            return jnp.tile(bias_rows[:rows], (GMLP_CHUNK // rows, 1))

        w = dict(
            norm_attn_g=norm_attn_g[l], gmlp_norm_g=gmlp_norm_g[l], norm_ffn_g=norm_ffn_g[l],
            gmlp_wm=gmlp_wm, gmlp_bias=gmlp_bias,
            a=w_branch_a[l], b=w_branch_b[l], o=w_out[l], up=w_up[l], conv_w=conv_w[l], conv_b=conv_b[l],
            down=w_down[l].astype(BF16),
        )
        w["in_t"] = jnp.swapaxes(w_in[l], 0, 1).astype(BF16)
        w["in_gates_t"] = jnp.swapaxes(w_in[l], 0, 1)[gate_col0:].astype(BF16)

        xp, kp, vp, kip, cp, _ = _layer(xp, pos_p, s, w, None, jnp.zeros((bsz, CONV_WIDTH - 1, f2), F32),
                                        False)
        past = (l, cache_kidx, cache_k4, cache_v4)
        xs, ks, vs, kis, cs, gv = _layer(xs, pos_s, t, w, past, state_ffn_conv[l], True)
        for lst, val in zip(outs, (kp, vp, kip, cp, ks, vs, kis, cs, gv.reshape(dbsz, t, da))):
            lst.append(val)

    y_prompt = rmsnorm(xp.reshape(bsz * s, d), norm_final_g, F32).reshape(bsz, s, d)
    y_sample = rmsnorm(xs.reshape(dbsz * t, d), norm_final_g, F32).reshape(dbsz, t, d)
    stack = lambda o: o[0][None] if depth == 1 else jnp.stack(o)
    return (y_prompt, y_sample) + tuple(stack(o) for o in outs)
```

```python
import functools
import math

import jax
import jax.numpy as jnp
from jax import lax
from jax.experimental import pallas as pl
from jax.experimental.pallas import tpu as pltpu

CHUNK = 64
GMLP_CHUNK = 128
G_A = 8
HEAD_DIM = 128
Q_PER_KV = 4
N_IDX_HEADS = 16
IDX_DIM = 64
TOPK_MAX = 256
ROPE_THETA = 500000.0
CONV_WIDTH = 3
EPS = 1e-6

LANE = 128
SUBLANE = 8
VMEM_LIMIT_BYTES = 56 * 1024 * 1024

FFN_UP_ROW_SPLITS = 4
DSA_PREFIX_CLASSES = 4
DSA_QUERY_BLOCK = 128
TOPK_STEP_SURPLUS = 2.0
TOPK_PROBES_PER_CHECK = 2

BF16 = jnp.bfloat16
F32 = jnp.float32
NT_DIMS = (((1,), (1,)), ((), ()))


def _params(*semantics):
    return pltpu.CompilerParams(dimension_semantics=semantics, vmem_limit_bytes=VMEM_LIMIT_BYTES)


def _tile(n, pref, align):
    if n <= pref:
        return n
    t = (pref // align) * align
    while t >= align:
        if n % t == 0:
            return t
        t -= align
    raise ValueError(f"no {align}-aligned tile of {n} below {pref}")


def _first_row_tile():
    return pl.program_id(1) == 0


def _sigmoid(x):
    return 0.5 * jnp.tanh(0.5 * x) + 0.5


def _rmsnorm_kernel(x_ref, g_ref, o_ref):
    x = x_ref[...]
    ms = jnp.mean(x * x, axis=-1, keepdims=True)
    o_ref[...] = (x * lax.rsqrt(ms + EPS) * g_ref[...]).astype(o_ref.dtype)


def rmsnorm(x, g, out_dtype):
    n, d = x.shape
    tr = _tile(n, 256, SUBLANE)
    return pl.pallas_call(
        _rmsnorm_kernel,
        grid=(n // tr,),
        in_specs=[pl.BlockSpec((tr, d), lambda i: (i, 0)), pl.BlockSpec((1, d), lambda i: (0, 0))],
        out_specs=pl.BlockSpec((tr, d), lambda i: (i, 0)),
        out_shape=jax.ShapeDtypeStruct((n, d), out_dtype),
        compiler_params=_params("parallel"),
        name="rmsnorm",
    )(x, g.reshape(1, d))


def _rope_slab(x, c, s1, s2, shift):
    return x * c + pltpu.roll(x, LANE - shift, 1) * s1 + pltpu.roll(x, shift, 1) * s2


def _proj_kernel(a_ref, b_ref, *rest, rope_shift):
    acc = lax.dot_general(a_ref[...], b_ref[...], NT_DIMS, preferred_element_type=F32)
    if rope_shift:
        c_ref, s1_ref, s2_ref, o_ref = rest
        c, s1, s2 = c_ref[...], s1_ref[...], s2_ref[...]
        for h in range(acc.shape[1] // LANE):
            sl = slice(h * LANE, (h + 1) * LANE)
            o_ref[:, sl] = _rope_slab(acc[:, sl], c, s1, s2, rope_shift).astype(o_ref.dtype)
    else:
        (o_ref,) = rest
        o_ref[...] = acc.astype(o_ref.dtype)


def project(a, wt, col0, ncols, out_dtype, rope=None):
    m, k = a.shape
    tm = _tile(m, 1024, SUBLANE)
    tn = _tile(math.gcd(ncols, col0), 1024, LANE)
    assert col0 % tn == 0 and ncols % tn == 0
    j0 = col0 // tn
    in_specs = [pl.BlockSpec((tm, k), lambda j, i: (i, 0)), pl.BlockSpec((tn, k), lambda j, i: (j + j0, 0))]
    args = [a, wt]
    shift = 0
    if rope is not None:
        shift, tabs = rope[0], rope[1:]
        reps = tabs[0].shape[0] // tm
        assert reps * tm == tabs[0].shape[0]
        for t in tabs:
            in_specs.append(pl.BlockSpec((tm, LANE), lambda j, i: (i % reps, 0)))
            args.append(t)
    return pl.pallas_call(
        functools.partial(_proj_kernel, rope_shift=shift),
        grid=(ncols // tn, m // tm),
        in_specs=in_specs,
        out_specs=pl.BlockSpec((tm, tn), lambda j, i: (i, j)),
        out_shape=jax.ShapeDtypeStruct((m, ncols), out_dtype),
        compiler_params=_params("parallel", "arbitrary"),
        name="project",
    )(*args)


def _proj_kv_kernel(a_ref, b_ref, ck_ref, s1k_ref, s2k_ref, cw_ref, s1w_ref, s2w_ref,
                    k_ref, v_ref, kw_ref, *, shift_k, shift_w):
    g = pl.program_id(1)
    tm = a_ref.shape[0]
    acc = lax.dot_general(a_ref[...], b_ref[...], NT_DIMS, preferred_element_type=F32)
    n_heads = acc.shape[1] // LANE

    @pl.when(g == 0)
    def _():
        c, s1, s2 = ck_ref[...], s1k_ref[...], s2k_ref[...]
        for h in range(n_heads):
            k_ref[pl.ds(h, tm, stride=n_heads), :] = _rope_slab(acc[:, h * LANE:(h + 1) * LANE], c, s1, s2, shift_k)

    @pl.when(g == 1)
    def _():
        for h in range(n_heads):
            v_ref[pl.ds(h, tm, stride=n_heads), :] = acc[:, h * LANE:(h + 1) * LANE]

    @pl.when(g == 2)
    def _():
        kw_ref[...] = _rope_slab(acc[:, 0:LANE], cw_ref[...], s1w_ref[...], s2w_ref[...], shift_w)


def project_kv(a, wt, col_k, col_v, col_kw, width, rope_k, rope_kw):
    m, kdim = a.shape
    tm = _tile(m, 1024, SUBLANE)
    assert col_k % width == 0 and col_v % width == 0 and col_kw % width == 0
    assert col_kw + width <= wt.shape[0] and width % LANE == 0
    blocks = (col_k // width, col_v // width, col_kw // width)
    heads = width // LANE

    def w_map(i, g):
        return (jnp.where(g == 0, blocks[0], jnp.where(g == 1, blocks[1], blocks[2])), 0)

    tabs = rope_k[1:] + rope_kw[1:]
    reps = tabs[0].shape[0] // tm
    assert reps * tm == tabs[0].shape[0]
    return pl.pallas_call(
        functools.partial(_proj_kv_kernel, shift_k=rope_k[0], shift_w=rope_kw[0]),
        grid=(m // tm, 3),
        in_specs=[pl.BlockSpec((tm, kdim), lambda i, g: (i, 0)), pl.BlockSpec((width, kdim), w_map)]
        + [pl.BlockSpec((tm, LANE), lambda i, g: (i % reps, 0)) for _ in tabs],
        out_specs=[
            pl.BlockSpec((tm * heads, LANE), lambda i, g: (i, 0)),
            pl.BlockSpec((tm * heads, LANE), lambda i, g: (i, 0)),
            pl.BlockSpec((tm, LANE), lambda i, g: (i, 0)),
        ],
        out_shape=[
            jax.ShapeDtypeStruct((m * heads, LANE), F32),
            jax.ShapeDtypeStruct((m * heads, LANE), F32),
            jax.ShapeDtypeStruct((m, LANE), F32),
        ],
        compiler_params=_params("parallel", "arbitrary"),
        name="project_kv",
    )(a, wt, *tabs)


def _rope_tables(pos, head_dim, rot_dim, rows):
    half = rot_dim // 2
    inv_freq = ROPE_THETA ** (-jnp.arange(half, dtype=F32) / half)
    ang = pos.astype(F32)[:, None] * inv_freq[None, :]
    cos, sin = jnp.cos(ang), jnp.sin(ang)
    t = pos.shape[0]
    zeros_h = jnp.zeros((t, half), F32)
    rest0 = jnp.zeros((t, head_dim - rot_dim), F32)
    c = jnp.concatenate([cos, cos, jnp.ones((t, head_dim - rot_dim), F32)], axis=1)
    s1 = jnp.concatenate([-sin, zeros_h, rest0], axis=1)
    s2 = jnp.concatenate([zeros_h, sin, rest0], axis=1)
    reps_l = LANE // head_dim
    reps_r = rows // t
    return tuple(jnp.tile(x, (reps_r, reps_l)) for x in (c, s1, s2))


def _gmlp_kernel(u_ref, v_ref, g_ref, wm_ref, b_ref, a_ref, *vn_out, groups):
    v = v_ref[...]
    vn = v * lax.rsqrt(jnp.mean(v * v, axis=-1, keepdims=True) + EPS) * g_ref[...]
    if vn_out:
        vn_out[0][...] = vn
    vb = vn.astype(BF16)
    gw = v.shape[1] // groups
    for g in range(groups):
        sl = slice(g * gw, (g + 1) * gw)
        s = jnp.dot(wm_ref[g], vb[:, sl], preferred_element_type=F32) + b_ref[:, sl]
        a_ref[:, sl] = (u_ref[:, sl] * s).astype(a_ref.dtype)


def gmlp(uv, g_norm, wm, bias, want_vn):
    m, d2 = uv.shape
    da = d2 // 2
    tc = GMLP_CHUNK
    out_shape = [jax.ShapeDtypeStruct((m, da), BF16)]
    out_specs = [pl.BlockSpec((tc, da), lambda i: (i, 0))]
    if want_vn:
        out_shape.append(jax.ShapeDtypeStruct((m, da), F32))
        out_specs.append(pl.BlockSpec((tc, da), lambda i: (i, 0)))
    res = pl.pallas_call(
        functools.partial(_gmlp_kernel, groups=wm.shape[0]),
        grid=(m // tc,),
        in_specs=[
            pl.BlockSpec((tc, da), lambda i: (i, 0)),
            pl.BlockSpec((tc, da), lambda i: (i, 1)),
            pl.BlockSpec((1, da), lambda i: (0, 0)),
            pl.BlockSpec(wm.shape, lambda i: (0, 0, 0)),
            pl.BlockSpec((tc, da), lambda i: (0, 0)),
        ],
        out_specs=out_specs,
        out_shape=out_shape,
        compiler_params=_params("parallel"),
        name="gmlp",
    )(uv, uv, g_norm.reshape(1, da), wm, bias)
    return res if want_vn else (res[0], None)


LOG2_E = 1.4426950408889634
ROW_REDUCE_GROUP = 64
ATTN_ROW_GROUP = 32


def _reduce_rows(x, op):
    pair = {jnp.sum: jnp.add, jnp.min: jnp.minimum, jnp.max: jnp.maximum}[op]
    rows = x.shape[0]
    if rows % ROW_REDUCE_GROUP == 0:
        parts = [x[i:i + ROW_REDUCE_GROUP] for i in range(0, rows, ROW_REDUCE_GROUP)]
        while len(parts) > 1:
            parts = [pair(parts[i], parts[i + 1]) if i + 1 < len(parts) else parts[i]
                     for i in range(0, len(parts), 2)]
        x = parts[0]
    return op(x, axis=0, keepdims=True)

def _dsa_select_attend(qi_ref, kwq_ref, q_ref, klo_ref, khi_ref, ks_ref, vt_ref, o_ref, bias_ref,
                       lg_ref, p_ref, *,
                       s_c, tq, n_kv, topk, row0, n_valid, scale):
    n_tiles = s_c // LANE
    reps = LANE // tq

    def rep_rows(x):
        return x if reps == 1 else jnp.concatenate([x] * reps, axis=0)

    w_t = jnp.transpose(rep_rows(kwq_ref[...]))
    klo = klo_ref[0:s_c, :]
    khi = khi_ref[0:s_c, :]

    score = jnp.zeros((s_c, LANE), F32)
    for pp in range(N_IDX_HEADS // 4):
        qp2 = jnp.concatenate([rep_rows(qi_ref[:, (2 * pp + i) * LANE:(2 * pp + i + 1) * LANE])
                               for i in range(2)], axis=0)
        rel_lo = lax.dot_general(klo, qp2, NT_DIMS, preferred_element_type=F32)
        rel_hi = lax.dot_general(khi, qp2, NT_DIMS, preferred_element_type=F32)
        for i in range(2):
            r0 = IDX_DIM + 2 * (2 * pp + i)
            score = (score + jnp.maximum(rel_lo[:, i * LANE:(i + 1) * LANE], 0.0) * w_t[r0:r0 + 1, :]
                     + jnp.maximum(rel_hi[:, i * LANE:(i + 1) * LANE], 0.0) * w_t[r0 + 1:r0 + 2, :])

    kpos = lax.broadcasted_iota(jnp.int32, (s_c, LANE), 0)
    if row0 is not None:
        qrow = row0 + jnp.bitwise_and(lax.broadcasted_iota(jnp.int32, (s_c, LANE), 1), tq - 1)
        adm = kpos < (jnp.right_shift(qrow, CHUNK.bit_length() - 1) + 1) * CHUNK
    else:
        adm = kpos < n_valid
    score = jnp.where(adm, score, -jnp.inf)
    bias_ref[0:s_c, :] = score

    kf = float(topk)
    lo0 = _reduce_rows(jnp.where(adm, score, jnp.inf), jnp.min)
    hi0 = _reduce_rows(score, jnp.max)
    c_lo0 = _reduce_rows(jnp.where(adm, 1.0, 0.0), jnp.sum)
    c_hi0 = jnp.zeros((1, LANE), F32)

    def probe(t):
        cnt = jnp.zeros((ROW_REDUCE_GROUP, LANE), F32)
        nxt = jnp.full((ROW_REDUCE_GROUP, LANE), jnp.inf, F32)
        for i in range(0, s_c, ROW_REDUCE_GROUP):
            sc = bias_ref[i:i + ROW_REDUCE_GROUP, :]
            above = sc > t
            cnt = cnt + jnp.where(above, 1.0, 0.0)
            nxt = jnp.minimum(nxt, jnp.where(above, sc, jnp.inf))
        return jnp.sum(cnt, axis=0, keepdims=True), jnp.min(nxt, axis=0, keepdims=True)

    def active_of(lo, hi, c_lo):
        return (c_lo > kf) & (hi > lo)

    def cond(carry):
        lo, hi, c_lo, _ = carry
        return jnp.max(jnp.where(active_of(lo, hi, c_lo), 1.0, 0.0)) > 0.0

    def body(carry):
        for _ in range(TOPK_PROBES_PER_CHECK):
            carry = advance(carry)
        return carry

    def advance(carry):
        lo, hi, c_lo, c_hi = carry
        act = active_of(lo, hi, c_lo)
        mid = 0.5 * lo + 0.5 * hi
        step = (c_lo - kf <= TOPK_STEP_SURPLUS) | (mid <= lo) | (mid >= hi)
        t = jnp.where(step, lo, mid)
        cnt, nxt = probe(t)
        take = act & (cnt >= kf)
        drop = act & (cnt < kf)
        return (jnp.where(take, nxt, lo), jnp.where(drop, t, hi),
                jnp.where(take, cnt, c_lo), jnp.where(drop, cnt, c_hi))

    lo, hi, c_lo, c_hi = lax.while_loop(cond, body, (lo0, hi0, c_lo0, c_hi0))
    tie = c_lo > kf
    any_tie = jnp.max(jnp.where(tie, 1.0, 0.0)) > 0.0

    @pl.when(jnp.logical_not(any_tie))
    def _():
        for jt in range(n_tiles):
            sl = slice(jt * LANE, (jt + 1) * LANE)
            bias_ref[sl, :] = jnp.where(bias_ref[sl, :] >= lo, 0.0, -jnp.inf)

    @pl.when(any_tie)
    def _():
        quota = kf - c_hi
        tri = (lax.broadcasted_iota(jnp.int32, (LANE, LANE), 1)
               <= lax.broadcasted_iota(jnp.int32, (LANE, LANE), 0)).astype(F32).astype(BF16)
        before = jnp.zeros((1, LANE), F32)
        for jt in range(n_tiles):
            sl = slice(jt * LANE, (jt + 1) * LANE)
            sc = bias_ref[sl, :]
            cand = jnp.where((sc >= lo) & (sc <= hi), 1.0, 0.0)
            rank = jnp.dot(tri, cand.astype(BF16), preferred_element_type=F32) + before
            keep = (sc > hi) | ((cand > 0.0) & (rank <= quota))
            bias_ref[sl, :] = jnp.where(tie, jnp.where(keep, 0.0, -jnp.inf),
                                        jnp.where(sc >= lo, 0.0, -jnp.inf))
            before = before + jnp.sum(cand, axis=0, keepdims=True)

    width = Q_PER_KV * tq
    n_slabs = width // LANE
    grp = ATTN_ROW_GROUP
    def put_logits(n):
        qn = jnp.concatenate(
            [q_ref[:, (n * Q_PER_KV + g) * HEAD_DIM:(n * Q_PER_KV + g + 1) * HEAD_DIM]
             for g in range(Q_PER_KV)], axis=0)
        lg_ref[n % 2, 0:s_c, :] = lax.dot_general(
            ks_ref[0:s_c, n * HEAD_DIM:(n + 1) * HEAD_DIM], qn, NT_DIMS,
            preferred_element_type=F32) * (scale * LOG2_E)

    put_logits(0)
    for n in range(n_kv):
        if n + 1 < n_kv:
            put_logits(n + 1)
        ksl = slice(n * HEAD_DIM, (n + 1) * HEAD_DIM)
        lgn = lg_ref.at[n % 2]
        pn = p_ref.at[n % 2]
        mpart = jnp.full((grp, width), -jnp.inf, F32)
        for i in range(0, s_c, grp):
            b = bias_ref[i:i + grp, :]
            lg = lgn[i:i + grp, :] + (b if n_slabs == 1 else jnp.concatenate([b] * n_slabs, axis=1))
            lgn[i:i + grp, :] = lg
            mpart = jnp.maximum(mpart, lg)
        mx = jnp.max(mpart, axis=0, keepdims=True)
        dpart = jnp.zeros((grp, width), F32)
        for i in range(0, s_c, grp):
            p = jnp.exp2(lgn[i:i + grp, :] - mx)
            dpart = dpart + p
            pn[i:i + grp, :] = p.astype(BF16)
        den = jnp.sum(dpart, axis=0, keepdims=True)
        o_t = jnp.dot(vt_ref[ksl, 0:s_c], pn[0:s_c, :], preferred_element_type=F32) / den
        for sb in range(n_slabs):
            o = jnp.transpose(o_t[:, sb * LANE:(sb + 1) * LANE])
            for r in range(reps):
                g = sb * reps + r
                hsl = slice((n * Q_PER_KV + g) * HEAD_DIM, (n * Q_PER_KV + g + 1) * HEAD_DIM)
                o_ref[:, hsl] = o[r * tq:(r + 1) * tq].astype(o_ref.dtype)


def _dsa_kernel(qi_ref, kwq_ref, q_ref, kw_ref, k_ref, v_ref, *rest, tq, t_new, p_len, n_classes, topk):
    if p_len:
        (pki_ref, pk_ref, pv_ref, o_ref, klo_ref, khi_ref, ks_ref, vt_ref, bias_ref, lg_ref, p_ref,
         tmp_ref) = rest
    else:
        o_ref, klo_ref, khi_ref, ks_ref, vt_ref, bias_ref, lg_ref, p_ref = rest
        pk_ref = pv_ref = None
    s_pad = ks_ref.shape[0]
    n_valid = p_len + t_new
    n_kv = ks_ref.shape[1] // HEAD_DIM
    j = pl.program_id(1)

    def head_rows(src_ref, rows, n):
        return src_ref[pl.ds(n, rows, stride=n_kv), :]

    @pl.when(j == 0)
    def _():
        lane = lax.broadcasted_iota(jnp.int32, (t_new, LANE), 1)
        new_lo = jnp.where(lane < IDX_DIM, kw_ref[...], 0.0)
        if p_len:
            tmp_ref[...] = jnp.zeros(tmp_ref.shape, F32)
            tmp_ref[0:p_len, 0:IDX_DIM] = pki_ref[...]
            tmp_ref[p_len:n_valid, :] = new_lo
            lo = tmp_ref[...]
        else:
            lo = new_lo
        klo_ref[...] = lo.astype(BF16)
        khi_ref[...] = pltpu.roll(lo, IDX_DIM, 1).astype(BF16)
        tail = jnp.zeros((s_pad - n_valid, HEAD_DIM), F32)
        for n in range(n_kv):
            csl = slice(n * HEAD_DIM, (n + 1) * HEAD_DIM)
            new_k = head_rows(k_ref, t_new, n)
            new_v = head_rows(v_ref, t_new, n)
            if s_pad > n_valid:
                new_k = jnp.concatenate([new_k, tail], axis=0)
                new_v = jnp.concatenate([new_v, tail], axis=0)
            if p_len:
                ks_ref[0:p_len, csl] = head_rows(pk_ref, p_len, n).astype(BF16)
                vt_ref[csl, 0:p_len] = jnp.transpose(head_rows(pv_ref, p_len, n)).astype(BF16)
            ks_ref[p_len:s_pad, csl] = new_k.astype(BF16)
            vt_ref[csl, p_len:s_pad] = jnp.transpose(new_v).astype(BF16)

    common = dict(tq=tq, n_kv=n_kv, topk=topk, n_valid=n_valid, scale=HEAD_DIM ** -0.5)
    refs = (qi_ref, kwq_ref, q_ref, klo_ref, khi_ref, ks_ref, vt_ref, o_ref, bias_ref, lg_ref, p_ref)
    if p_len:
        _dsa_select_attend(*refs, s_c=s_pad, row0=None, **common)
    else:
        per_class = pl.num_programs(1) // n_classes
        for c in range(n_classes):
            @pl.when(j // per_class == c)
            def _(c=c):
                _dsa_select_attend(*refs, s_c=(c + 1) * (s_pad // n_classes), row0=j * tq, **common)


def dsa(qi, kw, q, k, v, past, *, tq, topk):
    b, t, dq = q.shape
    n_kv = k.shape[1] // t
    dkv = n_kv * HEAD_DIM
    nq = t // tq
    row = lambda i, j: (i, j, 0)
    full = lambda i, j: (i, 0, 0)
    in_specs = [
        pl.BlockSpec((None, tq, qi.shape[2]), row),
        pl.BlockSpec((None, tq, LANE), row),
        pl.BlockSpec((None, tq, dq), row),
        pl.BlockSpec((None, t, LANE), full),
        pl.BlockSpec((None, t * n_kv, HEAD_DIM), full),
        pl.BlockSpec((None, t * n_kv, HEAD_DIM), full),
    ]
    args = [qi, kw, q, kw, k, v]
    if past is None:
        p_len = 0
        s_pad = t
        n_classes = DSA_PREFIX_CLASSES if (nq % DSA_PREFIX_CLASSES == 0
                                           and t % (DSA_PREFIX_CLASSES * LANE) == 0) else 1
        assert tq % CHUNK == 0 and LANE % tq == 0
    else:
        layer, pki, pk, pv = past
        p_len = pki.shape[2]
        s_pad = -(-(p_len + t) // LANE) * LANE
        n_classes = 1
        assert nq == 1 and p_len % LANE == 0 and LANE % tq == 0 and Q_PER_KV * tq % LANE == 0
        cache = lambda i, j: (layer, i, 0, 0)
        in_specs += [pl.BlockSpec((None, None, p_len, IDX_DIM), cache),
                     pl.BlockSpec((None, None, p_len * n_kv, HEAD_DIM), cache),
                     pl.BlockSpec((None, None, p_len * n_kv, HEAD_DIM), cache)]
        args += [pki, pk, pv]
    scratch = [pltpu.VMEM((s_pad, LANE), BF16), pltpu.VMEM((s_pad, LANE), BF16),
               pltpu.VMEM((s_pad, dkv), BF16), pltpu.VMEM((dkv, s_pad), BF16),
               pltpu.VMEM((s_pad, LANE), F32),
               pltpu.VMEM((2, s_pad, Q_PER_KV * tq), F32), pltpu.VMEM((2, s_pad, Q_PER_KV * tq), BF16)]
    if past is not None:
        scratch.append(pltpu.VMEM((s_pad, LANE), F32))
    kern = functools.partial(_dsa_kernel, tq=tq, t_new=t, p_len=p_len, n_classes=n_classes, topk=topk)
    return pl.pallas_call(
        kern,
        grid=(b, nq),
        in_specs=in_specs,
        out_specs=pl.BlockSpec((None, tq, dq), row),
        out_shape=jax.ShapeDtypeStruct((b, t, dq), BF16),
        scratch_shapes=scratch,
        compiler_params=_params("arbitrary", "arbitrary"),
        name="dsa",
    )(*args)


def _merge_kernel(a_ref, b_ref, wa_ref, wb_ref, ga_ref, gb_ref, o_ref, waq_ref, wbq_ref):
    @pl.when(_first_row_tile())
    def _():
        waq_ref[...] = wa_ref[...].astype(BF16)
        wbq_ref[...] = wb_ref[...].astype(BF16)

    ya = jnp.dot(a_ref[...], waq_ref[...], preferred_element_type=F32)
    yb = jnp.dot(b_ref[...], wbq_ref[...], preferred_element_type=F32)
    y = _sigmoid(ga_ref[...]) * ya + _sigmoid(gb_ref[...]) * yb
    o_ref[...] = y.astype(o_ref.dtype)


def merge(a, b, wa, wb, gates):
    m, ka = a.shape
    kb = b.shape[1]
    d = wa.shape[1]
    tm = _tile(m, 1024, SUBLANE)
    tn = _tile(d, 512, LANE)
    nj = d // tn
    return pl.pallas_call(
        _merge_kernel,
        grid=(nj, m // tm),
        in_specs=[
            pl.BlockSpec((tm, ka), lambda j, i: (i, 0)),
            pl.BlockSpec((tm, kb), lambda j, i: (i, 0)),
            pl.BlockSpec((ka, tn), lambda j, i: (0, j)),
            pl.BlockSpec((kb, tn), lambda j, i: (0, j)),
            pl.BlockSpec((tm, tn), lambda j, i: (i, j)),
            pl.BlockSpec((tm, tn), lambda j, i: (i, j + nj)),
        ],
        out_specs=pl.BlockSpec((tm, tn), lambda j, i: (i, j)),
        out_shape=jax.ShapeDtypeStruct((m, d), BF16),
        scratch_shapes=[pltpu.VMEM((ka, tn), BF16), pltpu.VMEM((kb, tn), BF16)],
        compiler_params=_params("arbitrary", "arbitrary"),
        name="merge",
    )(a, b, wa, wb, gates, gates)


def _matmul_res_kernel(a_ref, b_ref, r_ref, o_ref, bq_ref):
    @pl.when(_first_row_tile())
    def _():
        bq_ref[...] = b_ref[...].astype(BF16)

    o_ref[...] = r_ref[...] + jnp.dot(a_ref[...], bq_ref[...], preferred_element_type=F32)


def matmul_residual(a, b, res):
    m, k = a.shape
    n = b.shape[1]
    tm = _tile(m, 1024, SUBLANE)
    tn = _tile(n, 512, LANE)
    return pl.pallas_call(
        _matmul_res_kernel,
        grid=(n // tn, m // tm),
        in_specs=[
            pl.BlockSpec((tm, k), lambda j, i: (i, 0)),
            pl.BlockSpec((k, tn), lambda j, i: (0, j)),
            pl.BlockSpec((tm, tn), lambda j, i: (i, j)),
        ],
        out_specs=pl.BlockSpec((tm, tn), lambda j, i: (i, j)),
        out_shape=jax.ShapeDtypeStruct((m, n), F32),
        scratch_shapes=[pltpu.VMEM((k, tn), BF16)],
        compiler_params=_params("arbitrary", "arbitrary"),
        name="matmul_residual",
    )(a, b, res)


def _ffn_up_kernel(h_ref, wg_ref, wu_ref, cwg_ref, cwu_ref, cbg_ref, cbu_ref, sg_ref, su_ref,
                   act_ref, zg_ref, zu_ref, wq_ref, buf_ref, *, seq_len, tiles_per_seq, n_sub):
    tm = h_ref.shape[0]
    tn = act_ref.shape[1]
    pad = SUBLANE

    @pl.when(_first_row_tile())
    def _():
        wq_ref[:, 0:tn] = wg_ref[...].astype(BF16)
        wq_ref[:, tn:2 * tn] = wu_ref[...].astype(BF16)

    cw = jnp.concatenate([cwg_ref[...], cwu_ref[...]], axis=1)
    cb = jnp.concatenate([cbg_ref[...], cbu_ref[...]], axis=1)

    def gated(c):
        gate, up = c[:, :tn], c[:, tn:]
        return (gate * _sigmoid(gate) * up).astype(act_ref.dtype)

    if tiles_per_seq >= 1:
        @pl.when(pl.program_id(1) % tiles_per_seq == 0)
        def _():
            buf_ref[pad - 2:pad, 0:tn] = sg_ref[0]
            buf_ref[pad - 2:pad, tn:2 * tn] = su_ref[0]
        rs = tm // n_sub
        zs = [jnp.dot(h_ref[r * rs:(r + 1) * rs, :], wq_ref[...], preferred_element_type=F32)
              for r in range(n_sub)]
        for r, z in enumerate(zs):
            base = pad + r * rs
            buf_ref[base:base + rs, :] = z
            c = (cb + cw[0:1] * buf_ref[base - 2:base - 2 + rs, :]
                 + cw[1:2] * buf_ref[base - 1:base - 1 + rs, :] + cw[2:3] * z)
            act_ref[r * rs:(r + 1) * rs, :] = gated(c)
        tail = buf_ref[pad + tm - 2:pad + tm, :]
        zg_ref[0] = tail[:, :tn]
        zu_ref[0] = tail[:, tn:]
        buf_ref[pad - 2:pad, :] = tail
    else:
        buf_ref[0:pad, :] = jnp.zeros((pad, 2 * tn), F32)
        buf_ref[pad:pad + tm, :] = jnp.dot(h_ref[...], wq_ref[...], preferred_element_type=F32)
        rowi = lax.broadcasted_iota(jnp.int32, (seq_len, 2 * tn), 0)
        for s in range(tm // seq_len):
            base = pad + s * seq_len
            st0 = jnp.concatenate([sg_ref[s, 0:1, :], su_ref[s, 0:1, :]], axis=1)
            st1 = jnp.concatenate([sg_ref[s, 1:2, :], su_ref[s, 1:2, :]], axis=1)
            z0 = buf_ref[base:base + seq_len, :]
            p1 = jnp.where(rowi == 0, st1, buf_ref[base - 1:base - 1 + seq_len, :])
            p2 = jnp.where(rowi == 0, st0, jnp.where(rowi == 1, st1, buf_ref[base - 2:base - 2 + seq_len, :]))
            act_ref[s * seq_len:(s + 1) * seq_len, :] = gated(cb + cw[0:1] * p2 + cw[1:2] * p1 + cw[2:3] * z0)
            tail = buf_ref[base + seq_len - 2:base + seq_len, :]
            zg_ref[s] = tail[:, :tn]
            zu_ref[s] = tail[:, tn:]


def ffn_up(h, w_up, conv_w, conv_b, state, seq_len):
    m, d = h.shape
    f = w_up.shape[1] // 2
    tm = _tile(m, 1024, SUBLANE)
    tn = _tile(f, 256, LANE)
    nj = f // tn
    if seq_len >= tm:
        assert seq_len % tm == 0
        tiles_per_seq = seq_len // tm
        n_state = 1
        state_map_g = lambda j, i: (i // tiles_per_seq, 0, j)
        state_map_u = lambda j, i: (i // tiles_per_seq, 0, j + nj)
        n_last = m // tm
    else:
        assert tm % seq_len == 0 and seq_len % SUBLANE == 0
        tiles_per_seq = 0
        n_state = tm // seq_len
        state_map_g = lambda j, i: (i, 0, j)
        state_map_u = lambda j, i: (i, 0, j + nj)
        n_last = m // seq_len
    kern = functools.partial(_ffn_up_kernel, seq_len=seq_len, tiles_per_seq=tiles_per_seq,
                             n_sub=FFN_UP_ROW_SPLITS)
    cb = conv_b.reshape(1, 2 * f)
    act, zg, zu = pl.pallas_call(
        kern,
        grid=(nj, m // tm),
        in_specs=[
            pl.BlockSpec((tm, d), lambda j, i: (i, 0)),
            pl.BlockSpec((d, tn), lambda j, i: (0, j)),
            pl.BlockSpec((d, tn), lambda j, i: (0, j + nj)),
            pl.BlockSpec((CONV_WIDTH, tn), lambda j, i: (0, j)),
            pl.BlockSpec((CONV_WIDTH, tn), lambda j, i: (0, j + nj)),
            pl.BlockSpec((1, tn), lambda j, i: (0, j)),
            pl.BlockSpec((1, tn), lambda j, i: (0, j + nj)),
            pl.BlockSpec((n_state, 2, tn), state_map_g),
            pl.BlockSpec((n_state, 2, tn), state_map_u),
        ],
        out_specs=[
            pl.BlockSpec((tm, tn), lambda j, i: (i, j)),
            pl.BlockSpec((n_state, 2, tn), lambda j, i: (i, 0, j)),
            pl.BlockSpec((n_state, 2, tn), lambda j, i: (i, 0, j)),
        ],
        out_shape=[
            jax.ShapeDtypeStruct((m, f), BF16),
            jax.ShapeDtypeStruct((n_last, 2, f), F32),
            jax.ShapeDtypeStruct((n_last, 2, f), F32),
        ],
        scratch_shapes=[pltpu.VMEM((d, 2 * tn), BF16), pltpu.VMEM((tm + SUBLANE, 2 * tn), F32)],
        compiler_params=_params("arbitrary", "arbitrary"),
        name="ffn_up",
    )(h, w_up, w_up, conv_w, conv_w, cb, cb, state, state)
    zlast = jnp.concatenate([zg, zu], axis=-1)
    if tiles_per_seq > 1:
        zlast = zlast[tiles_per_seq - 1::tiles_per_seq]
    return act, zlast


def _ffn_down_kernel(a_ref, b_ref, r_ref, o_ref):
    o_ref[...] = r_ref[...] + jnp.dot(a_ref[...], b_ref[...], preferred_element_type=F32)


def ffn_down(a, b, res):
    m, k = a.shape
    n = b.shape[1]
    tm = _tile(m, 512, SUBLANE)
    tn = _tile(n, 512, LANE)
    return pl.pallas_call(
        _ffn_down_kernel,
        grid=(m // tm, n // tn),
        in_specs=[
            pl.BlockSpec((tm, k), lambda i, j: (i, 0)),
            pl.BlockSpec((k, tn), lambda i, j: (0, j)),
            pl.BlockSpec((tm, tn), lambda i, j: (i, j)),
        ],
        out_specs=pl.BlockSpec((tm, tn), lambda i, j: (i, j)),
        out_shape=jax.ShapeDtypeStruct((m, n), F32),
        compiler_params=_params("parallel", "arbitrary"),
        name="ffn_down",
    )(a, b, res)


def _in_offsets(d):
    da = d // 2
    n_heads = d // 256
    n_kv = n_heads // Q_PER_KV
    sizes = (da, da, n_heads * HEAD_DIM, n_kv * HEAD_DIM, n_kv * HEAD_DIM,
             N_IDX_HEADS * IDX_DIM, IDX_DIM, N_IDX_HEADS, d, d)
    offs = [0]
    for s in sizes:
        offs.append(offs[-1] + s)
    return offs


def _layer(x, pos, seq_len, w, past, conv_state, want_vn):
    bsz, t, d = x.shape
    m = bsz * t
    x2 = x.reshape(m, d)
    n_heads = d // 256
    n_kv = n_heads // Q_PER_KV
    offs = _in_offsets(d)
    w_in = w["in_t"]
    assert offs[-1] == w_in.shape[0] and offs[8] - offs[6] <= LANE

    h = rmsnorm(x2, w["norm_attn_g"], BF16)

    rows = max(t, _tile(m, 1024, SUBLANE))
    rope_h = (HEAD_DIM // 8,) + _rope_tables(pos, HEAD_DIM, HEAD_DIM // 4, rows)
    rope_i = (IDX_DIM // 8,) + _rope_tables(pos, IDX_DIM, IDX_DIM // 4, rows)
    ci, s1i, s2i = rope_i[1:]
    lane = jnp.arange(LANE)
    is_w = (lane >= IDX_DIM) & (lane < IDX_DIM + N_IDX_HEADS)
    w_scale = N_IDX_HEADS ** -0.5 * IDX_DIM ** -0.5
    rope_kw = (IDX_DIM // 8,
               jnp.where(is_w, w_scale, jnp.where(lane < IDX_DIM, ci, 1.0)).astype(F32),
               jnp.where(lane < IDX_DIM, s1i, 0.0), jnp.where(lane < IDX_DIM, s2i, 0.0))

    uv = project(h, w_in, offs[0], offs[2] - offs[0], F32)
    q = project(h, w_in, offs[2], offs[3] - offs[2], BF16, rope_h)
    qi = project(h, w_in, offs[5], offs[6] - offs[5], BF16, rope_i)
    k, v, kw = project_kv(h, w_in, offs[3], offs[4], offs[6], offs[4] - offs[3], rope_h, rope_kw)
    gates = project(h, w["in_gates_t"], 0, 2 * d, F32)

    chunk_rows = min(t, GMLP_CHUNK)
    a_out, vn = gmlp(uv, w["gmlp_norm_g"], w["gmlp_wm"](chunk_rows), w["gmlp_bias"](chunk_rows), want_vn)

    n_valid = t if past is None else past[1].shape[2] + t
    b_out = dsa(qi.reshape(bsz, t, -1), kw.reshape(bsz, t, LANE), q.reshape(bsz, t, -1),
                k.reshape(bsz, t * n_kv, HEAD_DIM), v.reshape(bsz, t * n_kv, HEAD_DIM), past,
                tq=min(DSA_QUERY_BLOCK, t), topk=min(TOPK_MAX, n_valid // 4))

    y = merge(a_out, b_out.reshape(m, n_heads * HEAD_DIM), w["a"], w["b"], gates)
    x2 = matmul_residual(y, w["o"], x2)

    hf = rmsnorm(x2, w["norm_ffn_g"], BF16)
    act, zlast = ffn_up(hf, w["up"], w["conv_w"], w["conv_b"], conv_state, seq_len)
    x2 = ffn_down(act, w["down"], x2)
    kidx = kw[:, :IDX_DIM].reshape(bsz, t, IDX_DIM)
    return (x2.reshape(bsz, t, d), k.reshape(bsz, t, n_kv, HEAD_DIM), v.reshape(bsz, t, n_kv, HEAD_DIM),
            kidx, zlast, vn)


def kernel(x_prompt, x_sample, cache_k, cache_v, cache_kidx, state_ffn_conv, norm_attn_g, w_in, gmlp_norm_g, gmlp_ws, gmlp_b, w_branch_a, w_branch_b, w_out, norm_ffn_g, w_up, conv_w, conv_b, w_down, norm_final_g):
    bsz, s, d = x_prompt.shape
    dbsz, t, _ = x_sample.shape
    depth = w_in.shape[0]
    p_len = cache_k.shape[2]
    da = d // 2
    f2 = w_up.shape[2]
    assert s % GMLP_CHUNK == 0 and GMLP_CHUNK % t == 0 and s % CHUNK == 0

    pos_p = jnp.arange(s, dtype=jnp.int32)
    pos_s = p_len + jnp.arange(t, dtype=jnp.int32)
    ci = jnp.arange(GMLP_CHUNK)
    chunk_mask = (ci[None, :] // CHUNK) <= (ci[:, None] // CHUNK)
    cache_k4 = cache_k.reshape(depth, dbsz, -1, HEAD_DIM)
    cache_v4 = cache_v.reshape(depth, dbsz, -1, HEAD_DIM)
    gate_col0 = _in_offsets(d)[8]

    xp, xs = x_prompt, x_sample
    outs = [[] for _ in range(9)]
    for l in range(depth):
        wm_full = jnp.where(chunk_mask[None], gmlp_ws[l], 0.0)
        bias_rows = jnp.repeat(jnp.transpose(gmlp_b[l]), da // G_A, axis=1)

        def gmlp_wm(rows, wm_full=wm_full):
            reps = GMLP_CHUNK // rows
            blk = wm_full[:, :rows, :rows]
            eye = jnp.eye(reps, dtype=F32)
            return jnp.einsum("ab,gij->gaibj", eye, blk).reshape(G_A, GMLP_CHUNK, GMLP_CHUNK).astype(BF16)

        def gmlp_bias(rows, bias_rows=bias_rows):
            return jnp.tile(bias_rows[:rows], (GMLP_CHUNK // rows, 1))

        w = dict(
            norm_attn_g=norm_attn_g[l], gmlp_norm_g=gmlp_norm_g[l], norm_ffn_g=norm_ffn_g[l],
            gmlp_wm=gmlp_wm, gmlp_bias=gmlp_bias,
            a=w_branch_a[l], b=w_branch_b[l], o=w_out[l], up=w_up[l], conv_w=conv_w[l], conv_b=conv_b[l],
            down=w_down[l].astype(BF16),
        )
        w["in_t"] = jnp.swapaxes(w_in[l], 0, 1).astype(BF16)
        w["in_gates_t"] = jnp.swapaxes(w_in[l], 0, 1)[gate_col0:].astype(BF16)

        xp, kp, vp, kip, cp, _ = _layer(xp, pos_p, s, w, None, jnp.zeros((bsz, CONV_WIDTH - 1, f2), F32),
                                        False)
        past = (l, cache_kidx, cache_k4, cache_v4)
        xs, ks, vs, kis, cs, gv = _layer(xs, pos_s, t, w, past, state_ffn_conv[l], True)
        for lst, val in zip(outs, (kp, vp, kip, cp, ks, vs, kis, cs, gv.reshape(dbsz, t, da))):
            lst.append(val)

    y_prompt = rmsnorm(xp.reshape(bsz * s, d), norm_final_g, F32).reshape(bsz, s, d)
    y_sample = rmsnorm(xs.reshape(dbsz * t, d), norm_final_g, F32).reshape(dbsz, t, d)
    stack = lambda o: o[0][None] if depth == 1 else jnp.stack(o)
    return (y_prompt, y_sample) + tuple(stack(o) for o in outs)
```

```python
import functools
import math

import jax
import jax.numpy as jnp
from jax import lax
from jax.experimental import pallas as pl
from jax.experimental.pallas import tpu as pltpu

CHUNK = 64
GMLP_CHUNK = 128
G_A = 8
HEAD_DIM = 128
Q_PER_KV = 4
N_IDX_HEADS = 16
IDX_DIM = 64
TOPK_MAX = 256
ROPE_THETA = 500000.0
CONV_WIDTH = 3
EPS = 1e-6

LANE = 128
SUBLANE = 8
VMEM_LIMIT_BYTES = 56 * 1024 * 1024

FFN_UP_ROW_SPLITS = 4
DSA_PREFIX_CLASSES = 4
GMLP_CHUNKS_PER_STEP = 4
DSA_QUERY_BLOCK = 128
TOPK_STEP_SURPLUS = 2.0
TOPK_PROBES_PER_CHECK = 3

BF16 = jnp.bfloat16
F32 = jnp.float32
NT_DIMS = (((1,), (1,)), ((), ()))


def _params(*semantics):
    return pltpu.CompilerParams(dimension_semantics=semantics, vmem_limit_bytes=VMEM_LIMIT_BYTES)


def _tile(n, pref, align):
    if n <= pref:
        return n
    t = (pref // align) * align
    while t >= align:
        if n % t == 0:
            return t
        t -= align
    raise ValueError(f"no {align}-aligned tile of {n} below {pref}")


def _first_row_tile():
    return pl.program_id(1) == 0


def _sigmoid(x):
    return 0.5 * jnp.tanh(0.5 * x) + 0.5


def _rmsnorm_kernel(x_ref, g_ref, o_ref):
    x = x_ref[...]
    ms = jnp.mean(x * x, axis=-1, keepdims=True)
    o_ref[...] = (x * lax.rsqrt(ms + EPS) * g_ref[...]).astype(o_ref.dtype)


def rmsnorm(x, g, out_dtype):
    n, d = x.shape
    tr = _tile(n, 512, SUBLANE)
    return pl.pallas_call(
        _rmsnorm_kernel,
        grid=(n // tr,),
        in_specs=[pl.BlockSpec((tr, d), lambda i: (i, 0)), pl.BlockSpec((1, d), lambda i: (0, 0))],
        out_specs=pl.BlockSpec((tr, d), lambda i: (i, 0)),
        out_shape=jax.ShapeDtypeStruct((n, d), out_dtype),
        compiler_params=_params("parallel"),
        name="rmsnorm",
    )(x, g.reshape(1, d))


def _rope_slab(x, c, s1, s2, shift):
    return x * c + pltpu.roll(x, LANE - shift, 1) * s1 + pltpu.roll(x, shift, 1) * s2


def _proj_kernel(a_ref, b_ref, *rest, rope_shift):
    acc = lax.dot_general(a_ref[...], b_ref[...], NT_DIMS, preferred_element_type=F32)
    if rope_shift:
        c_ref, s1_ref, s2_ref, o_ref = rest
        c, s1, s2 = c_ref[...], s1_ref[...], s2_ref[...]
        for h in range(acc.shape[1] // LANE):
            sl = slice(h * LANE, (h + 1) * LANE)
            o_ref[:, sl] = _rope_slab(acc[:, sl], c, s1, s2, rope_shift).astype(o_ref.dtype)
    else:
        (o_ref,) = rest
        o_ref[...] = acc.astype(o_ref.dtype)


def project(a, wt, col0, ncols, out_dtype, rope=None):
    m, k = a.shape
    tm = _tile(m, 1024, SUBLANE)
    tn = _tile(math.gcd(ncols, col0), 1024, LANE)
    assert col0 % tn == 0 and ncols % tn == 0
    j0 = col0 // tn
    in_specs = [pl.BlockSpec((tm, k), lambda j, i: (i, 0)), pl.BlockSpec((tn, k), lambda j, i: (j + j0, 0))]
    args = [a, wt]
    shift = 0
    if rope is not None:
        shift, tabs = rope[0], rope[1:]
        reps = tabs[0].shape[0] // tm
        assert reps * tm == tabs[0].shape[0]
        for t in tabs:
            in_specs.append(pl.BlockSpec((tm, LANE), lambda j, i: (i % reps, 0)))
            args.append(t)
    return pl.pallas_call(
        functools.partial(_proj_kernel, rope_shift=shift),
        grid=(ncols // tn, m // tm),
        in_specs=in_specs,
        out_specs=pl.BlockSpec((tm, tn), lambda j, i: (i, j)),
        out_shape=jax.ShapeDtypeStruct((m, ncols), out_dtype),
        compiler_params=_params("parallel", "arbitrary"),
        name="project",
    )(*args)


def _proj_kv_kernel(a_ref, b_ref, ck_ref, s1k_ref, s2k_ref, cw_ref, s1w_ref, s2w_ref,
                    k_ref, v_ref, kw_ref, *, shift_k, shift_w):
    g = pl.program_id(1)
    tm = a_ref.shape[0]
    n_heads = b_ref.shape[0] // LANE

    def matmul(rows):
        return lax.dot_general(a_ref[...], b_ref[0:rows, :], NT_DIMS, preferred_element_type=F32)

    @pl.when(g == 0)
    def _():
        acc = matmul(n_heads * LANE)
        c, s1, s2 = ck_ref[...], s1k_ref[...], s2k_ref[...]
        for h in range(n_heads):
            k_ref[pl.ds(h, tm, stride=n_heads), :] = _rope_slab(acc[:, h * LANE:(h + 1) * LANE], c, s1, s2, shift_k)

    @pl.when(g == 1)
    def _():
        acc = matmul(n_heads * LANE)
        for h in range(n_heads):
            v_ref[pl.ds(h, tm, stride=n_heads), :] = acc[:, h * LANE:(h + 1) * LANE]

    @pl.when(g == 2)
    def _():
        kw_ref[...] = _rope_slab(matmul(LANE), cw_ref[...], s1w_ref[...], s2w_ref[...], shift_w)


def project_kv(a, wt, col_k, col_v, col_kw, width, rope_k, rope_kw):
    m, kdim = a.shape
    tm = _tile(m, 1024, SUBLANE)
    assert col_k % width == 0 and col_v % width == 0 and col_kw % width == 0
    assert col_kw + width <= wt.shape[0] and width % LANE == 0
    blocks = (col_k // width, col_v // width, col_kw // width)
    heads = width // LANE

    def w_map(i, g):
        return (jnp.where(g == 0, blocks[0], jnp.where(g == 1, blocks[1], blocks[2])), 0)

    tabs = rope_k[1:] + rope_kw[1:]
    reps = tabs[0].shape[0] // tm
    assert reps * tm == tabs[0].shape[0]
    return pl.pallas_call(
        functools.partial(_proj_kv_kernel, shift_k=rope_k[0], shift_w=rope_kw[0]),
        grid=(m // tm, 3),
        in_specs=[pl.BlockSpec((tm, kdim), lambda i, g: (i, 0)), pl.BlockSpec((width, kdim), w_map)]
        + [pl.BlockSpec((tm, LANE), lambda i, g: (i % reps, 0)) for _ in tabs],
        out_specs=[
            pl.BlockSpec((tm * heads, LANE), lambda i, g: (i, 0)),
            pl.BlockSpec((tm * heads, LANE), lambda i, g: (i, 0)),
            pl.BlockSpec((tm, LANE), lambda i, g: (i, 0)),
        ],
        out_shape=[
            jax.ShapeDtypeStruct((m * heads, LANE), F32),
            jax.ShapeDtypeStruct((m * heads, LANE), F32),
            jax.ShapeDtypeStruct((m, LANE), F32),
        ],
        compiler_params=_params("parallel", "arbitrary"),
        name="project_kv",
    )(a, wt, *tabs)


def _rope_tables(pos, head_dim, rot_dim, rows):
    half = rot_dim // 2
    inv_freq = ROPE_THETA ** (-jnp.arange(half, dtype=F32) / half)
    ang = pos.astype(F32)[:, None] * inv_freq[None, :]
    cos, sin = jnp.cos(ang), jnp.sin(ang)
    t = pos.shape[0]
    zeros_h = jnp.zeros((t, half), F32)
    rest0 = jnp.zeros((t, head_dim - rot_dim), F32)
    c = jnp.concatenate([cos, cos, jnp.ones((t, head_dim - rot_dim), F32)], axis=1)
    s1 = jnp.concatenate([-sin, zeros_h, rest0], axis=1)
    s2 = jnp.concatenate([zeros_h, sin, rest0], axis=1)
    reps_l = LANE // head_dim
    reps_r = rows // t
    return tuple(jnp.tile(x, (reps_r, reps_l)) for x in (c, s1, s2))


def _gmlp_kernel(u_ref, v_ref, g_ref, wm_ref, b_ref, a_ref, *vn_out, groups):
    gw = v_ref.shape[1] // groups
    for c in range(v_ref.shape[0] // GMLP_CHUNK):
        rows = slice(c * GMLP_CHUNK, (c + 1) * GMLP_CHUNK)
        v = v_ref[rows, :]
        vn = v * lax.rsqrt(jnp.mean(v * v, axis=-1, keepdims=True) + EPS) * g_ref[...]
        if vn_out:
            vn_out[0][rows, :] = vn
        vb = vn.astype(BF16)
        for g in range(groups):
            sl = slice(g * gw, (g + 1) * gw)
            s = jnp.dot(wm_ref[g], vb[:, sl], preferred_element_type=F32) + b_ref[:, sl]
            a_ref[rows, sl] = (u_ref[rows, sl] * s).astype(a_ref.dtype)


def gmlp(uv, g_norm, wm, bias, want_vn):
    m, d2 = uv.shape
    da = d2 // 2
    tc = GMLP_CHUNK * math.gcd(m // GMLP_CHUNK, GMLP_CHUNKS_PER_STEP)
    out_shape = [jax.ShapeDtypeStruct((m, da), BF16)]
    out_specs = [pl.BlockSpec((tc, da), lambda i: (i, 0))]
    if want_vn:
        out_shape.append(jax.ShapeDtypeStruct((m, da), F32))
        out_specs.append(pl.BlockSpec((tc, da), lambda i: (i, 0)))
    res = pl.pallas_call(
        functools.partial(_gmlp_kernel, groups=wm.shape[0]),
        grid=(m // tc,),
        in_specs=[
            pl.BlockSpec((tc, da), lambda i: (i, 0)),
            pl.BlockSpec((tc, da), lambda i: (i, 1)),
            pl.BlockSpec((1, da), lambda i: (0, 0)),
            pl.BlockSpec(wm.shape, lambda i: (0, 0, 0)),
            pl.BlockSpec((GMLP_CHUNK, da), lambda i: (0, 0)),
        ],
        out_specs=out_specs,
        out_shape=out_shape,
        compiler_params=_params("parallel"),
        name="gmlp",
    )(uv, uv, g_norm.reshape(1, da), wm, bias)
    return res if want_vn else (res[0], None)


LOG2_E = 1.4426950408889634
ROW_REDUCE_GROUP = 64
ATTN_ROW_GROUP = 32


def _reduce_rows(x, op):
    pair = {jnp.sum: jnp.add, jnp.min: jnp.minimum, jnp.max: jnp.maximum}[op]
    rows = x.shape[0]
    if rows % ROW_REDUCE_GROUP == 0:
        parts = [x[i:i + ROW_REDUCE_GROUP] for i in range(0, rows, ROW_REDUCE_GROUP)]
        while len(parts) > 1:
            parts = [pair(parts[i], parts[i + 1]) if i + 1 < len(parts) else parts[i]
                     for i in range(0, len(parts), 2)]
        x = parts[0]
    return op(x, axis=0, keepdims=True)

def _dsa_select_attend(qi_ref, kwq_ref, q_ref, klo_ref, khi_ref, ks_ref, vt_ref, o_ref, bias_ref,
                       lg_ref, p_ref, *,
                       s_c, tq, n_kv, topk, row0, n_valid, scale):
    n_tiles = s_c // LANE
    reps = LANE // tq

    def rep_rows(x):
        return x if reps == 1 else jnp.concatenate([x] * reps, axis=0)

    w_t = jnp.transpose(rep_rows(kwq_ref[...]))
    klo = klo_ref[0:s_c, :]
    khi = khi_ref[0:s_c, :]

    score = jnp.zeros((s_c, LANE), F32)
    for pp in range(N_IDX_HEADS // 4):
        qp2 = jnp.concatenate([rep_rows(qi_ref[:, (2 * pp + i) * LANE:(2 * pp + i + 1) * LANE])
                               for i in range(2)], axis=0)
        rel_lo = lax.dot_general(klo, qp2, NT_DIMS, preferred_element_type=F32)
        rel_hi = lax.dot_general(khi, qp2, NT_DIMS, preferred_element_type=F32)
        for i in range(2):
            r0 = IDX_DIM + 2 * (2 * pp + i)
            score = (score + jnp.maximum(rel_lo[:, i * LANE:(i + 1) * LANE], 0.0) * w_t[r0:r0 + 1, :]
                     + jnp.maximum(rel_hi[:, i * LANE:(i + 1) * LANE], 0.0) * w_t[r0 + 1:r0 + 2, :])

    kpos = lax.broadcasted_iota(jnp.int32, (s_c, LANE), 0)
    if row0 is not None:
        qrow = row0 + jnp.bitwise_and(lax.broadcasted_iota(jnp.int32, (s_c, LANE), 1), tq - 1)
        adm = kpos < (jnp.right_shift(qrow, CHUNK.bit_length() - 1) + 1) * CHUNK
    else:
        adm = kpos < n_valid
    score = jnp.where(adm, score, -jnp.inf)
    bias_ref[0:s_c, :] = score

    kf = float(topk)
    lo0 = _reduce_rows(jnp.where(adm, score, jnp.inf), jnp.min)
    hi0 = _reduce_rows(score, jnp.max)
    c_lo0 = _reduce_rows(jnp.where(adm, 1.0, 0.0), jnp.sum)
    c_hi0 = jnp.zeros((1, LANE), F32)

    def probe(t):
        cnt = jnp.zeros((ROW_REDUCE_GROUP, LANE), F32)
        nxt = jnp.full((ROW_REDUCE_GROUP, LANE), jnp.inf, F32)
        for i in range(0, s_c, ROW_REDUCE_GROUP):
            sc = bias_ref[i:i + ROW_REDUCE_GROUP, :]
            above = sc > t
            cnt = cnt + jnp.where(above, 1.0, 0.0)
            nxt = jnp.minimum(nxt, jnp.where(above, sc, jnp.inf))
        return jnp.sum(cnt, axis=0, keepdims=True), jnp.min(nxt, axis=0, keepdims=True)

    def active_of(lo, hi, c_lo):
        return (c_lo > kf) & (hi > lo)

    def cond(carry):
        lo, hi, c_lo, _ = carry
        return jnp.max(jnp.where(active_of(lo, hi, c_lo), 1.0, 0.0)) > 0.0

    def body(carry):
        for _ in range(TOPK_PROBES_PER_CHECK):
            carry = advance(carry)
        return carry

    def advance(carry):
        lo, hi, c_lo, c_hi = carry
        act = active_of(lo, hi, c_lo)
        mid = 0.5 * lo + 0.5 * hi
        step = (c_lo - kf <= TOPK_STEP_SURPLUS) | (mid <= lo) | (mid >= hi)
        t = jnp.where(step, lo, mid)
        cnt, nxt = probe(t)
        take = act & (cnt >= kf)
        drop = act & (cnt < kf)
        return (jnp.where(take, nxt, lo), jnp.where(drop, t, hi),
                jnp.where(take, cnt, c_lo), jnp.where(drop, cnt, c_hi))

    lo, hi, c_lo, c_hi = lax.while_loop(cond, body, (lo0, hi0, c_lo0, c_hi0))
    tie = c_lo > kf
    any_tie = jnp.max(jnp.where(tie, 1.0, 0.0)) > 0.0

    @pl.when(jnp.logical_not(any_tie))
    def _():
        for jt in range(n_tiles):
            sl = slice(jt * LANE, (jt + 1) * LANE)
            bias_ref[sl, :] = jnp.where(bias_ref[sl, :] >= lo, 0.0, -jnp.inf)

    @pl.when(any_tie)
    def _():
        quota = kf - c_hi
        tri = (lax.broadcasted_iota(jnp.int32, (LANE, LANE), 1)
               <= lax.broadcasted_iota(jnp.int32, (LANE, LANE), 0)).astype(F32).astype(BF16)
        before = jnp.zeros((1, LANE), F32)
        for jt in range(n_tiles):
            sl = slice(jt * LANE, (jt + 1) * LANE)
            sc = bias_ref[sl, :]
            cand = jnp.where((sc >= lo) & (sc <= hi), 1.0, 0.0)
            rank = jnp.dot(tri, cand.astype(BF16), preferred_element_type=F32) + before
            keep = (sc > hi) | ((cand > 0.0) & (rank <= quota))
            bias_ref[sl, :] = jnp.where(tie, jnp.where(keep, 0.0, -jnp.inf),
                                        jnp.where(sc >= lo, 0.0, -jnp.inf))
            before = before + jnp.sum(cand, axis=0, keepdims=True)

    width = Q_PER_KV * tq
    n_slabs = width // LANE
    grp = ATTN_ROW_GROUP
    def put_logits(n):
        qn = jnp.concatenate(
            [q_ref[:, (n * Q_PER_KV + g) * HEAD_DIM:(n * Q_PER_KV + g + 1) * HEAD_DIM]
             for g in range(Q_PER_KV)], axis=0)
        lg_ref[n % 2, 0:s_c, :] = lax.dot_general(
            ks_ref[0:s_c, n * HEAD_DIM:(n + 1) * HEAD_DIM], qn, NT_DIMS,
            preferred_element_type=F32) * (scale * LOG2_E)

    put_logits(0)
    for n in range(n_kv):
        if n + 1 < n_kv:
            put_logits(n + 1)
        ksl = slice(n * HEAD_DIM, (n + 1) * HEAD_DIM)
        lgn = lg_ref.at[n % 2]
        pn = p_ref.at[n % 2]
        mpart = jnp.full((grp, width), -jnp.inf, F32)
        for i in range(0, s_c, grp):
            b = bias_ref[i:i + grp, :]
            lg = lgn[i:i + grp, :] + (b if n_slabs == 1 else jnp.concatenate([b] * n_slabs, axis=1))
            lgn[i:i + grp, :] = lg
            mpart = jnp.maximum(mpart, lg)
        mx = jnp.max(mpart, axis=0, keepdims=True)
        dpart = jnp.zeros((grp, width), F32)
        for i in range(0, s_c, grp):
            p = jnp.exp2(lgn[i:i + grp, :] - mx)
            dpart = dpart + p
            pn[i:i + grp, :] = p.astype(BF16)
        den = jnp.sum(dpart, axis=0, keepdims=True)
        o_t = jnp.dot(vt_ref[ksl, 0:s_c], pn[0:s_c, :], preferred_element_type=F32) / den
        for sb in range(n_slabs):
            o = jnp.transpose(o_t[:, sb * LANE:(sb + 1) * LANE])
            for r in range(reps):
                g = sb * reps + r
                hsl = slice((n * Q_PER_KV + g) * HEAD_DIM, (n * Q_PER_KV + g + 1) * HEAD_DIM)
                o_ref[:, hsl] = o[r * tq:(r + 1) * tq].astype(o_ref.dtype)


def _dsa_kernel(qi_ref, kwq_ref, q_ref, kw_ref, k_ref, v_ref, *rest, tq, t_new, p_len, n_classes, topk):
    if p_len:
        (pki_ref, pk_ref, pv_ref, o_ref, klo_ref, khi_ref, ks_ref, vt_ref, bias_ref, lg_ref, p_ref,
         tmp_ref) = rest
    else:
        o_ref, klo_ref, khi_ref, ks_ref, vt_ref, bias_ref, lg_ref, p_ref = rest
        pk_ref = pv_ref = None
    s_pad = ks_ref.shape[0]
    n_valid = p_len + t_new
    n_kv = ks_ref.shape[1] // HEAD_DIM
    j = pl.program_id(1)

    def head_rows(src_ref, rows, n):
        return src_ref[pl.ds(n, rows, stride=n_kv), :]

    @pl.when(j == 0)
    def _():
        lane = lax.broadcasted_iota(jnp.int32, (t_new, LANE), 1)
        new_lo = jnp.where(lane < IDX_DIM, kw_ref[...], 0.0)
        if p_len:
            tmp_ref[...] = jnp.zeros(tmp_ref.shape, F32)
            tmp_ref[0:p_len, 0:IDX_DIM] = pki_ref[...]
            tmp_ref[p_len:n_valid, :] = new_lo
            lo = tmp_ref[...]
        else:
            lo = new_lo
        klo_ref[...] = lo.astype(BF16)
        khi_ref[...] = pltpu.roll(lo, IDX_DIM, 1).astype(BF16)
        tail = jnp.zeros((s_pad - n_valid, HEAD_DIM), F32)
        for n in range(n_kv):
            csl = slice(n * HEAD_DIM, (n + 1) * HEAD_DIM)
            new_k = head_rows(k_ref, t_new, n)
            new_v = head_rows(v_ref, t_new, n)
            if s_pad > n_valid:
                new_k = jnp.concatenate([new_k, tail], axis=0)
                new_v = jnp.concatenate([new_v, tail], axis=0)
            if p_len:
                ks_ref[0:p_len, csl] = head_rows(pk_ref, p_len, n).astype(BF16)
                vt_ref[csl, 0:p_len] = jnp.transpose(head_rows(pv_ref, p_len, n)).astype(BF16)
            ks_ref[p_len:s_pad, csl] = new_k.astype(BF16)
            vt_ref[csl, p_len:s_pad] = jnp.transpose(new_v).astype(BF16)

    common = dict(tq=tq, n_kv=n_kv, topk=topk, n_valid=n_valid, scale=HEAD_DIM ** -0.5)
    refs = (qi_ref, kwq_ref, q_ref, klo_ref, khi_ref, ks_ref, vt_ref, o_ref, bias_ref, lg_ref, p_ref)
    if p_len:
        _dsa_select_attend(*refs, s_c=s_pad, row0=None, **common)
    else:
        per_class = pl.num_programs(1) // n_classes
        for c in range(n_classes):
            @pl.when(j // per_class == c)
            def _(c=c):
                _dsa_select_attend(*refs, s_c=(c + 1) * (s_pad // n_classes), row0=j * tq, **common)


def dsa(qi, kw, q, k, v, past, *, tq, topk):
    b, t, dq = q.shape
    n_kv = k.shape[1] // t
    dkv = n_kv * HEAD_DIM
    nq = t // tq
    row = lambda i, j: (i, j, 0)
    full = lambda i, j: (i, 0, 0)
    in_specs = [
        pl.BlockSpec((None, tq, qi.shape[2]), row),
        pl.BlockSpec((None, tq, LANE), row),
        pl.BlockSpec((None, tq, dq), row),
        pl.BlockSpec((None, t, LANE), full),
        pl.BlockSpec((None, t * n_kv, HEAD_DIM), full),
        pl.BlockSpec((None, t * n_kv, HEAD_DIM), full),
    ]
    args = [qi, kw, q, kw, k, v]
    if past is None:
        p_len = 0
        s_pad = t
        n_classes = DSA_PREFIX_CLASSES if (nq % DSA_PREFIX_CLASSES == 0
                                           and t % (DSA_PREFIX_CLASSES * LANE) == 0) else 1
        assert tq % CHUNK == 0 and LANE % tq == 0
    else:
        layer, pki, pk, pv = past
        p_len = pki.shape[2]
        s_pad = -(-(p_len + t) // LANE) * LANE
        n_classes = 1
        assert nq == 1 and p_len % LANE == 0 and LANE % tq == 0 and Q_PER_KV * tq % LANE == 0
        cache = lambda i, j: (layer, i, 0, 0)
        in_specs += [pl.BlockSpec((None, None, p_len, IDX_DIM), cache),
                     pl.BlockSpec((None, None, p_len * n_kv, HEAD_DIM), cache),
                     pl.BlockSpec((None, None, p_len * n_kv, HEAD_DIM), cache)]
        args += [pki, pk, pv]
    scratch = [pltpu.VMEM((s_pad, LANE), BF16), pltpu.VMEM((s_pad, LANE), BF16),
               pltpu.VMEM((s_pad, dkv), BF16), pltpu.VMEM((dkv, s_pad), BF16),
               pltpu.VMEM((s_pad, LANE), F32),
               pltpu.VMEM((2, s_pad, Q_PER_KV * tq), F32), pltpu.VMEM((2, s_pad, Q_PER_KV * tq), BF16)]
    if past is not None:
        scratch.append(pltpu.VMEM((s_pad, LANE), F32))
    kern = functools.partial(_dsa_kernel, tq=tq, t_new=t, p_len=p_len, n_classes=n_classes, topk=topk)
    return pl.pallas_call(
        kern,
        grid=(b, nq),
        in_specs=in_specs,
        out_specs=pl.BlockSpec((None, tq, dq), row),
        out_shape=jax.ShapeDtypeStruct((b, t, dq), BF16),
        scratch_shapes=scratch,
        compiler_params=_params("arbitrary", "arbitrary"),
        name="dsa",
    )(*args)


def _merge_kernel(a_ref, b_ref, wa_ref, wb_ref, ga_ref, gb_ref, o_ref, waq_ref, wbq_ref):
    @pl.when(_first_row_tile())
    def _():
        waq_ref[...] = wa_ref[...].astype(BF16)
        wbq_ref[...] = wb_ref[...].astype(BF16)

    ya = jnp.dot(a_ref[...], waq_ref[...], preferred_element_type=F32)
    yb = jnp.dot(b_ref[...], wbq_ref[...], preferred_element_type=F32)
    y = _sigmoid(ga_ref[...]) * ya + _sigmoid(gb_ref[...]) * yb
    o_ref[...] = y.astype(o_ref.dtype)


def merge(a, b, wa, wb, gates):
    m, ka = a.shape
    kb = b.shape[1]
    d = wa.shape[1]
    tm = _tile(m, 1024, SUBLANE)
    tn = _tile(d, 512, LANE)
    nj = d // tn
    return pl.pallas_call(
        _merge_kernel,
        grid=(nj, m // tm),
        in_specs=[
            pl.BlockSpec((tm, ka), lambda j, i: (i, 0)),
            pl.BlockSpec((tm, kb), lambda j, i: (i, 0)),
            pl.BlockSpec((ka, tn), lambda j, i: (0, j)),
            pl.BlockSpec((kb, tn), lambda j, i: (0, j)),
            pl.BlockSpec((tm, tn), lambda j, i: (i, j)),
            pl.BlockSpec((tm, tn), lambda j, i: (i, j + nj)),
        ],
        out_specs=pl.BlockSpec((tm, tn), lambda j, i: (i, j)),
        out_shape=jax.ShapeDtypeStruct((m, d), BF16),
        scratch_shapes=[pltpu.VMEM((ka, tn), BF16), pltpu.VMEM((kb, tn), BF16)],
        compiler_params=_params("arbitrary", "arbitrary"),
        name="merge",
    )(a, b, wa, wb, gates, gates)


def _matmul_res_kernel(a_ref, b_ref, r_ref, o_ref, bq_ref):
    @pl.when(_first_row_tile())
    def _():
        bq_ref[...] = b_ref[...].astype(BF16)

    o_ref[...] = r_ref[...] + jnp.dot(a_ref[...], bq_ref[...], preferred_element_type=F32)


def matmul_residual(a, b, res):
    m, k = a.shape
    n = b.shape[1]
    tm = _tile(m, 1024, SUBLANE)
    tn = _tile(n, 512, LANE)
    return pl.pallas_call(
        _matmul_res_kernel,
        grid=(n // tn, m // tm),
        in_specs=[
            pl.BlockSpec((tm, k), lambda j, i: (i, 0)),
            pl.BlockSpec((k, tn), lambda j, i: (0, j)),
            pl.BlockSpec((tm, tn), lambda j, i: (i, j)),
        ],
        out_specs=pl.BlockSpec((tm, tn), lambda j, i: (i, j)),
        out_shape=jax.ShapeDtypeStruct((m, n), F32),
        scratch_shapes=[pltpu.VMEM((k, tn), BF16)],
        compiler_params=_params("arbitrary", "arbitrary"),
        name="matmul_residual",
    )(a, b, res)


def _ffn_up_kernel(h_ref, wg_ref, wu_ref, cwg_ref, cwu_ref, cbg_ref, cbu_ref, sg_ref, su_ref,
                   act_ref, zg_ref, zu_ref, wq_ref, buf_ref, *, seq_len, tiles_per_seq, n_sub):
    tm = h_ref.shape[0]
    tn = act_ref.shape[1]
    pad = SUBLANE

    @pl.when(_first_row_tile())
    def _():
        wq_ref[:, 0:tn] = wg_ref[...].astype(BF16)
        wq_ref[:, tn:2 * tn] = wu_ref[...].astype(BF16)

    cw = jnp.concatenate([cwg_ref[...], cwu_ref[...]], axis=1)
    cb = jnp.concatenate([cbg_ref[...], cbu_ref[...]], axis=1)

    def gated(c):
        gate, up = c[:, :tn], c[:, tn:]
        return (gate * _sigmoid(gate) * up).astype(act_ref.dtype)

    if tiles_per_seq >= 1:
        @pl.when(pl.program_id(1) % tiles_per_seq == 0)
        def _():
            buf_ref[pad - 2:pad, 0:tn] = sg_ref[0]
            buf_ref[pad - 2:pad, tn:2 * tn] = su_ref[0]
        rs = tm // n_sub
        zs = [jnp.dot(h_ref[r * rs:(r + 1) * rs, :], wq_ref[...], preferred_element_type=F32)
              for r in range(n_sub)]
        for r, z in enumerate(zs):
            base = pad + r * rs
            buf_ref[base:base + rs, :] = z
            c = (cb + cw[0:1] * buf_ref[base - 2:base - 2 + rs, :]
                 + cw[1:2] * buf_ref[base - 1:base - 1 + rs, :] + cw[2:3] * z)
            act_ref[r * rs:(r + 1) * rs, :] = gated(c)
        tail = buf_ref[pad + tm - 2:pad + tm, :]
        zg_ref[0] = tail[:, :tn]
        zu_ref[0] = tail[:, tn:]
        buf_ref[pad - 2:pad, :] = tail
    else:
        buf_ref[0:pad, :] = jnp.zeros((pad, 2 * tn), F32)
        buf_ref[pad:pad + tm, :] = jnp.dot(h_ref[...], wq_ref[...], preferred_element_type=F32)
        rowi = lax.broadcasted_iota(jnp.int32, (seq_len, 2 * tn), 0)
        for s in range(tm // seq_len):
            base = pad + s * seq_len
            st0 = jnp.concatenate([sg_ref[s, 0:1, :], su_ref[s, 0:1, :]], axis=1)
            st1 = jnp.concatenate([sg_ref[s, 1:2, :], su_ref[s, 1:2, :]], axis=1)
            z0 = buf_ref[base:base + seq_len, :]
            p1 = jnp.where(rowi == 0, st1, buf_ref[base - 1:base - 1 + seq_len, :])
            p2 = jnp.where(rowi == 0, st0, jnp.where(rowi == 1, st1, buf_ref[base - 2:base - 2 + seq_len, :]))
            act_ref[s * seq_len:(s + 1) * seq_len, :] = gated(cb + cw[0:1] * p2 + cw[1:2] * p1 + cw[2:3] * z0)
            tail = buf_ref[base + seq_len - 2:base + seq_len, :]
            zg_ref[s] = tail[:, :tn]
            zu_ref[s] = tail[:, tn:]


def ffn_up(h, w_up, conv_w, conv_b, state, seq_len):
    m, d = h.shape
    f = w_up.shape[1] // 2
    tm = _tile(m, 1024, SUBLANE)
    tn = _tile(f, 256, LANE)
    nj = f // tn
    if seq_len >= tm:
        assert seq_len % tm == 0
        tiles_per_seq = seq_len // tm
        n_state = 1
        state_map_g = lambda j, i: (i // tiles_per_seq, 0, j)
        state_map_u = lambda j, i: (i // tiles_per_seq, 0, j + nj)
        n_last = m // tm
    else:
        assert tm % seq_len == 0 and seq_len % SUBLANE == 0
        tiles_per_seq = 0
        n_state = tm // seq_len
        state_map_g = lambda j, i: (i, 0, j)
        state_map_u = lambda j, i: (i, 0, j + nj)
        n_last = m // seq_len
    kern = functools.partial(_ffn_up_kernel, seq_len=seq_len, tiles_per_seq=tiles_per_seq,
                             n_sub=FFN_UP_ROW_SPLITS)
    cb = conv_b.reshape(1, 2 * f)
    act, zg, zu = pl.pallas_call(
        kern,
        grid=(nj, m // tm),
        in_specs=[
            pl.BlockSpec((tm, d), lambda j, i: (i, 0)),
            pl.BlockSpec((d, tn), lambda j, i: (0, j)),
            pl.BlockSpec((d, tn), lambda j, i: (0, j + nj)),
            pl.BlockSpec((CONV_WIDTH, tn), lambda j, i: (0, j)),
            pl.BlockSpec((CONV_WIDTH, tn), lambda j, i: (0, j + nj)),
            pl.BlockSpec((1, tn), lambda j, i: (0, j)),
            pl.BlockSpec((1, tn), lambda j, i: (0, j + nj)),
            pl.BlockSpec((n_state, 2, tn), state_map_g),
            pl.BlockSpec((n_state, 2, tn), state_map_u),
        ],
        out_specs=[
            pl.BlockSpec((tm, tn), lambda j, i: (i, j)),
            pl.BlockSpec((n_state, 2, tn), lambda j, i: (i, 0, j)),
            pl.BlockSpec((n_state, 2, tn), lambda j, i: (i, 0, j)),
        ],
        out_shape=[
            jax.ShapeDtypeStruct((m, f), BF16),
            jax.ShapeDtypeStruct((n_last, 2, f), F32),
            jax.ShapeDtypeStruct((n_last, 2, f), F32),
        ],
        scratch_shapes=[pltpu.VMEM((d, 2 * tn), BF16), pltpu.VMEM((tm + SUBLANE, 2 * tn), F32)],
        compiler_params=_params("arbitrary", "arbitrary"),
        name="ffn_up",
    )(h, w_up, w_up, conv_w, conv_w, cb, cb, state, state)
    zlast = jnp.concatenate([zg, zu], axis=-1)
    if tiles_per_seq > 1:
        zlast = zlast[tiles_per_seq - 1::tiles_per_seq]
    return act, zlast


def _ffn_down_kernel(a_ref, b_ref, r_ref, o_ref):
    o_ref[...] = r_ref[...] + jnp.dot(a_ref[...], b_ref[...], preferred_element_type=F32)


def ffn_down(a, b, res):
    m, k = a.shape
    n = b.shape[1]
    tm = _tile(m, 512, SUBLANE)
    tn = _tile(n, 512, LANE)
    return pl.pallas_call(
        _ffn_down_kernel,
        grid=(m // tm, n // tn),
        in_specs=[
            pl.BlockSpec((tm, k), lambda i, j: (i, 0)),
            pl.BlockSpec((k, tn), lambda i, j: (0, j)),
            pl.BlockSpec((tm, tn), lambda i, j: (i, j)),
        ],
        out_specs=pl.BlockSpec((tm, tn), lambda i, j: (i, j)),
        out_shape=jax.ShapeDtypeStruct((m, n), F32),
        compiler_params=_params("parallel", "arbitrary"),
        name="ffn_down",
    )(a, b, res)


def _in_offsets(d):
    da = d // 2
    n_heads = d // 256
    n_kv = n_heads // Q_PER_KV
    sizes = (da, da, n_heads * HEAD_DIM, n_kv * HEAD_DIM, n_kv * HEAD_DIM,
             N_IDX_HEADS * IDX_DIM, IDX_DIM, N_IDX_HEADS, d, d)
    offs = [0]
    for s in sizes:
        offs.append(offs[-1] + s)
    return offs


def _layer(x, pos, seq_len, w, past, conv_state, want_vn):
    bsz, t, d = x.shape
    m = bsz * t
    x2 = x.reshape(m, d)
    n_heads = d // 256
    n_kv = n_heads // Q_PER_KV
    offs = _in_offsets(d)
    w_in = w["in_t"]
    assert offs[-1] == w_in.shape[0] and offs[8] - offs[6] <= LANE

    h = rmsnorm(x2, w["norm_attn_g"], BF16)

    rows = max(t, _tile(m, 1024, SUBLANE))
    rope_h = (HEAD_DIM // 8,) + _rope_tables(pos, HEAD_DIM, HEAD_DIM // 4, rows)
    rope_i = (IDX_DIM // 8,) + _rope_tables(pos, IDX_DIM, IDX_DIM // 4, rows)
    ci, s1i, s2i = rope_i[1:]
    lane = jnp.arange(LANE)
    is_w = (lane >= IDX_DIM) & (lane < IDX_DIM + N_IDX_HEADS)
    w_scale = N_IDX_HEADS ** -0.5 * IDX_DIM ** -0.5
    rope_kw = (IDX_DIM // 8,
               jnp.where(is_w, w_scale, jnp.where(lane < IDX_DIM, ci, 1.0)).astype(F32),
               jnp.where(lane < IDX_DIM, s1i, 0.0), jnp.where(lane < IDX_DIM, s2i, 0.0))

    uv = project(h, w_in, offs[0], offs[2] - offs[0], F32)
    q = project(h, w_in, offs[2], offs[3] - offs[2], BF16, rope_h)
    qi = project(h, w_in, offs[5], offs[6] - offs[5], BF16, rope_i)
    k, v, kw = project_kv(h, w_in, offs[3], offs[4], offs[6], offs[4] - offs[3], rope_h, rope_kw)
    gates = project(h, w["in_gates_t"], 0, 2 * d, F32)

    chunk_rows = min(t, GMLP_CHUNK)
    a_out, vn = gmlp(uv, w["gmlp_norm_g"], w["gmlp_wm"](chunk_rows), w["gmlp_bias"](chunk_rows), want_vn)

    n_valid = t if past is None else past[1].shape[2] + t
    b_out = dsa(qi.reshape(bsz, t, -1), kw.reshape(bsz, t, LANE), q.reshape(bsz, t, -1),
                k.reshape(bsz, t * n_kv, HEAD_DIM), v.reshape(bsz, t * n_kv, HEAD_DIM), past,
                tq=min(DSA_QUERY_BLOCK, t), topk=min(TOPK_MAX, n_valid // 4))

    y = merge(a_out, b_out.reshape(m, n_heads * HEAD_DIM), w["a"], w["b"], gates)
    x2 = matmul_residual(y, w["o"], x2)

    hf = rmsnorm(x2, w["norm_ffn_g"], BF16)
    act, zlast = ffn_up(hf, w["up"], w["conv_w"], w["conv_b"], conv_state, seq_len)
    x2 = ffn_down(act, w["down"], x2)
    kidx = kw[:, :IDX_DIM].reshape(bsz, t, IDX_DIM)
    return (x2.reshape(bsz, t, d), k.reshape(bsz, t, n_kv, HEAD_DIM), v.reshape(bsz, t, n_kv, HEAD_DIM),
            kidx, zlast, vn)


def kernel(x_prompt, x_sample, cache_k, cache_v, cache_kidx, state_ffn_conv, norm_attn_g, w_in, gmlp_norm_g, gmlp_ws, gmlp_b, w_branch_a, w_branch_b, w_out, norm_ffn_g, w_up, conv_w, conv_b, w_down, norm_final_g):
    bsz, s, d = x_prompt.shape
    dbsz, t, _ = x_sample.shape
    depth = w_in.shape[0]
    p_len = cache_k.shape[2]
    da = d // 2
    f2 = w_up.shape[2]
    assert s % GMLP_CHUNK == 0 and GMLP_CHUNK % t == 0 and s % CHUNK == 0

    pos_p = jnp.arange(s, dtype=jnp.int32)
    pos_s = p_len + jnp.arange(t, dtype=jnp.int32)
    ci = jnp.arange(GMLP_CHUNK)
    chunk_mask = (ci[None, :] // CHUNK) <= (ci[:, None] // CHUNK)
    cache_k4 = cache_k.reshape(depth, dbsz, -1, HEAD_DIM)
    cache_v4 = cache_v.reshape(depth, dbsz, -1, HEAD_DIM)
    gate_col0 = _in_offsets(d)[8]

    xp, xs = x_prompt, x_sample
    outs = [[] for _ in range(9)]
    for l in range(depth):
        wm_full = jnp.where(chunk_mask[None], gmlp_ws[l], 0.0)
        bias_rows = jnp.repeat(jnp.transpose(gmlp_b[l]), da // G_A, axis=1)

        def gmlp_wm(rows, wm_full=wm_full):
            reps = GMLP_CHUNK // rows
            blk = wm_full[:, :rows, :rows]
            eye = jnp.eye(reps, dtype=F32)
            return jnp.einsum("ab,gij->gaibj", eye, blk).reshape(G_A, GMLP_CHUNK, GMLP_CHUNK).astype(BF16)

        def gmlp_bias(rows, bias_rows=bias_rows):
            return jnp.tile(bias_rows[:rows], (GMLP_CHUNK // rows, 1))

        w = dict(
            norm_attn_g=norm_attn_g[l], gmlp_norm_g=gmlp_norm_g[l], norm_ffn_g=norm_ffn_g[l],
            gmlp_wm=gmlp_wm, gmlp_bias=gmlp_bias,
            a=w_branch_a[l], b=w_branch_b[l], o=w_out[l], up=w_up[l], conv_w=conv_w[l], conv_b=conv_b[l],
            down=w_down[l].astype(BF16),
        )
        w["in_t"] = jnp.swapaxes(w_in[l], 0, 1).astype(BF16)
        w["in_gates_t"] = jnp.swapaxes(w_in[l], 0, 1)[gate_col0:].astype(BF16)

        xp, kp, vp, kip, cp, _ = _layer(xp, pos_p, s, w, None, jnp.zeros((bsz, CONV_WIDTH - 1, f2), F32),
                                        False)
        past = (l, cache_kidx, cache_k4, cache_v4)
        xs, ks, vs, kis, cs, gv = _layer(xs, pos_s, t, w, past, state_ffn_conv[l], True)
        for lst, val in zip(outs, (kp, vp, kip, cp, ks, vs, kis, cs, gv.reshape(dbsz, t, da))):
            lst.append(val)

    y_prompt = rmsnorm(xp.reshape(bsz * s, d), norm_final_g, F32).reshape(bsz, s, d)
    y_sample = rmsnorm(xs.reshape(dbsz * t, d), norm_final_g, F32).reshape(dbsz, t, d)
    stack = lambda o: o[0][None] if depth == 1 else jnp.stack(o)
    return (y_prompt, y_sample) + tuple(stack(o) for o in outs)
```

```python
import functools
import math

import jax
import jax.numpy as jnp
from jax import lax
from jax.experimental import pallas as pl
from jax.experimental.pallas import tpu as pltpu

CHUNK = 64
GMLP_CHUNK = 128
G_A = 8
HEAD_DIM = 128
Q_PER_KV = 4
N_IDX_HEADS = 16
IDX_DIM = 64
TOPK_MAX = 256
ROPE_THETA = 500000.0
CONV_WIDTH = 3
EPS = 1e-6

LANE = 128
SUBLANE = 8
VMEM_LIMIT_BYTES = 56 * 1024 * 1024

FFN_UP_ROW_SPLITS = 4
DSA_PREFIX_CLASSES = 4
GMLP_CHUNKS_PER_STEP = 4
DSA_QUERY_BLOCK = 128
TOPK_STEP_SURPLUS = 2.0
TOPK_PROBES_PER_CHECK = 3

BF16 = jnp.bfloat16
F32 = jnp.float32
NT_DIMS = (((1,), (1,)), ((), ()))


def _params(*semantics):
    return pltpu.CompilerParams(dimension_semantics=semantics, vmem_limit_bytes=VMEM_LIMIT_BYTES)


def _tile(n, pref, align):
    if n <= pref:
        return n
    t = (pref // align) * align
    while t >= align:
        if n % t == 0:
            return t
        t -= align
    raise ValueError(f"no {align}-aligned tile of {n} below {pref}")


def _first_row_tile():
    return pl.program_id(1) == 0


def _sigmoid(x):
    return 0.5 * jnp.tanh(0.5 * x) + 0.5


def _rmsnorm_kernel(x_ref, g_ref, o_ref):
    x = x_ref[...]
    ms = jnp.mean(x * x, axis=-1, keepdims=True)
    o_ref[...] = (x * lax.rsqrt(ms + EPS) * g_ref[...]).astype(o_ref.dtype)


def rmsnorm(x, g, out_dtype):
    n, d = x.shape
    tr = _tile(n, 512, SUBLANE)
    return pl.pallas_call(
        _rmsnorm_kernel,
        grid=(n // tr,),
        in_specs=[pl.BlockSpec((tr, d), lambda i: (i, 0)), pl.BlockSpec((1, d), lambda i: (0, 0))],
        out_specs=pl.BlockSpec((tr, d), lambda i: (i, 0)),
        out_shape=jax.ShapeDtypeStruct((n, d), out_dtype),
        compiler_params=_params("parallel"),
        name="rmsnorm",
    )(x, g.reshape(1, d))


def _rope_slab(x, c, s1, s2, shift):
    return x * c + pltpu.roll(x, LANE - shift, 1) * s1 + pltpu.roll(x, shift, 1) * s2


def _proj_kernel(a_ref, b_ref, *rest, rope_shift):
    acc = lax.dot_general(a_ref[...], b_ref[...], NT_DIMS, preferred_element_type=F32)
    if rope_shift:
        c_ref, s1_ref, s2_ref, o_ref = rest
        c, s1, s2 = c_ref[...], s1_ref[...], s2_ref[...]
        for h in range(acc.shape[1] // LANE):
            sl = slice(h * LANE, (h + 1) * LANE)
            o_ref[:, sl] = _rope_slab(acc[:, sl], c, s1, s2, rope_shift).astype(o_ref.dtype)
    else:
        (o_ref,) = rest
        o_ref[...] = acc.astype(o_ref.dtype)


def project(a, wt, col0, ncols, out_dtype, rope=None):
    m, k = a.shape
    tm = _tile(m, 1024, SUBLANE)
    tn = _tile(math.gcd(ncols, col0), 1024, LANE)
    assert col0 % tn == 0 and ncols % tn == 0
    j0 = col0 // tn
    in_specs = [pl.BlockSpec((tm, k), lambda j, i: (i, 0)), pl.BlockSpec((tn, k), lambda j, i: (j + j0, 0))]
    args = [a, wt]
    shift = 0
    if rope is not None:
        shift, tabs = rope[0], rope[1:]
        reps = tabs[0].shape[0] // tm
        assert reps * tm == tabs[0].shape[0]
        for t in tabs:
            in_specs.append(pl.BlockSpec((tm, LANE), lambda j, i: (i % reps, 0)))
            args.append(t)
    return pl.pallas_call(
        functools.partial(_proj_kernel, rope_shift=shift),
        grid=(ncols // tn, m // tm),
        in_specs=in_specs,
        out_specs=pl.BlockSpec((tm, tn), lambda j, i: (i, j)),
        out_shape=jax.ShapeDtypeStruct((m, ncols), out_dtype),
        compiler_params=_params("parallel", "arbitrary"),
        name="project",
    )(*args)


def _proj_kv_kernel(a_ref, b_ref, ck_ref, s1k_ref, s2k_ref, cw_ref, s1w_ref, s2w_ref,
                    k_ref, v_ref, kw_ref, *, shift_k, shift_w):
    g = pl.program_id(1)
    tm = a_ref.shape[0]
    n_heads = b_ref.shape[0] // LANE

    def matmul(rows):
        return lax.dot_general(a_ref[...], b_ref[0:rows, :], NT_DIMS, preferred_element_type=F32)

    @pl.when(g == 0)
    def _():
        acc = matmul(n_heads * LANE)
        c, s1, s2 = ck_ref[...], s1k_ref[...], s2k_ref[...]
        for h in range(n_heads):
            k_ref[pl.ds(h, tm, stride=n_heads), :] = _rope_slab(acc[:, h * LANE:(h + 1) * LANE], c, s1, s2, shift_k)

    @pl.when(g == 1)
    def _():
        acc = matmul(n_heads * LANE)
        for h in range(n_heads):
            v_ref[pl.ds(h, tm, stride=n_heads), :] = acc[:, h * LANE:(h + 1) * LANE]

    @pl.when(g == 2)
    def _():
        kw_ref[...] = _rope_slab(matmul(LANE), cw_ref[...], s1w_ref[...], s2w_ref[...], shift_w)


def project_kv(a, wt, col_k, col_v, col_kw, width, rope_k, rope_kw):
    m, kdim = a.shape
    tm = _tile(m, 1024, SUBLANE)
    assert col_k % width == 0 and col_v % width == 0 and col_kw % width == 0
    assert col_kw + width <= wt.shape[0] and width % LANE == 0
    blocks = (col_k // width, col_v // width, col_kw // width)
    heads = width // LANE

    def w_map(i, g):
        return (jnp.where(g == 0, blocks[0], jnp.where(g == 1, blocks[1], blocks[2])), 0)

    tabs = rope_k[1:] + rope_kw[1:]
    reps = tabs[0].shape[0] // tm
    assert reps * tm == tabs[0].shape[0]
    return pl.pallas_call(
        functools.partial(_proj_kv_kernel, shift_k=rope_k[0], shift_w=rope_kw[0]),
        grid=(m // tm, 3),
        in_specs=[pl.BlockSpec((tm, kdim), lambda i, g: (i, 0)), pl.BlockSpec((width, kdim), w_map)]
        + [pl.BlockSpec((tm, LANE), lambda i, g: (i % reps, 0)) for _ in tabs],
        out_specs=[
            pl.BlockSpec((tm * heads, LANE), lambda i, g: (i, 0)),
            pl.BlockSpec((tm * heads, LANE), lambda i, g: (i, 0)),
            pl.BlockSpec((tm, LANE), lambda i, g: (i, 0)),
        ],
        out_shape=[
            jax.ShapeDtypeStruct((m * heads, LANE), F32),
            jax.ShapeDtypeStruct((m * heads, LANE), F32),
            jax.ShapeDtypeStruct((m, LANE), F32),
        ],
        compiler_params=_params("parallel", "arbitrary"),
        name="project_kv",
    )(a, wt, *tabs)


def _rope_tables(pos, head_dim, rot_dim, rows):
    half = rot_dim // 2
    inv_freq = ROPE_THETA ** (-jnp.arange(half, dtype=F32) / half)
    ang = pos.astype(F32)[:, None] * inv_freq[None, :]
    cos, sin = jnp.cos(ang), jnp.sin(ang)
    t = pos.shape[0]
    zeros_h = jnp.zeros((t, half), F32)
    rest0 = jnp.zeros((t, head_dim - rot_dim), F32)
    c = jnp.concatenate([cos, cos, jnp.ones((t, head_dim - rot_dim), F32)], axis=1)
    s1 = jnp.concatenate([-sin, zeros_h, rest0], axis=1)
    s2 = jnp.concatenate([zeros_h, sin, rest0], axis=1)
    reps_l = LANE // head_dim
    reps_r = rows // t
    return tuple(jnp.tile(x, (reps_r, reps_l)) for x in (c, s1, s2))


def _gmlp_kernel(u_ref, v_ref, g_ref, wm_ref, b_ref, a_ref, *vn_out, groups):
    gw = v_ref.shape[1] // groups
    for c in range(v_ref.shape[0] // GMLP_CHUNK):
        rows = slice(c * GMLP_CHUNK, (c + 1) * GMLP_CHUNK)
        v = v_ref[rows, :]
        vn = v * lax.rsqrt(jnp.mean(v * v, axis=-1, keepdims=True) + EPS) * g_ref[...]
        if vn_out:
            vn_out[0][rows, :] = vn
        vb = vn.astype(BF16)
        for g in range(groups):
            sl = slice(g * gw, (g + 1) * gw)
            s = jnp.dot(wm_ref[g], vb[:, sl], preferred_element_type=F32) + b_ref[:, sl]
            a_ref[rows, sl] = (u_ref[rows, sl] * s).astype(a_ref.dtype)


def gmlp(uv, g_norm, wm, bias, want_vn):
    m, d2 = uv.shape
    da = d2 // 2
    tc = GMLP_CHUNK * math.gcd(m // GMLP_CHUNK, GMLP_CHUNKS_PER_STEP)
    out_shape = [jax.ShapeDtypeStruct((m, da), BF16)]
    out_specs = [pl.BlockSpec((tc, da), lambda i: (i, 0))]
    if want_vn:
        out_shape.append(jax.ShapeDtypeStruct((m, da), F32))
        out_specs.append(pl.BlockSpec((tc, da), lambda i: (i, 0)))
    res = pl.pallas_call(
        functools.partial(_gmlp_kernel, groups=wm.shape[0]),
        grid=(m // tc,),
        in_specs=[
            pl.BlockSpec((tc, da), lambda i: (i, 0)),
            pl.BlockSpec((tc, da), lambda i: (i, 1)),
            pl.BlockSpec((1, da), lambda i: (0, 0)),
            pl.BlockSpec(wm.shape, lambda i: (0, 0, 0)),
            pl.BlockSpec((GMLP_CHUNK, da), lambda i: (0, 0)),
        ],
        out_specs=out_specs,
        out_shape=out_shape,
        compiler_params=_params("parallel"),
        name="gmlp",
    )(uv, uv, g_norm.reshape(1, da), wm, bias)
    return res if want_vn else (res[0], None)


LOG2_E = 1.4426950408889634
ROW_REDUCE_GROUP = 64
ATTN_ROW_GROUP = 32


def _reduce_rows(x, op):
    pair = {jnp.sum: jnp.add, jnp.min: jnp.minimum, jnp.max: jnp.maximum}[op]
    rows = x.shape[0]
    if rows % ROW_REDUCE_GROUP == 0:
        parts = [x[i:i + ROW_REDUCE_GROUP] for i in range(0, rows, ROW_REDUCE_GROUP)]
        while len(parts) > 1:
            parts = [pair(parts[i], parts[i + 1]) if i + 1 < len(parts) else parts[i]
                     for i in range(0, len(parts), 2)]
        x = parts[0]
    return op(x, axis=0, keepdims=True)

def _dsa_select_attend(qi_ref, kwq_ref, q_ref, klo_ref, khi_ref, ks_ref, vt_ref, o_ref, bias_ref,
                       lg_ref, p_ref, *,
                       s_c, tq, n_kv, topk, row0, n_valid, scale):
    n_tiles = s_c // LANE
    reps = LANE // tq

    def rep_rows(x):
        return x if reps == 1 else jnp.concatenate([x] * reps, axis=0)

    w_t = jnp.transpose(rep_rows(kwq_ref[...]))
    klo = klo_ref[0:s_c, :]
    khi = khi_ref[0:s_c, :]

    score = jnp.zeros((s_c, LANE), F32)
    for pp in range(N_IDX_HEADS // 4):
        qp2 = jnp.concatenate([rep_rows(qi_ref[:, (2 * pp + i) * LANE:(2 * pp + i + 1) * LANE])
                               for i in range(2)], axis=0)
        rel_lo = lax.dot_general(klo, qp2, NT_DIMS, preferred_element_type=F32)
        rel_hi = lax.dot_general(khi, qp2, NT_DIMS, preferred_element_type=F32)
        for i in range(2):
            r0 = IDX_DIM + 2 * (2 * pp + i)
            score = (score + jnp.maximum(rel_lo[:, i * LANE:(i + 1) * LANE], 0.0) * w_t[r0:r0 + 1, :]
                     + jnp.maximum(rel_hi[:, i * LANE:(i + 1) * LANE], 0.0) * w_t[r0 + 1:r0 + 2, :])

    kpos = lax.broadcasted_iota(jnp.int32, (s_c, LANE), 0)
    if row0 is not None:
        qrow = row0 + jnp.bitwise_and(lax.broadcasted_iota(jnp.int32, (s_c, LANE), 1), tq - 1)
        adm = kpos < (jnp.right_shift(qrow, CHUNK.bit_length() - 1) + 1) * CHUNK
    else:
        adm = kpos < n_valid
    score = jnp.where(adm, score, -jnp.inf)
    bias_ref[0:s_c, :] = score

    kf = float(topk)
    lo0 = _reduce_rows(jnp.where(adm, score, jnp.inf), jnp.min)
    hi0 = _reduce_rows(score, jnp.max)
    c_lo0 = _reduce_rows(jnp.where(adm, 1.0, 0.0), jnp.sum)
    c_hi0 = jnp.zeros((1, LANE), F32)

    def probe(t):
        cnt = jnp.zeros((ROW_REDUCE_GROUP, LANE), F32)
        nxt = jnp.full((ROW_REDUCE_GROUP, LANE), jnp.inf, F32)
        for i in range(0, s_c, ROW_REDUCE_GROUP):
            sc = bias_ref[i:i + ROW_REDUCE_GROUP, :]
            above = sc > t
            cnt = cnt + jnp.where(above, 1.0, 0.0)
            nxt = jnp.minimum(nxt, jnp.where(above, sc, jnp.inf))
        return jnp.sum(cnt, axis=0, keepdims=True), jnp.min(nxt, axis=0, keepdims=True)

    def active_of(lo, hi, c_lo):
        return (c_lo > kf) & (hi > lo)

    def cond(carry):
        lo, hi, c_lo, _ = carry
        return jnp.max(jnp.where(active_of(lo, hi, c_lo), 1.0, 0.0)) > 0.0

    def body(carry):
        for _ in range(TOPK_PROBES_PER_CHECK):
            carry = advance(carry)
        return carry

    def advance(carry):
        lo, hi, c_lo, c_hi = carry
        act = active_of(lo, hi, c_lo)
        mid = 0.5 * lo + 0.5 * hi
        step = (c_lo - kf <= TOPK_STEP_SURPLUS) | (mid <= lo) | (mid >= hi)
        t = jnp.where(step, lo, mid)
        cnt, nxt = probe(t)
        take = act & (cnt >= kf)
        drop = act & (cnt < kf)
        return (jnp.where(take, nxt, lo), jnp.where(drop, t, hi),
                jnp.where(take, cnt, c_lo), jnp.where(drop, cnt, c_hi))

    lo, hi, c_lo, c_hi = lax.while_loop(cond, body, (lo0, hi0, c_lo0, c_hi0))
    tie = c_lo > kf
    any_tie = jnp.max(jnp.where(tie, 1.0, 0.0)) > 0.0

    @pl.when(jnp.logical_not(any_tie))
    def _():
        for jt in range(n_tiles):
            sl = slice(jt * LANE, (jt + 1) * LANE)
            bias_ref[sl, :] = jnp.where(bias_ref[sl, :] >= lo, 0.0, -jnp.inf)

    @pl.when(any_tie)
    def _():
        quota = kf - c_hi
        tri = (lax.broadcasted_iota(jnp.int32, (LANE, LANE), 1)
               <= lax.broadcasted_iota(jnp.int32, (LANE, LANE), 0)).astype(F32).astype(BF16)
        before = jnp.zeros((1, LANE), F32)
        for jt in range(n_tiles):
            sl = slice(jt * LANE, (jt + 1) * LANE)
            sc = bias_ref[sl, :]
            cand = jnp.where((sc >= lo) & (sc <= hi), 1.0, 0.0)
            rank = jnp.dot(tri, cand.astype(BF16), preferred_element_type=F32) + before
            keep = (sc > hi) | ((cand > 0.0) & (rank <= quota))
            bias_ref[sl, :] = jnp.where(tie, jnp.where(keep, 0.0, -jnp.inf),
                                        jnp.where(sc >= lo, 0.0, -jnp.inf))
            before = before + jnp.sum(cand, axis=0, keepdims=True)

    width = Q_PER_KV * tq
    n_slabs = width // LANE
    grp = ATTN_ROW_GROUP
    def put_logits(n):
        qn = jnp.concatenate(
            [q_ref[:, (n * Q_PER_KV + g) * HEAD_DIM:(n * Q_PER_KV + g + 1) * HEAD_DIM]
             for g in range(Q_PER_KV)], axis=0)
        lg_ref[n % 2, 0:s_c, :] = lax.dot_general(
            ks_ref[0:s_c, n * HEAD_DIM:(n + 1) * HEAD_DIM], qn, NT_DIMS,
            preferred_element_type=F32) * (scale * LOG2_E)

    put_logits(0)
    for n in range(n_kv):
        if n + 1 < n_kv:
            put_logits(n + 1)
        ksl = slice(n * HEAD_DIM, (n + 1) * HEAD_DIM)
        lgn = lg_ref.at[n % 2]
        pn = p_ref.at[n % 2]
        mpart = jnp.full((grp, width), -jnp.inf, F32)
        for i in range(0, s_c, grp):
            b = bias_ref[i:i + grp, :]
            lg = lgn[i:i + grp, :] + (b if n_slabs == 1 else jnp.concatenate([b] * n_slabs, axis=1))
            lgn[i:i + grp, :] = lg
            mpart = jnp.maximum(mpart, lg)
        mx = jnp.max(mpart, axis=0, keepdims=True)
        dpart = jnp.zeros((grp, width), F32)
        for i in range(0, s_c, grp):
            p = jnp.exp2(lgn[i:i + grp, :] - mx)
            dpart = dpart + p
            pn[i:i + grp, :] = p.astype(BF16)
        den = jnp.sum(dpart, axis=0, keepdims=True)
        o_t = jnp.dot(vt_ref[ksl, 0:s_c], pn[0:s_c, :], preferred_element_type=F32) / den
        for sb in range(n_slabs):
            o = jnp.transpose(o_t[:, sb * LANE:(sb + 1) * LANE])
            for r in range(reps):
                g = sb * reps + r
                hsl = slice((n * Q_PER_KV + g) * HEAD_DIM, (n * Q_PER_KV + g + 1) * HEAD_DIM)
                o_ref[:, hsl] = o[r * tq:(r + 1) * tq].astype(o_ref.dtype)


def _dsa_kernel(qi_ref, kwq_ref, q_ref, kw_ref, k_ref, v_ref, *rest, tq, t_new, p_len, n_classes, topk):
    if p_len:
        (pki_ref, pk_ref, pv_ref, o_ref, klo_ref, khi_ref, ks_ref, vt_ref, bias_ref, lg_ref, p_ref,
         tmp_ref) = rest
    else:
        o_ref, klo_ref, khi_ref, ks_ref, vt_ref, bias_ref, lg_ref, p_ref = rest
        pk_ref = pv_ref = None
    s_pad = ks_ref.shape[0]
    n_valid = p_len + t_new
    n_kv = ks_ref.shape[1] // HEAD_DIM
    j = pl.program_id(1)

    def head_rows(src_ref, rows, n):
        return src_ref[pl.ds(n, rows, stride=n_kv), :]

    @pl.when(j == 0)
    def _():
        lane = lax.broadcasted_iota(jnp.int32, (t_new, LANE), 1)
        new_lo = jnp.where(lane < IDX_DIM, kw_ref[...], 0.0)
        if p_len:
            tmp_ref[...] = jnp.zeros(tmp_ref.shape, F32)
            tmp_ref[0:p_len, 0:IDX_DIM] = pki_ref[...]
            tmp_ref[p_len:n_valid, :] = new_lo
            lo = tmp_ref[...]
        else:
            lo = new_lo
        klo_ref[...] = lo.astype(BF16)
        khi_ref[...] = pltpu.roll(lo, IDX_DIM, 1).astype(BF16)
        tail = jnp.zeros((s_pad - n_valid, HEAD_DIM), F32)
        for n in range(n_kv):
            csl = slice(n * HEAD_DIM, (n + 1) * HEAD_DIM)
            new_k = head_rows(k_ref, t_new, n)
            new_v = head_rows(v_ref, t_new, n)
            if s_pad > n_valid:
                new_k = jnp.concatenate([new_k, tail], axis=0)
                new_v = jnp.concatenate([new_v, tail], axis=0)
            if p_len:
                ks_ref[0:p_len, csl] = head_rows(pk_ref, p_len, n).astype(BF16)
                vt_ref[csl, 0:p_len] = jnp.transpose(head_rows(pv_ref, p_len, n)).astype(BF16)
            ks_ref[p_len:s_pad, csl] = new_k.astype(BF16)
            vt_ref[csl, p_len:s_pad] = jnp.transpose(new_v).astype(BF16)

    common = dict(tq=tq, n_kv=n_kv, topk=topk, n_valid=n_valid, scale=HEAD_DIM ** -0.5)
    refs = (qi_ref, kwq_ref, q_ref, klo_ref, khi_ref, ks_ref, vt_ref, o_ref, bias_ref, lg_ref, p_ref)
    if p_len:
        _dsa_select_attend(*refs, s_c=s_pad, row0=None, **common)
    else:
        per_class = pl.num_programs(1) // n_classes
        for c in range(n_classes):
            @pl.when(j // per_class == c)
            def _(c=c):
                _dsa_select_attend(*refs, s_c=(c + 1) * (s_pad // n_classes), row0=j * tq, **common)


def dsa(qi, kw, q, k, v, past, *, tq, topk):
    b, t, dq = q.shape
    n_kv = k.shape[1] // t
    dkv = n_kv * HEAD_DIM
    nq = t // tq
    row = lambda i, j: (i, j, 0)
    full = lambda i, j: (i, 0, 0)
    in_specs = [
        pl.BlockSpec((None, tq, qi.shape[2]), row),
        pl.BlockSpec((None, tq, LANE), row),
        pl.BlockSpec((None, tq, dq), row),
        pl.BlockSpec((None, t, LANE), full),
        pl.BlockSpec((None, t * n_kv, HEAD_DIM), full),
        pl.BlockSpec((None, t * n_kv, HEAD_DIM), full),
    ]
    args = [qi, kw, q, kw, k, v]
    if past is None:
        p_len = 0
        s_pad = t
        n_classes = DSA_PREFIX_CLASSES if (nq % DSA_PREFIX_CLASSES == 0
                                           and t % (DSA_PREFIX_CLASSES * LANE) == 0) else 1
        assert tq % CHUNK == 0 and LANE % tq == 0
    else:
        layer, pki, pk, pv = past
        p_len = pki.shape[2]
        s_pad = -(-(p_len + t) // LANE) * LANE
        n_classes = 1
        assert nq == 1 and p_len % LANE == 0 and LANE % tq == 0 and Q_PER_KV * tq % LANE == 0
        cache = lambda i, j: (layer, i, 0, 0)
        in_specs += [pl.BlockSpec((None, None, p_len, IDX_DIM), cache),
                     pl.BlockSpec((None, None, p_len * n_kv, HEAD_DIM), cache),
                     pl.BlockSpec((None, None, p_len * n_kv, HEAD_DIM), cache)]
        args += [pki, pk, pv]
    scratch = [pltpu.VMEM((s_pad, LANE), BF16), pltpu.VMEM((s_pad, LANE), BF16),
               pltpu.VMEM((s_pad, dkv), BF16), pltpu.VMEM((dkv, s_pad), BF16),
               pltpu.VMEM((s_pad, LANE), F32),
               pltpu.VMEM((2, s_pad, Q_PER_KV * tq), F32), pltpu.VMEM((2, s_pad, Q_PER_KV * tq), BF16)]
    if past is not None:
        scratch.append(pltpu.VMEM((s_pad, LANE), F32))
    kern = functools.partial(_dsa_kernel, tq=tq, t_new=t, p_len=p_len, n_classes=n_classes, topk=topk)
    return pl.pallas_call(
        kern,
        grid=(b, nq),
        in_specs=in_specs,
        out_specs=pl.BlockSpec((None, tq, dq), row),
        out_shape=jax.ShapeDtypeStruct((b, t, dq), BF16),
        scratch_shapes=scratch,
        compiler_params=_params("arbitrary", "arbitrary"),
        name="dsa",
    )(*args)


def _merge_kernel(a_ref, b_ref, wa_ref, wb_ref, ga_ref, gb_ref, o_ref, waq_ref, wbq_ref):
    @pl.when(_first_row_tile())
    def _():
        waq_ref[...] = wa_ref[...].astype(BF16)
        wbq_ref[...] = wb_ref[...].astype(BF16)

    ya = jnp.dot(a_ref[...], waq_ref[...], preferred_element_type=F32)
    yb = jnp.dot(b_ref[...], wbq_ref[...], preferred_element_type=F32)
    y = _sigmoid(ga_ref[...]) * ya + _sigmoid(gb_ref[...]) * yb
    o_ref[...] = y.astype(o_ref.dtype)


def merge(a, b, wa, wb, gates):
    m, ka = a.shape
    kb = b.shape[1]
    d = wa.shape[1]
    tm = _tile(m, 1024, SUBLANE)
    tn = _tile(d, 512, LANE)
    nj = d // tn
    return pl.pallas_call(
        _merge_kernel,
        grid=(nj, m // tm),
        in_specs=[
            pl.BlockSpec((tm, ka), lambda j, i: (i, 0)),
            pl.BlockSpec((tm, kb), lambda j, i: (i, 0)),
            pl.BlockSpec((ka, tn), lambda j, i: (0, j)),
            pl.BlockSpec((kb, tn), lambda j, i: (0, j)),
            pl.BlockSpec((tm, tn), lambda j, i: (i, j)),
            pl.BlockSpec((tm, tn), lambda j, i: (i, j + nj)),
        ],
        out_specs=pl.BlockSpec((tm, tn), lambda j, i: (i, j)),
        out_shape=jax.ShapeDtypeStruct((m, d), BF16),
        scratch_shapes=[pltpu.VMEM((ka, tn), BF16), pltpu.VMEM((kb, tn), BF16)],
        compiler_params=_params("arbitrary", "arbitrary"),
        name="merge",
    )(a, b, wa, wb, gates, gates)


def _matmul_res_kernel(a_ref, b_ref, r_ref, o_ref, bq_ref):
    @pl.when(_first_row_tile())
    def _():
        bq_ref[...] = b_ref[...].astype(BF16)

    o_ref[...] = r_ref[...] + jnp.dot(a_ref[...], bq_ref[...], preferred_element_type=F32)


def matmul_residual(a, b, res):
    m, k = a.shape
    n = b.shape[1]
    tm = _tile(m, 1024, SUBLANE)
    tn = _tile(n, 512, LANE)
    return pl.pallas_call(
        _matmul_res_kernel,
        grid=(n // tn, m // tm),
        in_specs=[
            pl.BlockSpec((tm, k), lambda j, i: (i, 0)),
            pl.BlockSpec((k, tn), lambda j, i: (0, j)),
            pl.BlockSpec((tm, tn), lambda j, i: (i, j)),
        ],
        out_specs=pl.BlockSpec((tm, tn), lambda j, i: (i, j)),
        out_shape=jax.ShapeDtypeStruct((m, n), F32),
        scratch_shapes=[pltpu.VMEM((k, tn), BF16)],
        compiler_params=_params("arbitrary", "arbitrary"),
        name="matmul_residual",
    )(a, b, res)


def _ffn_up_kernel(h_ref, wg_ref, wu_ref, cwg_ref, cwu_ref, cbg_ref, cbu_ref, sg_ref, su_ref,
                   act_ref, zg_ref, zu_ref, wq_ref, buf_ref, *, seq_len, tiles_per_seq, n_sub):
    tm = h_ref.shape[0]
    tn = act_ref.shape[1]
    pad = SUBLANE

    @pl.when(_first_row_tile())
    def _():
        wq_ref[:, 0:tn] = wg_ref[...].astype(BF16)
        wq_ref[:, tn:2 * tn] = wu_ref[...].astype(BF16)

    cw = jnp.concatenate([cwg_ref[...], cwu_ref[...]], axis=1)
    cb = jnp.concatenate([cbg_ref[...], cbu_ref[...]], axis=1)

    def gated(c):
        gate, up = c[:, :tn], c[:, tn:]
        return (gate * _sigmoid(gate) * up).astype(act_ref.dtype)

    if tiles_per_seq >= 1:
        @pl.when(pl.program_id(1) % tiles_per_seq == 0)
        def _():
            buf_ref[pad - 2:pad, 0:tn] = sg_ref[0]
            buf_ref[pad - 2:pad, tn:2 * tn] = su_ref[0]
        rs = tm // n_sub
        zs = [jnp.dot(h_ref[r * rs:(r + 1) * rs, :], wq_ref[...], preferred_element_type=F32)
              for r in range(n_sub)]
        for r, z in enumerate(zs):
            base = pad + r * rs
            buf_ref[base:base + rs, :] = z
            zext = buf_ref[base - pad:base + rs, :]
            z1 = pltpu.roll(zext, 1, 0)[pad:]
            z2 = pltpu.roll(zext, 2, 0)[pad:]
            c = cb + cw[0:1] * z2 + cw[1:2] * z1 + cw[2:3] * z
            act_ref[r * rs:(r + 1) * rs, :] = gated(c)
        tail = buf_ref[pad + tm - 2:pad + tm, :]
        zg_ref[0] = tail[:, :tn]
        zu_ref[0] = tail[:, tn:]
        buf_ref[pad - 2:pad, :] = tail
    else:
        buf_ref[0:pad, :] = jnp.zeros((pad, 2 * tn), F32)
        buf_ref[pad:pad + tm, :] = jnp.dot(h_ref[...], wq_ref[...], preferred_element_type=F32)
        rowi = lax.broadcasted_iota(jnp.int32, (seq_len, 2 * tn), 0)
        for s in range(tm // seq_len):
            base = pad + s * seq_len
            st0 = jnp.concatenate([sg_ref[s, 0:1, :], su_ref[s, 0:1, :]], axis=1)
            st1 = jnp.concatenate([sg_ref[s, 1:2, :], su_ref[s, 1:2, :]], axis=1)
            z0 = buf_ref[base:base + seq_len, :]
            p1 = jnp.where(rowi == 0, st1, buf_ref[base - 1:base - 1 + seq_len, :])
            p2 = jnp.where(rowi == 0, st0, jnp.where(rowi == 1, st1, buf_ref[base - 2:base - 2 + seq_len, :]))
            act_ref[s * seq_len:(s + 1) * seq_len, :] = gated(cb + cw[0:1] * p2 + cw[1:2] * p1 + cw[2:3] * z0)
            tail = buf_ref[base + seq_len - 2:base + seq_len, :]
            zg_ref[s] = tail[:, :tn]
            zu_ref[s] = tail[:, tn:]


def ffn_up(h, w_up, conv_w, conv_b, state, seq_len):
    m, d = h.shape
    f = w_up.shape[1] // 2
    tm = _tile(m, 1024, SUBLANE)
    tn = _tile(f, 256, LANE)
    nj = f // tn
    if seq_len >= tm:
        assert seq_len % tm == 0
        tiles_per_seq = seq_len // tm
        n_state = 1
        state_map_g = lambda j, i: (i // tiles_per_seq, 0, j)
        state_map_u = lambda j, i: (i // tiles_per_seq, 0, j + nj)
        n_last = m // tm
    else:
        assert tm % seq_len == 0 and seq_len % SUBLANE == 0
        tiles_per_seq = 0
        n_state = tm // seq_len
        state_map_g = lambda j, i: (i, 0, j)
        state_map_u = lambda j, i: (i, 0, j + nj)
        n_last = m // seq_len
    kern = functools.partial(_ffn_up_kernel, seq_len=seq_len, tiles_per_seq=tiles_per_seq,
                             n_sub=FFN_UP_ROW_SPLITS)
    cb = conv_b.reshape(1, 2 * f)
    act, zg, zu = pl.pallas_call(
        kern,
        grid=(nj, m // tm),
        in_specs=[
            pl.BlockSpec((tm, d), lambda j, i: (i, 0)),
            pl.BlockSpec((d, tn), lambda j, i: (0, j)),
            pl.BlockSpec((d, tn), lambda j, i: (0, j + nj)),
            pl.BlockSpec((CONV_WIDTH, tn), lambda j, i: (0, j)),
            pl.BlockSpec((CONV_WIDTH, tn), lambda j, i: (0, j + nj)),
            pl.BlockSpec((1, tn), lambda j, i: (0, j)),
            pl.BlockSpec((1, tn), lambda j, i: (0, j + nj)),
            pl.BlockSpec((n_state, 2, tn), state_map_g),
            pl.BlockSpec((n_state, 2, tn), state_map_u),
        ],
        out_specs=[
            pl.BlockSpec((tm, tn), lambda j, i: (i, j)),
            pl.BlockSpec((n_state, 2, tn), lambda j, i: (i, 0, j)),
            pl.BlockSpec((n_state, 2, tn), lambda j, i: (i, 0, j)),
        ],
        out_shape=[
            jax.ShapeDtypeStruct((m, f), BF16),
            jax.ShapeDtypeStruct((n_last, 2, f), F32),
            jax.ShapeDtypeStruct((n_last, 2, f), F32),
        ],
        scratch_shapes=[pltpu.VMEM((d, 2 * tn), BF16), pltpu.VMEM((tm + SUBLANE, 2 * tn), F32)],
        compiler_params=_params("arbitrary", "arbitrary"),
        name="ffn_up",
    )(h, w_up, w_up, conv_w, conv_w, cb, cb, state, state)
    zlast = jnp.concatenate([zg, zu], axis=-1)
    if tiles_per_seq > 1:
        zlast = zlast[tiles_per_seq - 1::tiles_per_seq]
    return act, zlast


def _ffn_down_kernel(a_ref, b_ref, r_ref, o_ref):
    o_ref[...] = r_ref[...] + jnp.dot(a_ref[...], b_ref[...], preferred_element_type=F32)


def ffn_down(a, b, res):
    m, k = a.shape
    n = b.shape[1]
    tm = _tile(m, 512, SUBLANE)
    tn = _tile(n, 512, LANE)
    return pl.pallas_call(
        _ffn_down_kernel,
        grid=(m // tm, n // tn),
        in_specs=[
            pl.BlockSpec((tm, k), lambda i, j: (i, 0)),
            pl.BlockSpec((k, tn), lambda i, j: (0, j)),
            pl.BlockSpec((tm, tn), lambda i, j: (i, j)),
        ],
        out_specs=pl.BlockSpec((tm, tn), lambda i, j: (i, j)),
        out_shape=jax.ShapeDtypeStruct((m, n), F32),
        compiler_params=_params("parallel", "arbitrary"),
        name="ffn_down",
    )(a, b, res)


def _in_offsets(d):
    da = d // 2
    n_heads = d // 256
    n_kv = n_heads // Q_PER_KV
    sizes = (da, da, n_heads * HEAD_DIM, n_kv * HEAD_DIM, n_kv * HEAD_DIM,
             N_IDX_HEADS * IDX_DIM, IDX_DIM, N_IDX_HEADS, d, d)
    offs = [0]
    for s in sizes:
        offs.append(offs[-1] + s)
    return offs


def _layer(x, pos, seq_len, w, past, conv_state, want_vn):
    bsz, t, d = x.shape
    m = bsz * t
    x2 = x.reshape(m, d)
    n_heads = d // 256
    n_kv = n_heads // Q_PER_KV
    offs = _in_offsets(d)
    w_in = w["in_t"]
    assert offs[-1] == w_in.shape[0] and offs[8] - offs[6] <= LANE

    h = rmsnorm(x2, w["norm_attn_g"], BF16)

    rows = max(t, _tile(m, 1024, SUBLANE))
    rope_h = (HEAD_DIM // 8,) + _rope_tables(pos, HEAD_DIM, HEAD_DIM // 4, rows)
    rope_i = (IDX_DIM // 8,) + _rope_tables(pos, IDX_DIM, IDX_DIM // 4, rows)
    ci, s1i, s2i = rope_i[1:]
    lane = jnp.arange(LANE)
    is_w = (lane >= IDX_DIM) & (lane < IDX_DIM + N_IDX_HEADS)
    w_scale = N_IDX_HEADS ** -0.5 * IDX_DIM ** -0.5
    rope_kw = (IDX_DIM // 8,
               jnp.where(is_w, w_scale, jnp.where(lane < IDX_DIM, ci, 1.0)).astype(F32),
               jnp.where(lane < IDX_DIM, s1i, 0.0), jnp.where(lane < IDX_DIM, s2i, 0.0))

    uv = project(h, w_in, offs[0], offs[2] - offs[0], F32)
    q = project(h, w_in, offs[2], offs[3] - offs[2], BF16, rope_h)
    qi = project(h, w_in, offs[5], offs[6] - offs[5], BF16, rope_i)
    k, v, kw = project_kv(h, w_in, offs[3], offs[4], offs[6], offs[4] - offs[3], rope_h, rope_kw)
    gates = project(h, w["in_gates_t"], 0, 2 * d, F32)

    chunk_rows = min(t, GMLP_CHUNK)
    a_out, vn = gmlp(uv, w["gmlp_norm_g"], w["gmlp_wm"](chunk_rows), w["gmlp_bias"](chunk_rows), want_vn)

    n_valid = t if past is None else past[1].shape[2] + t
    b_out = dsa(qi.reshape(bsz, t, -1), kw.reshape(bsz, t, LANE), q.reshape(bsz, t, -1),
                k.reshape(bsz, t * n_kv, HEAD_DIM), v.reshape(bsz, t * n_kv, HEAD_DIM), past,
                tq=min(DSA_QUERY_BLOCK, t), topk=min(TOPK_MAX, n_valid // 4))

    y = merge(a_out, b_out.reshape(m, n_heads * HEAD_DIM), w["a"], w["b"], gates)
    x2 = matmul_residual(y, w["o"], x2)

    hf = rmsnorm(x2, w["norm_ffn_g"], BF16)
    act, zlast = ffn_up(hf, w["up"], w["conv_w"], w["conv_b"], conv_state, seq_len)
    x2 = ffn_down(act, w["down"], x2)
    kidx = kw[:, :IDX_DIM].reshape(bsz, t, IDX_DIM)
    return (x2.reshape(bsz, t, d), k.reshape(bsz, t, n_kv, HEAD_DIM), v.reshape(bsz, t, n_kv, HEAD_DIM),
            kidx, zlast, vn)


def kernel(x_prompt, x_sample, cache_k, cache_v, cache_kidx, state_ffn_conv, norm_attn_g, w_in, gmlp_norm_g, gmlp_ws, gmlp_b, w_branch_a, w_branch_b, w_out, norm_ffn_g, w_up, conv_w, conv_b, w_down, norm_final_g):
    bsz, s, d = x_prompt.shape
    dbsz, t, _ = x_sample.shape
    depth = w_in.shape[0]
    p_len = cache_k.shape[2]
    da = d // 2
    f2 = w_up.shape[2]
    assert s % GMLP_CHUNK == 0 and GMLP_CHUNK % t == 0 and s % CHUNK == 0

    pos_p = jnp.arange(s, dtype=jnp.int32)
    pos_s = p_len + jnp.arange(t, dtype=jnp.int32)
    ci = jnp.arange(GMLP_CHUNK)
    chunk_mask = (ci[None, :] // CHUNK) <= (ci[:, None] // CHUNK)
    cache_k4 = cache_k.reshape(depth, dbsz, -1, HEAD_DIM)
    cache_v4 = cache_v.reshape(depth, dbsz, -1, HEAD_DIM)
    gate_col0 = _in_offsets(d)[8]

    xp, xs = x_prompt, x_sample
    outs = [[] for _ in range(9)]
    for l in range(depth):
        wm_full = jnp.where(chunk_mask[None], gmlp_ws[l], 0.0)
        bias_rows = jnp.repeat(jnp.transpose(gmlp_b[l]), da // G_A, axis=1)

        def gmlp_wm(rows, wm_full=wm_full):
            reps = GMLP_CHUNK // rows
            blk = wm_full[:, :rows, :rows]
            eye = jnp.eye(reps, dtype=F32)
            return jnp.einsum("ab,gij->gaibj", eye, blk).reshape(G_A, GMLP_CHUNK, GMLP_CHUNK).astype(BF16)

        def gmlp_bias(rows, bias_rows=bias_rows):
            return jnp.tile(bias_rows[:rows], (GMLP_CHUNK // rows, 1))

        w = dict(
            norm_attn_g=norm_attn_g[l], gmlp_norm_g=gmlp_norm_g[l], norm_ffn_g=norm_ffn_g[l],
            gmlp_wm=gmlp_wm, gmlp_bias=gmlp_bias,
            a=w_branch_a[l], b=w_branch_b[l], o=w_out[l], up=w_up[l], conv_w=conv_w[l], conv_b=conv_b[l],
            down=w_down[l].astype(BF16),
        )
        w["in_t"] = jnp.swapaxes(w_in[l], 0, 1).astype(BF16)
        w["in_gates_t"] = jnp.swapaxes(w_in[l], 0, 1)[gate_col0:].astype(BF16)

        xp, kp, vp, kip, cp, _ = _layer(xp, pos_p, s, w, None, jnp.zeros((bsz, CONV_WIDTH - 1, f2), F32),
                                        False)
        past = (l, cache_kidx, cache_k4, cache_v4)
        xs, ks, vs, kis, cs, gv = _layer(xs, pos_s, t, w, past, state_ffn_conv[l], True)
        for lst, val in zip(outs, (kp, vp, kip, cp, ks, vs, kis, cs, gv.reshape(dbsz, t, da))):
            lst.append(val)

    y_prompt = rmsnorm(xp.reshape(bsz * s, d), norm_final_g, F32).reshape(bsz, s, d)
    y_sample = rmsnorm(xs.reshape(dbsz * t, d), norm_final_g, F32).reshape(dbsz, t, d)
    stack = lambda o: o[0][None] if depth == 1 else jnp.stack(o)
    return (y_prompt, y_sample) + tuple(stack(o) for o in outs)
```

```python
import functools
import math

import jax
import jax.numpy as jnp
from jax import lax
from jax.experimental import pallas as pl
from jax.experimental.pallas import tpu as pltpu

CHUNK = 64
GMLP_CHUNK = 128
G_A = 8
HEAD_DIM = 128
Q_PER_KV = 4
N_IDX_HEADS = 16
IDX_DIM = 64
TOPK_MAX = 256
ROPE_THETA = 500000.0
CONV_WIDTH = 3
EPS = 1e-6

LANE = 128
SUBLANE = 8
VMEM_LIMIT_BYTES = 56 * 1024 * 1024

FFN_UP_TILE_ROWS = 2048
FFN_UP_SUB_ROWS = 256
FFN_UP_VMEM_LIMIT_BYTES = 60 * 1024 * 1024
DSA_PREFIX_CLASSES = 4
GMLP_CHUNKS_PER_STEP = 4
DSA_QUERY_BLOCK = 128
TOPK_STEP_SURPLUS = 2.0
TOPK_PROBES_PER_CHECK = 3

BF16 = jnp.bfloat16
F32 = jnp.float32
NT_DIMS = (((1,), (1,)), ((), ()))


def _params(*semantics, vmem_limit_bytes=VMEM_LIMIT_BYTES):
    return pltpu.CompilerParams(dimension_semantics=semantics, vmem_limit_bytes=vmem_limit_bytes)


def _tile(n, pref, align):
    if n <= pref:
        return n
    t = (pref // align) * align
    while t >= align:
        if n % t == 0:
            return t
        t -= align
    raise ValueError(f"no {align}-aligned tile of {n} below {pref}")


def _first_row_tile():
    return pl.program_id(1) == 0


def _sigmoid(x):
    return 0.5 * jnp.tanh(0.5 * x) + 0.5


def _rmsnorm_kernel(x_ref, g_ref, o_ref):
    x = x_ref[...]
    ms = jnp.mean(x * x, axis=-1, keepdims=True)
    o_ref[...] = (x * lax.rsqrt(ms + EPS) * g_ref[...]).astype(o_ref.dtype)


def rmsnorm(x, g, out_dtype):
    n, d = x.shape
    tr = _tile(n, 512, SUBLANE)
    return pl.pallas_call(
        _rmsnorm_kernel,
        grid=(n // tr,),
        in_specs=[pl.BlockSpec((tr, d), lambda i: (i, 0)), pl.BlockSpec((1, d), lambda i: (0, 0))],
        out_specs=pl.BlockSpec((tr, d), lambda i: (i, 0)),
        out_shape=jax.ShapeDtypeStruct((n, d), out_dtype),
        compiler_params=_params("parallel"),
        name="rmsnorm",
    )(x, g.reshape(1, d))


def _rope_slab(x, c, s1, s2, shift):
    return x * c + pltpu.roll(x, LANE - shift, 1) * s1 + pltpu.roll(x, shift, 1) * s2


def _proj_kernel(a_ref, b_ref, *rest, rope_shift):
    acc = lax.dot_general(a_ref[...], b_ref[...], NT_DIMS, preferred_element_type=F32)
    if rope_shift:
        c_ref, s1_ref, s2_ref, o_ref = rest
        c, s1, s2 = c_ref[...], s1_ref[...], s2_ref[...]
        for h in range(acc.shape[1] // LANE):
            sl = slice(h * LANE, (h + 1) * LANE)
            o_ref[:, sl] = _rope_slab(acc[:, sl], c, s1, s2, rope_shift).astype(o_ref.dtype)
    else:
        (o_ref,) = rest
        o_ref[...] = acc.astype(o_ref.dtype)


def project(a, wt, col0, ncols, out_dtype, rope=None):
    m, k = a.shape
    tm = _tile(m, 1024, SUBLANE)
    tn = _tile(math.gcd(ncols, col0), 1024, LANE)
    assert col0 % tn == 0 and ncols % tn == 0
    j0 = col0 // tn
    in_specs = [pl.BlockSpec((tm, k), lambda j, i: (i, 0)), pl.BlockSpec((tn, k), lambda j, i: (j + j0, 0))]
    args = [a, wt]
    shift = 0
    if rope is not None:
        shift, tabs = rope[0], rope[1:]
        reps = tabs[0].shape[0] // tm
        assert reps * tm == tabs[0].shape[0]
        for t in tabs:
            in_specs.append(pl.BlockSpec((tm, LANE), lambda j, i: (i % reps, 0)))
            args.append(t)
    return pl.pallas_call(
        functools.partial(_proj_kernel, rope_shift=shift),
        grid=(ncols // tn, m // tm),
        in_specs=in_specs,
        out_specs=pl.BlockSpec((tm, tn), lambda j, i: (i, j)),
        out_shape=jax.ShapeDtypeStruct((m, ncols), out_dtype),
        compiler_params=_params("parallel", "arbitrary"),
        name="project",
    )(*args)


def _proj_kv_kernel(a_ref, b_ref, ck_ref, s1k_ref, s2k_ref, cw_ref, s1w_ref, s2w_ref,
                    k_ref, v_ref, kw_ref, *, shift_k, shift_w):
    g = pl.program_id(1)
    tm = a_ref.shape[0]
    n_heads = b_ref.shape[0] // LANE

    def matmul(rows):
        return lax.dot_general(a_ref[...], b_ref[0:rows, :], NT_DIMS, preferred_element_type=F32)

    @pl.when(g == 0)
    def _():
        acc = matmul(n_heads * LANE)
        c, s1, s2 = ck_ref[...], s1k_ref[...], s2k_ref[...]
        for h in range(n_heads):
            k_ref[pl.ds(h, tm, stride=n_heads), :] = _rope_slab(acc[:, h * LANE:(h + 1) * LANE], c, s1, s2, shift_k)

    @pl.when(g == 1)
    def _():
        acc = matmul(n_heads * LANE)
        for h in range(n_heads):
            v_ref[pl.ds(h, tm, stride=n_heads), :] = acc[:, h * LANE:(h + 1) * LANE]

    @pl.when(g == 2)
    def _():
        kw_ref[...] = _rope_slab(matmul(LANE), cw_ref[...], s1w_ref[...], s2w_ref[...], shift_w)


def project_kv(a, wt, col_k, col_v, col_kw, width, rope_k, rope_kw):
    m, kdim = a.shape
    tm = _tile(m, 1024, SUBLANE)
    assert col_k % width == 0 and col_v % width == 0 and col_kw % width == 0
    assert col_kw + width <= wt.shape[0] and width % LANE == 0
    blocks = (col_k // width, col_v // width, col_kw // width)
    heads = width // LANE

    def w_map(i, g):
        return (jnp.where(g == 0, blocks[0], jnp.where(g == 1, blocks[1], blocks[2])), 0)

    tabs = rope_k[1:] + rope_kw[1:]
    reps = tabs[0].shape[0] // tm
    assert reps * tm == tabs[0].shape[0]
    return pl.pallas_call(
        functools.partial(_proj_kv_kernel, shift_k=rope_k[0], shift_w=rope_kw[0]),
        grid=(m // tm, 3),
        in_specs=[pl.BlockSpec((tm, kdim), lambda i, g: (i, 0)), pl.BlockSpec((width, kdim), w_map)]
        + [pl.BlockSpec((tm, LANE), lambda i, g: (i % reps, 0)) for _ in tabs],
        out_specs=[
            pl.BlockSpec((tm * heads, LANE), lambda i, g: (i, 0)),
            pl.BlockSpec((tm * heads, LANE), lambda i, g: (i, 0)),
            pl.BlockSpec((tm, LANE), lambda i, g: (i, 0)),
        ],
        out_shape=[
            jax.ShapeDtypeStruct((m * heads, LANE), F32),
            jax.ShapeDtypeStruct((m * heads, LANE), F32),
            jax.ShapeDtypeStruct((m, LANE), F32),
        ],
        compiler_params=_params("parallel", "arbitrary"),
        name="project_kv",
    )(a, wt, *tabs)


def _rope_tables(pos, head_dim, rot_dim, rows):
    half = rot_dim // 2
    inv_freq = ROPE_THETA ** (-jnp.arange(half, dtype=F32) / half)
    ang = pos.astype(F32)[:, None] * inv_freq[None, :]
    cos, sin = jnp.cos(ang), jnp.sin(ang)
    t = pos.shape[0]
    zeros_h = jnp.zeros((t, half), F32)
    rest0 = jnp.zeros((t, head_dim - rot_dim), F32)
    c = jnp.concatenate([cos, cos, jnp.ones((t, head_dim - rot_dim), F32)], axis=1)
    s1 = jnp.concatenate([-sin, zeros_h, rest0], axis=1)
    s2 = jnp.concatenate([zeros_h, sin, rest0], axis=1)
    reps_l = LANE // head_dim
    reps_r = rows // t
    return tuple(jnp.tile(x, (reps_r, reps_l)) for x in (c, s1, s2))


def _gmlp_kernel(u_ref, v_ref, g_ref, wm_ref, b_ref, a_ref, *vn_out, groups):
    gw = v_ref.shape[1] // groups
    for c in range(v_ref.shape[0] // GMLP_CHUNK):
        rows = slice(c * GMLP_CHUNK, (c + 1) * GMLP_CHUNK)
        v = v_ref[rows, :]
        vn = v * lax.rsqrt(jnp.mean(v * v, axis=-1, keepdims=True) + EPS) * g_ref[...]
        if vn_out:
            vn_out[0][rows, :] = vn
        vb = vn.astype(BF16)
        for g in range(groups):
            sl = slice(g * gw, (g + 1) * gw)
            s = jnp.dot(wm_ref[g], vb[:, sl], preferred_element_type=F32) + b_ref[:, sl]
            a_ref[rows, sl] = (u_ref[rows, sl] * s).astype(a_ref.dtype)


def gmlp(uv, g_norm, wm, bias, want_vn):
    m, d2 = uv.shape
    da = d2 // 2
    tc = GMLP_CHUNK * math.gcd(m // GMLP_CHUNK, GMLP_CHUNKS_PER_STEP)
    out_shape = [jax.ShapeDtypeStruct((m, da), BF16)]
    out_specs = [pl.BlockSpec((tc, da), lambda i: (i, 0))]
    if want_vn:
        out_shape.append(jax.ShapeDtypeStruct((m, da), F32))
        out_specs.append(pl.BlockSpec((tc, da), lambda i: (i, 0)))
    res = pl.pallas_call(
        functools.partial(_gmlp_kernel, groups=wm.shape[0]),
        grid=(m // tc,),
        in_specs=[
            pl.BlockSpec((tc, da), lambda i: (i, 0)),
            pl.BlockSpec((tc, da), lambda i: (i, 1)),
            pl.BlockSpec((1, da), lambda i: (0, 0)),
            pl.BlockSpec(wm.shape, lambda i: (0, 0, 0)),
            pl.BlockSpec((GMLP_CHUNK, da), lambda i: (0, 0)),
        ],
        out_specs=out_specs,
        out_shape=out_shape,
        compiler_params=_params("parallel"),
        name="gmlp",
    )(uv, uv, g_norm.reshape(1, da), wm, bias)
    return res if want_vn else (res[0], None)


LOG2_E = 1.4426950408889634
ROW_REDUCE_GROUP = 64
ATTN_ROW_GROUP = 32


def _reduce_rows(x, op):
    pair = {jnp.sum: jnp.add, jnp.min: jnp.minimum, jnp.max: jnp.maximum}[op]
    rows = x.shape[0]
    if rows % ROW_REDUCE_GROUP == 0:
        parts = [x[i:i + ROW_REDUCE_GROUP] for i in range(0, rows, ROW_REDUCE_GROUP)]
        while len(parts) > 1:
            parts = [pair(parts[i], parts[i + 1]) if i + 1 < len(parts) else parts[i]
                     for i in range(0, len(parts), 2)]
        x = parts[0]
    return op(x, axis=0, keepdims=True)

def _dsa_select_attend(qi_ref, kwq_ref, q_ref, klo_ref, khi_ref, ks_ref, vt_ref, o_ref, bias_ref,
                       lg_ref, p_ref, *,
                       s_c, tq, n_kv, topk, row0, n_valid, scale):
    n_tiles = s_c // LANE
    reps = LANE // tq

    def rep_rows(x):
        return x if reps == 1 else jnp.concatenate([x] * reps, axis=0)

    w_t = jnp.transpose(rep_rows(kwq_ref[...]))
    klo = klo_ref[0:s_c, :]
    khi = khi_ref[0:s_c, :]

    score = jnp.zeros((s_c, LANE), F32)
    for pp in range(N_IDX_HEADS // 4):
        qp2 = jnp.concatenate([rep_rows(qi_ref[:, (2 * pp + i) * LANE:(2 * pp + i + 1) * LANE])
                               for i in range(2)], axis=0)
        rel_lo = lax.dot_general(klo, qp2, NT_DIMS, preferred_element_type=F32)
        rel_hi = lax.dot_general(khi, qp2, NT_DIMS, preferred_element_type=F32)
        for i in range(2):
            r0 = IDX_DIM + 2 * (2 * pp + i)
            score = (score + jnp.maximum(rel_lo[:, i * LANE:(i + 1) * LANE], 0.0) * w_t[r0:r0 + 1, :]
                     + jnp.maximum(rel_hi[:, i * LANE:(i + 1) * LANE], 0.0) * w_t[r0 + 1:r0 + 2, :])

    kpos = lax.broadcasted_iota(jnp.int32, (s_c, LANE), 0)
    if row0 is not None:
        qrow = row0 + jnp.bitwise_and(lax.broadcasted_iota(jnp.int32, (s_c, LANE), 1), tq - 1)
        adm = kpos < (jnp.right_shift(qrow, CHUNK.bit_length() - 1) + 1) * CHUNK
    else:
        adm = kpos < n_valid
    score = jnp.where(adm, score, -jnp.inf)
    bias_ref[0:s_c, :] = score

    kf = float(topk)
    lo0 = _reduce_rows(jnp.where(adm, score, jnp.inf), jnp.min)
    hi0 = _reduce_rows(score, jnp.max)
    c_lo0 = _reduce_rows(jnp.where(adm, 1.0, 0.0), jnp.sum)
    c_hi0 = jnp.zeros((1, LANE), F32)

    def probe(t):
        cnt = jnp.zeros((ROW_REDUCE_GROUP, LANE), F32)
        nxt = jnp.full((ROW_REDUCE_GROUP, LANE), jnp.inf, F32)
        for i in range(0, s_c, ROW_REDUCE_GROUP):
            sc = bias_ref[i:i + ROW_REDUCE_GROUP, :]
            above = sc > t
            cnt = cnt + jnp.where(above, 1.0, 0.0)
            nxt = jnp.minimum(nxt, jnp.where(above, sc, jnp.inf))
        return jnp.sum(cnt, axis=0, keepdims=True), jnp.min(nxt, axis=0, keepdims=True)

    def active_of(lo, hi, c_lo):
        return (c_lo > kf) & (hi > lo)

    def cond(carry):
        lo, hi, c_lo, _ = carry
        return jnp.max(jnp.where(active_of(lo, hi, c_lo), 1.0, 0.0)) > 0.0

    def body(carry):
        for _ in range(TOPK_PROBES_PER_CHECK):
            carry = advance(carry)
        return carry

    def advance(carry):
        lo, hi, c_lo, c_hi = carry
        act = active_of(lo, hi, c_lo)
        mid = 0.5 * lo + 0.5 * hi
        step = (c_lo - kf <= TOPK_STEP_SURPLUS) | (mid <= lo) | (mid >= hi)
        t = jnp.where(step, lo, mid)
        cnt, nxt = probe(t)
        take = act & (cnt >= kf)
        drop = act & (cnt < kf)
        return (jnp.where(take, nxt, lo), jnp.where(drop, t, hi),
                jnp.where(take, cnt, c_lo), jnp.where(drop, cnt, c_hi))

    lo, hi, c_lo, c_hi = lax.while_loop(cond, body, (lo0, hi0, c_lo0, c_hi0))
    tie = c_lo > kf
    any_tie = jnp.max(jnp.where(tie, 1.0, 0.0)) > 0.0

    @pl.when(jnp.logical_not(any_tie))
    def _():
        for jt in range(n_tiles):
            sl = slice(jt * LANE, (jt + 1) * LANE)
            bias_ref[sl, :] = jnp.where(bias_ref[sl, :] >= lo, 0.0, -jnp.inf)

    @pl.when(any_tie)
    def _():
        quota = kf - c_hi
        tri = (lax.broadcasted_iota(jnp.int32, (LANE, LANE), 1)
               <= lax.broadcasted_iota(jnp.int32, (LANE, LANE), 0)).astype(F32).astype(BF16)
        before = jnp.zeros((1, LANE), F32)
        for jt in range(n_tiles):
            sl = slice(jt * LANE, (jt + 1) * LANE)
            sc = bias_ref[sl, :]
            cand = jnp.where((sc >= lo) & (sc <= hi), 1.0, 0.0)
            rank = jnp.dot(tri, cand.astype(BF16), preferred_element_type=F32) + before
            keep = (sc > hi) | ((cand > 0.0) & (rank <= quota))
            bias_ref[sl, :] = jnp.where(tie, jnp.where(keep, 0.0, -jnp.inf),
                                        jnp.where(sc >= lo, 0.0, -jnp.inf))
            before = before + jnp.sum(cand, axis=0, keepdims=True)

    width = Q_PER_KV * tq
    n_slabs = width // LANE
    grp = ATTN_ROW_GROUP
    def put_logits(n):
        qn = jnp.concatenate(
            [q_ref[:, (n * Q_PER_KV + g) * HEAD_DIM:(n * Q_PER_KV + g + 1) * HEAD_DIM]
             for g in range(Q_PER_KV)], axis=0)
        lg_ref[n % 2, 0:s_c, :] = lax.dot_general(
            ks_ref[0:s_c, n * HEAD_DIM:(n + 1) * HEAD_DIM], qn, NT_DIMS,
            preferred_element_type=F32) * (scale * LOG2_E)

    put_logits(0)
    for n in range(n_kv):
        if n + 1 < n_kv:
            put_logits(n + 1)
        ksl = slice(n * HEAD_DIM, (n + 1) * HEAD_DIM)
        lgn = lg_ref.at[n % 2]
        pn = p_ref.at[n % 2]
        mpart = jnp.full((grp, width), -jnp.inf, F32)
        for i in range(0, s_c, grp):
            b = bias_ref[i:i + grp, :]
            lg = lgn[i:i + grp, :] + (b if n_slabs == 1 else jnp.concatenate([b] * n_slabs, axis=1))
            lgn[i:i + grp, :] = lg
            mpart = jnp.maximum(mpart, lg)
        mx = jnp.max(mpart, axis=0, keepdims=True)
        dpart = jnp.zeros((grp, width), F32)
        for i in range(0, s_c, grp):
            p = jnp.exp2(lgn[i:i + grp, :] - mx)
            dpart = dpart + p
            pn[i:i + grp, :] = p.astype(BF16)
        den = jnp.sum(dpart, axis=0, keepdims=True)
        o_t = jnp.dot(vt_ref[ksl, 0:s_c], pn[0:s_c, :], preferred_element_type=F32) / den
        for sb in range(n_slabs):
            o = jnp.transpose(o_t[:, sb * LANE:(sb + 1) * LANE])
            for r in range(reps):
                g = sb * reps + r
                hsl = slice((n * Q_PER_KV + g) * HEAD_DIM, (n * Q_PER_KV + g + 1) * HEAD_DIM)
                o_ref[:, hsl] = o[r * tq:(r + 1) * tq].astype(o_ref.dtype)


def _dsa_kernel(qi_ref, kwq_ref, q_ref, kw_ref, k_ref, v_ref, *rest, tq, t_new, p_len, n_classes, topk):
    if p_len:
        (pki_ref, pk_ref, pv_ref, o_ref, klo_ref, khi_ref, ks_ref, vt_ref, bias_ref, lg_ref, p_ref,
         tmp_ref) = rest
    else:
        o_ref, klo_ref, khi_ref, ks_ref, vt_ref, bias_ref, lg_ref, p_ref = rest
        pk_ref = pv_ref = None
    s_pad = ks_ref.shape[0]
    n_valid = p_len + t_new
    n_kv = ks_ref.shape[1] // HEAD_DIM
    j = pl.program_id(1)

    def head_rows(src_ref, rows, n):
        return src_ref[pl.ds(n, rows, stride=n_kv), :]

    @pl.when(j == 0)
    def _():
        lane = lax.broadcasted_iota(jnp.int32, (t_new, LANE), 1)
        new_lo = jnp.where(lane < IDX_DIM, kw_ref[...], 0.0)
        if p_len:
            tmp_ref[...] = jnp.zeros(tmp_ref.shape, F32)
            tmp_ref[0:p_len, 0:IDX_DIM] = pki_ref[...]
            tmp_ref[p_len:n_valid, :] = new_lo
            lo = tmp_ref[...]
        else:
            lo = new_lo
        klo_ref[...] = lo.astype(BF16)
        khi_ref[...] = pltpu.roll(lo, IDX_DIM, 1).astype(BF16)
        tail = jnp.zeros((s_pad - n_valid, HEAD_DIM), F32)
        for n in range(n_kv):
            csl = slice(n * HEAD_DIM, (n + 1) * HEAD_DIM)
            new_k = head_rows(k_ref, t_new, n)
            new_v = head_rows(v_ref, t_new, n)
            if s_pad > n_valid:
                new_k = jnp.concatenate([new_k, tail], axis=0)
                new_v = jnp.concatenate([new_v, tail], axis=0)
            if p_len:
                ks_ref[0:p_len, csl] = head_rows(pk_ref, p_len, n).astype(BF16)
                vt_ref[csl, 0:p_len] = jnp.transpose(head_rows(pv_ref, p_len, n)).astype(BF16)
            ks_ref[p_len:s_pad, csl] = new_k.astype(BF16)
            vt_ref[csl, p_len:s_pad] = jnp.transpose(new_v).astype(BF16)

    common = dict(tq=tq, n_kv=n_kv, topk=topk, n_valid=n_valid, scale=HEAD_DIM ** -0.5)
    refs = (qi_ref, kwq_ref, q_ref, klo_ref, khi_ref, ks_ref, vt_ref, o_ref, bias_ref, lg_ref, p_ref)
    if p_len:
        _dsa_select_attend(*refs, s_c=s_pad, row0=None, **common)
    else:
        per_class = pl.num_programs(1) // n_classes
        for c in range(n_classes):
            @pl.when(j // per_class == c)
            def _(c=c):
                _dsa_select_attend(*refs, s_c=(c + 1) * (s_pad // n_classes), row0=j * tq, **common)


def dsa(qi, kw, q, k, v, past, *, tq, topk):
    b, t, dq = q.shape
    n_kv = k.shape[1] // t
    dkv = n_kv * HEAD_DIM
    nq = t // tq
    row = lambda i, j: (i, j, 0)
    full = lambda i, j: (i, 0, 0)
    in_specs = [
        pl.BlockSpec((None, tq, qi.shape[2]), row),
        pl.BlockSpec((None, tq, LANE), row),
        pl.BlockSpec((None, tq, dq), row),
        pl.BlockSpec((None, t, LANE), full),
        pl.BlockSpec((None, t * n_kv, HEAD_DIM), full),
        pl.BlockSpec((None, t * n_kv, HEAD_DIM), full),
    ]
    args = [qi, kw, q, kw, k, v]
    if past is None:
        p_len = 0
        s_pad = t
        n_classes = DSA_PREFIX_CLASSES if (nq % DSA_PREFIX_CLASSES == 0
                                           and t % (DSA_PREFIX_CLASSES * LANE) == 0) else 1
        assert tq % CHUNK == 0 and LANE % tq == 0
    else:
        layer, pki, pk, pv = past
        p_len = pki.shape[2]
        s_pad = -(-(p_len + t) // LANE) * LANE
        n_classes = 1
        assert nq == 1 and p_len % LANE == 0 and LANE % tq == 0 and Q_PER_KV * tq % LANE == 0
        cache = lambda i, j: (layer, i, 0, 0)
        in_specs += [pl.BlockSpec((None, None, p_len, IDX_DIM), cache),
                     pl.BlockSpec((None, None, p_len * n_kv, HEAD_DIM), cache),
                     pl.BlockSpec((None, None, p_len * n_kv, HEAD_DIM), cache)]
        args += [pki, pk, pv]
    scratch = [pltpu.VMEM((s_pad, LANE), BF16), pltpu.VMEM((s_pad, LANE), BF16),
               pltpu.VMEM((s_pad, dkv), BF16), pltpu.VMEM((dkv, s_pad), BF16),
               pltpu.VMEM((s_pad, LANE), F32),
               pltpu.VMEM((2, s_pad, Q_PER_KV * tq), F32), pltpu.VMEM((2, s_pad, Q_PER_KV * tq), BF16)]
    if past is not None:
        scratch.append(pltpu.VMEM((s_pad, LANE), F32))
    kern = functools.partial(_dsa_kernel, tq=tq, t_new=t, p_len=p_len, n_classes=n_classes, topk=topk)
    return pl.pallas_call(
        kern,
        grid=(b, nq),
        in_specs=in_specs,
        out_specs=pl.BlockSpec((None, tq, dq), row),
        out_shape=jax.ShapeDtypeStruct((b, t, dq), BF16),
        scratch_shapes=scratch,
        compiler_params=_params("arbitrary", "arbitrary"),
        name="dsa",
    )(*args)


def _merge_kernel(a_ref, b_ref, wa_ref, wb_ref, ga_ref, gb_ref, o_ref, waq_ref, wbq_ref):
    @pl.when(_first_row_tile())
    def _():
        waq_ref[...] = wa_ref[...].astype(BF16)
        wbq_ref[...] = wb_ref[...].astype(BF16)

    ya = jnp.dot(a_ref[...], waq_ref[...], preferred_element_type=F32)
    yb = jnp.dot(b_ref[...], wbq_ref[...], preferred_element_type=F32)
    y = _sigmoid(ga_ref[...]) * ya + _sigmoid(gb_ref[...]) * yb
    o_ref[...] = y.astype(o_ref.dtype)


def merge(a, b, wa, wb, gates):
    m, ka = a.shape
    kb = b.shape[1]
    d = wa.shape[1]
    tm = _tile(m, 1024, SUBLANE)
    tn = _tile(d, 512, LANE)
    nj = d // tn
    return pl.pallas_call(
        _merge_kernel,
        grid=(nj, m // tm),
        in_specs=[
            pl.BlockSpec((tm, ka), lambda j, i: (i, 0)),
            pl.BlockSpec((tm, kb), lambda j, i: (i, 0)),
            pl.BlockSpec((ka, tn), lambda j, i: (0, j)),
            pl.BlockSpec((kb, tn), lambda j, i: (0, j)),
            pl.BlockSpec((tm, tn), lambda j, i: (i, j)),
            pl.BlockSpec((tm, tn), lambda j, i: (i, j + nj)),
        ],
        out_specs=pl.BlockSpec((tm, tn), lambda j, i: (i, j)),
        out_shape=jax.ShapeDtypeStruct((m, d), BF16),
        scratch_shapes=[pltpu.VMEM((ka, tn), BF16), pltpu.VMEM((kb, tn), BF16)],
        compiler_params=_params("arbitrary", "arbitrary"),
        name="merge",
    )(a, b, wa, wb, gates, gates)


def _matmul_res_kernel(a_ref, b_ref, r_ref, o_ref, bq_ref):
    @pl.when(_first_row_tile())
    def _():
        bq_ref[...] = b_ref[...].astype(BF16)

    o_ref[...] = r_ref[...] + jnp.dot(a_ref[...], bq_ref[...], preferred_element_type=F32)


def matmul_residual(a, b, res):
    m, k = a.shape
    n = b.shape[1]
    tm = _tile(m, 1024, SUBLANE)
    tn = _tile(n, 512, LANE)
    return pl.pallas_call(
        _matmul_res_kernel,
        grid=(n // tn, m // tm),
        in_specs=[
            pl.BlockSpec((tm, k), lambda j, i: (i, 0)),
            pl.BlockSpec((k, tn), lambda j, i: (0, j)),
            pl.BlockSpec((tm, tn), lambda j, i: (i, j)),
        ],
        out_specs=pl.BlockSpec((tm, tn), lambda j, i: (i, j)),
        out_shape=jax.ShapeDtypeStruct((m, n), F32),
        scratch_shapes=[pltpu.VMEM((k, tn), BF16)],
        compiler_params=_params("arbitrary", "arbitrary"),
        name="matmul_residual",
    )(a, b, res)


def _ffn_up_kernel(h_ref, wg_ref, wu_ref, cwg_ref, cwu_ref, cbg_ref, cbu_ref, sg_ref, su_ref,
                   act_ref, zg_ref, zu_ref, wq_ref, buf_ref, *, seq_len, tiles_per_seq, n_sub):
    tm = h_ref.shape[0]
    tn = act_ref.shape[1]
    pad = SUBLANE

    @pl.when(_first_row_tile())
    def _():
        wq_ref[:, 0:tn] = wg_ref[...].astype(BF16)
        wq_ref[:, tn:2 * tn] = wu_ref[...].astype(BF16)

    cw = jnp.concatenate([cwg_ref[...], cwu_ref[...]], axis=1)
    cb = jnp.concatenate([cbg_ref[...], cbu_ref[...]], axis=1)

    def gated(c):
        gate, up = c[:, :tn], c[:, tn:]
        return (gate * _sigmoid(gate) * up).astype(act_ref.dtype)

    if tiles_per_seq >= 1:
        @pl.when(pl.program_id(1) % tiles_per_seq == 0)
        def _():
            buf_ref[pad - 2:pad, 0:tn] = sg_ref[0]
            buf_ref[pad - 2:pad, tn:2 * tn] = su_ref[0]
        rs = tm // n_sub
        zs = [jnp.dot(h_ref[r * rs:(r + 1) * rs, :], wq_ref[...], preferred_element_type=F32)
              for r in range(n_sub)]
        for r, z in enumerate(zs):
            base = pad + r * rs
            buf_ref[base:base + rs, :] = z
            zext = buf_ref[base - pad:base + rs, :]
            z1 = pltpu.roll(zext, 1, 0)[pad:]
            z2 = pltpu.roll(zext, 2, 0)[pad:]
            c = cb + cw[0:1] * z2 + cw[1:2] * z1 + cw[2:3] * z
            act_ref[r * rs:(r + 1) * rs, :] = gated(c)
        tail = buf_ref[pad + tm - 2:pad + tm, :]
        zg_ref[0] = tail[:, :tn]
        zu_ref[0] = tail[:, tn:]
        buf_ref[pad - 2:pad, :] = tail
    else:
        buf_ref[0:pad, :] = jnp.zeros((pad, 2 * tn), F32)
        buf_ref[pad:pad + tm, :] = jnp.dot(h_ref[...], wq_ref[...], preferred_element_type=F32)
        rowi = lax.broadcasted_iota(jnp.int32, (seq_len, 2 * tn), 0)
        for s in range(tm // seq_len):
            base = pad + s * seq_len
            st0 = jnp.concatenate([sg_ref[s, 0:1, :], su_ref[s, 0:1, :]], axis=1)
            st1 = jnp.concatenate([sg_ref[s, 1:2, :], su_ref[s, 1:2, :]], axis=1)
            z0 = buf_ref[base:base + seq_len, :]
            p1 = jnp.where(rowi == 0, st1, buf_ref[base - 1:base - 1 + seq_len, :])
            p2 = jnp.where(rowi == 0, st0, jnp.where(rowi == 1, st1, buf_ref[base - 2:base - 2 + seq_len, :]))
            act_ref[s * seq_len:(s + 1) * seq_len, :] = gated(cb + cw[0:1] * p2 + cw[1:2] * p1 + cw[2:3] * z0)
            tail = buf_ref[base + seq_len - 2:base + seq_len, :]
            zg_ref[s] = tail[:, :tn]
            zu_ref[s] = tail[:, tn:]


def ffn_up(h, w_up, conv_w, conv_b, state, seq_len):
    m, d = h.shape
    f = w_up.shape[1] // 2
    tm = _tile(m, FFN_UP_TILE_ROWS, SUBLANE)
    tn = _tile(f, 256, LANE)
    nj = f // tn
    if seq_len >= tm:
        assert seq_len % tm == 0
        tiles_per_seq = seq_len // tm
        n_state = 1
        state_map_g = lambda j, i: (i // tiles_per_seq, 0, j)
        state_map_u = lambda j, i: (i // tiles_per_seq, 0, j + nj)
        n_last = m // tm
    else:
        assert tm % seq_len == 0 and seq_len % SUBLANE == 0
        tiles_per_seq = 0
        n_state = tm // seq_len
        state_map_g = lambda j, i: (i, 0, j)
        state_map_u = lambda j, i: (i, 0, j + nj)
        n_last = m // seq_len
    kern = functools.partial(_ffn_up_kernel, seq_len=seq_len, tiles_per_seq=tiles_per_seq,
                             n_sub=max(1, tm // FFN_UP_SUB_ROWS))
    cb = conv_b.reshape(1, 2 * f)
    act, zg, zu = pl.pallas_call(
        kern,
        grid=(nj, m // tm),
        in_specs=[
            pl.BlockSpec((tm, d), lambda j, i: (i, 0)),
            pl.BlockSpec((d, tn), lambda j, i: (0, j)),
            pl.BlockSpec((d, tn), lambda j, i: (0, j + nj)),
            pl.BlockSpec((CONV_WIDTH, tn), lambda j, i: (0, j)),
            pl.BlockSpec((CONV_WIDTH, tn), lambda j, i: (0, j + nj)),
            pl.BlockSpec((1, tn), lambda j, i: (0, j)),
            pl.BlockSpec((1, tn), lambda j, i: (0, j + nj)),
            pl.BlockSpec((n_state, 2, tn), state_map_g),
            pl.BlockSpec((n_state, 2, tn), state_map_u),
        ],
        out_specs=[
            pl.BlockSpec((tm, tn), lambda j, i: (i, j)),
            pl.BlockSpec((n_state, 2, tn), lambda j, i: (i, 0, j)),
            pl.BlockSpec((n_state, 2, tn), lambda j, i: (i, 0, j)),
        ],
        out_shape=[
            jax.ShapeDtypeStruct((m, f), BF16),
            jax.ShapeDtypeStruct((n_last, 2, f), F32),
            jax.ShapeDtypeStruct((n_last, 2, f), F32),
        ],
        scratch_shapes=[pltpu.VMEM((d, 2 * tn), BF16), pltpu.VMEM((tm + SUBLANE, 2 * tn), F32)],
        compiler_params=_params("arbitrary", "arbitrary", vmem_limit_bytes=FFN_UP_VMEM_LIMIT_BYTES),
        name="ffn_up",
    )(h, w_up, w_up, conv_w, conv_w, cb, cb, state, state)
    zlast = jnp.concatenate([zg, zu], axis=-1)
    if tiles_per_seq > 1:
        zlast = zlast[tiles_per_seq - 1::tiles_per_seq]
    return act, zlast


def _ffn_down_kernel(a_ref, b_ref, r_ref, o_ref):
    o_ref[...] = r_ref[...] + jnp.dot(a_ref[...], b_ref[...], preferred_element_type=F32)


def ffn_down(a, b, res):
    m, k = a.shape
    n = b.shape[1]
    tm = _tile(m, 512, SUBLANE)
    tn = _tile(n, 512, LANE)
    return pl.pallas_call(
        _ffn_down_kernel,
        grid=(m // tm, n // tn),
        in_specs=[
            pl.BlockSpec((tm, k), lambda i, j: (i, 0)),
            pl.BlockSpec((k, tn), lambda i, j: (0, j)),
            pl.BlockSpec((tm, tn), lambda i, j: (i, j)),
        ],
        out_specs=pl.BlockSpec((tm, tn), lambda i, j: (i, j)),
        out_shape=jax.ShapeDtypeStruct((m, n), F32),
        compiler_params=_params("parallel", "arbitrary"),
        name="ffn_down",
    )(a, b, res)


def _in_offsets(d):
    da = d // 2
    n_heads = d // 256
    n_kv = n_heads // Q_PER_KV
    sizes = (da, da, n_heads * HEAD_DIM, n_kv * HEAD_DIM, n_kv * HEAD_DIM,
             N_IDX_HEADS * IDX_DIM, IDX_DIM, N_IDX_HEADS, d, d)
    offs = [0]
    for s in sizes:
        offs.append(offs[-1] + s)
    return offs


def _layer(x, pos, seq_len, w, past, conv_state, want_vn):
    bsz, t, d = x.shape
    m = bsz * t
    x2 = x.reshape(m, d)
    n_heads = d // 256
    n_kv = n_heads // Q_PER_KV
    offs = _in_offsets(d)
    w_in = w["in_t"]
    assert offs[-1] == w_in.shape[0] and offs[8] - offs[6] <= LANE

    h = rmsnorm(x2, w["norm_attn_g"], BF16)

    rows = max(t, _tile(m, 1024, SUBLANE))
    rope_h = (HEAD_DIM // 8,) + _rope_tables(pos, HEAD_DIM, HEAD_DIM // 4, rows)
    rope_i = (IDX_DIM // 8,) + _rope_tables(pos, IDX_DIM, IDX_DIM // 4, rows)
    ci, s1i, s2i = rope_i[1:]
    lane = jnp.arange(LANE)
    is_w = (lane >= IDX_DIM) & (lane < IDX_DIM + N_IDX_HEADS)
    w_scale = N_IDX_HEADS ** -0.5 * IDX_DIM ** -0.5
    rope_kw = (IDX_DIM // 8,
               jnp.where(is_w, w_scale, jnp.where(lane < IDX_DIM, ci, 1.0)).astype(F32),
               jnp.where(lane < IDX_DIM, s1i, 0.0), jnp.where(lane < IDX_DIM, s2i, 0.0))

    uv = project(h, w_in, offs[0], offs[2] - offs[0], F32)
    q = project(h, w_in, offs[2], offs[3] - offs[2], BF16, rope_h)
    qi = project(h, w_in, offs[5], offs[6] - offs[5], BF16, rope_i)
    k, v, kw = project_kv(h, w_in, offs[3], offs[4], offs[6], offs[4] - offs[3], rope_h, rope_kw)
    gates = project(h, w["in_gates_t"], 0, 2 * d, F32)

    chunk_rows = min(t, GMLP_CHUNK)
    a_out, vn = gmlp(uv, w["gmlp_norm_g"], w["gmlp_wm"](chunk_rows), w["gmlp_bias"](chunk_rows), want_vn)

    n_valid = t if past is None else past[1].shape[2] + t
    b_out = dsa(qi.reshape(bsz, t, -1), kw.reshape(bsz, t, LANE), q.reshape(bsz, t, -1),
                k.reshape(bsz, t * n_kv, HEAD_DIM), v.reshape(bsz, t * n_kv, HEAD_DIM), past,
                tq=min(DSA_QUERY_BLOCK, t), topk=min(TOPK_MAX, n_valid // 4))

    y = merge(a_out, b_out.reshape(m, n_heads * HEAD_DIM), w["a"], w["b"], gates)
    x2 = matmul_residual(y, w["o"], x2)

    hf = rmsnorm(x2, w["norm_ffn_g"], BF16)
    act, zlast = ffn_up(hf, w["up"], w["conv_w"], w["conv_b"], conv_state, seq_len)
    x2 = ffn_down(act, w["down"], x2)
    kidx = kw[:, :IDX_DIM].reshape(bsz, t, IDX_DIM)
    return (x2.reshape(bsz, t, d), k.reshape(bsz, t, n_kv, HEAD_DIM), v.reshape(bsz, t, n_kv, HEAD_DIM),
            kidx, zlast, vn)


def kernel(x_prompt, x_sample, cache_k, cache_v, cache_kidx, state_ffn_conv, norm_attn_g, w_in, gmlp_norm_g, gmlp_ws, gmlp_b, w_branch_a, w_branch_b, w_out, norm_ffn_g, w_up, conv_w, conv_b, w_down, norm_final_g):
    bsz, s, d = x_prompt.shape
    dbsz, t, _ = x_sample.shape
    depth = w_in.shape[0]
    p_len = cache_k.shape[2]
    da = d // 2
    f2 = w_up.shape[2]
    assert s % GMLP_CHUNK == 0 and GMLP_CHUNK % t == 0 and s % CHUNK == 0

    pos_p = jnp.arange(s, dtype=jnp.int32)
    pos_s = p_len + jnp.arange(t, dtype=jnp.int32)
    ci = jnp.arange(GMLP_CHUNK)
    chunk_mask = (ci[None, :] // CHUNK) <= (ci[:, None] // CHUNK)
    cache_k4 = cache_k.reshape(depth, dbsz, -1, HEAD_DIM)
    cache_v4 = cache_v.reshape(depth, dbsz, -1, HEAD_DIM)
    gate_col0 = _in_offsets(d)[8]

    xp, xs = x_prompt, x_sample
    outs = [[] for _ in range(9)]
    for l in range(depth):
        wm_full = jnp.where(chunk_mask[None], gmlp_ws[l], 0.0)
        bias_rows = jnp.repeat(jnp.transpose(gmlp_b[l]), da // G_A, axis=1)

        def gmlp_wm(rows, wm_full=wm_full):
            reps = GMLP_CHUNK // rows
            blk = wm_full[:, :rows, :rows]
            eye = jnp.eye(reps, dtype=F32)
            return jnp.einsum("ab,gij->gaibj", eye, blk).reshape(G_A, GMLP_CHUNK, GMLP_CHUNK).astype(BF16)

        def gmlp_bias(rows, bias_rows=bias_rows):
            return jnp.tile(bias_rows[:rows], (GMLP_CHUNK // rows, 1))

        w = dict(
            norm_attn_g=norm_attn_g[l], gmlp_norm_g=gmlp_norm_g[l], norm_ffn_g=norm_ffn_g[l],
            gmlp_wm=gmlp_wm, gmlp_bias=gmlp_bias,
            a=w_branch_a[l], b=w_branch_b[l], o=w_out[l], up=w_up[l], conv_w=conv_w[l], conv_b=conv_b[l],
            down=w_down[l].astype(BF16),
        )
        w["in_t"] = jnp.swapaxes(w_in[l], 0, 1).astype(BF16)
        w["in_gates_t"] = jnp.swapaxes(w_in[l], 0, 1)[gate_col0:].astype(BF16)

        xp, kp, vp, kip, cp, _ = _layer(xp, pos_p, s, w, None, jnp.zeros((bsz, CONV_WIDTH - 1, f2), F32),
                                        False)
        past = (l, cache_kidx, cache_k4, cache_v4)
        xs, ks, vs, kis, cs, gv = _layer(xs, pos_s, t, w, past, state_ffn_conv[l], True)
        for lst, val in zip(outs, (kp, vp, kip, cp, ks, vs, kis, cs, gv.reshape(dbsz, t, da))):
            lst.append(val)

    y_prompt = rmsnorm(xp.reshape(bsz * s, d), norm_final_g, F32).reshape(bsz, s, d)
    y_sample = rmsnorm(xs.reshape(dbsz * t, d), norm_final_g, F32).reshape(dbsz, t, d)
    stack = lambda o: o[0][None] if depth == 1 else jnp.stack(o)
    return (y_prompt, y_sample) + tuple(stack(o) for o in outs)
```

```python
import functools
import math

import jax
import jax.numpy as jnp
from jax import lax
from jax.experimental import pallas as pl
from jax.experimental.pallas import tpu as pltpu

CHUNK = 64
GMLP_CHUNK = 128
G_A = 8
HEAD_DIM = 128
Q_PER_KV = 4
N_IDX_HEADS = 16
IDX_DIM = 64
TOPK_MAX = 256
ROPE_THETA = 500000.0
CONV_WIDTH = 3
EPS = 1e-6

LANE = 128
SUBLANE = 8
VMEM_LIMIT_BYTES = 56 * 1024 * 1024

FFN_UP_TILE_ROWS = 2048
FFN_UP_SUB_ROWS = 256
FFN_UP_VMEM_LIMIT_BYTES = 60 * 1024 * 1024
DSA_PREFIX_CLASSES = 4
GMLP_CHUNKS_PER_STEP = 4
DSA_QUERY_BLOCK = 128
TOPK_STEP_SURPLUS = 2.0
TOPK_PROBES_PER_CHECK = 3

BF16 = jnp.bfloat16
F32 = jnp.float32
NT_DIMS = (((1,), (1,)), ((), ()))


def _params(*semantics, vmem_limit_bytes=VMEM_LIMIT_BYTES):
    return pltpu.CompilerParams(dimension_semantics=semantics, vmem_limit_bytes=vmem_limit_bytes)


def _tile(n, pref, align):
    if n <= pref:
        return n
    t = (pref // align) * align
    while t >= align:
        if n % t == 0:
            return t
        t -= align
    raise ValueError(f"no {align}-aligned tile of {n} below {pref}")


def _first_row_tile():
    return pl.program_id(1) == 0


def _sigmoid(x):
    return 0.5 * jnp.tanh(0.5 * x) + 0.5


def _rmsnorm_kernel(x_ref, g_ref, o_ref):
    x = x_ref[...]
    ms = jnp.mean(x * x, axis=-1, keepdims=True)
    o_ref[...] = (x * lax.rsqrt(ms + EPS) * g_ref[...]).astype(o_ref.dtype)


def rmsnorm(x, g, out_dtype):
    n, d = x.shape
    tr = _tile(n, 512, SUBLANE)
    return pl.pallas_call(
        _rmsnorm_kernel,
        grid=(n // tr,),
        in_specs=[pl.BlockSpec((tr, d), lambda i: (i, 0)), pl.BlockSpec((1, d), lambda i: (0, 0))],
        out_specs=pl.BlockSpec((tr, d), lambda i: (i, 0)),
        out_shape=jax.ShapeDtypeStruct((n, d), out_dtype),
        compiler_params=_params("parallel"),
        name="rmsnorm",
    )(x, g.reshape(1, d))


def _rope_slab(x, c, s1, s2, shift):
    return x * c + pltpu.roll(x, LANE - shift, 1) * s1 + pltpu.roll(x, shift, 1) * s2


def _proj_kernel(a_ref, b_ref, *rest, rope_shift):
    acc = lax.dot_general(a_ref[...], b_ref[...], NT_DIMS, preferred_element_type=F32)
    if rope_shift:
        c_ref, s1_ref, s2_ref, o_ref = rest
        c, s1, s2 = c_ref[...], s1_ref[...], s2_ref[...]
        for h in range(acc.shape[1] // LANE):
            sl = slice(h * LANE, (h + 1) * LANE)
            o_ref[:, sl] = _rope_slab(acc[:, sl], c, s1, s2, rope_shift).astype(o_ref.dtype)
    else:
        (o_ref,) = rest
        o_ref[...] = acc.astype(o_ref.dtype)


def project(a, wt, col0, ncols, out_dtype, rope=None):
    m, k = a.shape
    tm = _tile(m, 1024, SUBLANE)
    tn = _tile(math.gcd(ncols, col0), 1024, LANE)
    assert col0 % tn == 0 and ncols % tn == 0
    j0 = col0 // tn
    in_specs = [pl.BlockSpec((tm, k), lambda j, i: (i, 0)), pl.BlockSpec((tn, k), lambda j, i: (j + j0, 0))]
    args = [a, wt]
    shift = 0
    if rope is not None:
        shift, tabs = rope[0], rope[1:]
        reps = tabs[0].shape[0] // tm
        assert reps * tm == tabs[0].shape[0]
        for t in tabs:
            in_specs.append(pl.BlockSpec((tm, LANE), lambda j, i: (i % reps, 0)))
            args.append(t)
    return pl.pallas_call(
        functools.partial(_proj_kernel, rope_shift=shift),
        grid=(ncols // tn, m // tm),
        in_specs=in_specs,
        out_specs=pl.BlockSpec((tm, tn), lambda j, i: (i, j)),
        out_shape=jax.ShapeDtypeStruct((m, ncols), out_dtype),
        compiler_params=_params("parallel", "arbitrary"),
        name="project",
    )(*args)


def _proj_kv_kernel(a_ref, b_ref, ck_ref, s1k_ref, s2k_ref, cw_ref, s1w_ref, s2w_ref,
                    k_ref, v_ref, kw_ref, *, shift_k, shift_w):
    g = pl.program_id(1)
    tm = a_ref.shape[0]
    n_heads = b_ref.shape[0] // LANE

    def matmul(rows):
        return lax.dot_general(a_ref[...], b_ref[0:rows, :], NT_DIMS, preferred_element_type=F32)

    @pl.when(g == 0)
    def _():
        acc = matmul(n_heads * LANE)
        c, s1, s2 = ck_ref[...], s1k_ref[...], s2k_ref[...]
        for h in range(n_heads):
            k_ref[pl.ds(h, tm, stride=n_heads), :] = _rope_slab(acc[:, h * LANE:(h + 1) * LANE], c, s1, s2, shift_k)

    @pl.when(g == 1)
    def _():
        acc = matmul(n_heads * LANE)
        for h in range(n_heads):
            v_ref[pl.ds(h, tm, stride=n_heads), :] = acc[:, h * LANE:(h + 1) * LANE]

    @pl.when(g == 2)
    def _():
        kw_ref[...] = _rope_slab(matmul(LANE), cw_ref[...], s1w_ref[...], s2w_ref[...], shift_w)


def project_kv(a, wt, col_k, col_v, col_kw, width, rope_k, rope_kw):
    m, kdim = a.shape
    tm = _tile(m, 1024, SUBLANE)
    assert col_k % width == 0 and col_v % width == 0 and col_kw % width == 0
    assert col_kw + width <= wt.shape[0] and width % LANE == 0
    blocks = (col_k // width, col_v // width, col_kw // width)
    heads = width // LANE

    def w_map(i, g):
        return (jnp.where(g == 0, blocks[0], jnp.where(g == 1, blocks[1], blocks[2])), 0)

    tabs = rope_k[1:] + rope_kw[1:]
    reps = tabs[0].shape[0] // tm
    assert reps * tm == tabs[0].shape[0]
    return pl.pallas_call(
        functools.partial(_proj_kv_kernel, shift_k=rope_k[0], shift_w=rope_kw[0]),
        grid=(m // tm, 3),
        in_specs=[pl.BlockSpec((tm, kdim), lambda i, g: (i, 0)), pl.BlockSpec((width, kdim), w_map)]
        + [pl.BlockSpec((tm, LANE), lambda i, g: (i % reps, 0)) for _ in tabs],
        out_specs=[
            pl.BlockSpec((tm * heads, LANE), lambda i, g: (i, 0)),
            pl.BlockSpec((tm * heads, LANE), lambda i, g: (i, 0)),
            pl.BlockSpec((tm, LANE), lambda i, g: (i, 0)),
        ],
        out_shape=[
            jax.ShapeDtypeStruct((m * heads, LANE), F32),
            jax.ShapeDtypeStruct((m * heads, LANE), F32),
            jax.ShapeDtypeStruct((m, LANE), F32),
        ],
        compiler_params=_params("parallel", "arbitrary"),
        name="project_kv",
    )(a, wt, *tabs)


def _rope_tables(pos, head_dim, rot_dim, rows):
    half = rot_dim // 2
    inv_freq = ROPE_THETA ** (-jnp.arange(half, dtype=F32) / half)
    ang = pos.astype(F32)[:, None] * inv_freq[None, :]
    cos, sin = jnp.cos(ang), jnp.sin(ang)
    t = pos.shape[0]
    zeros_h = jnp.zeros((t, half), F32)
    rest0 = jnp.zeros((t, head_dim - rot_dim), F32)
    c = jnp.concatenate([cos, cos, jnp.ones((t, head_dim - rot_dim), F32)], axis=1)
    s1 = jnp.concatenate([-sin, zeros_h, rest0], axis=1)
    s2 = jnp.concatenate([zeros_h, sin, rest0], axis=1)
    reps_l = LANE // head_dim
    reps_r = rows // t
    return tuple(jnp.tile(x, (reps_r, reps_l)) for x in (c, s1, s2))


def _gmlp_kernel(u_ref, v_ref, g_ref, wm_ref, b_ref, a_ref, *vn_out, groups):
    gw = v_ref.shape[1] // groups
    for c in range(v_ref.shape[0] // GMLP_CHUNK):
        rows = slice(c * GMLP_CHUNK, (c + 1) * GMLP_CHUNK)
        v = v_ref[rows, :]
        vn = v * lax.rsqrt(jnp.mean(v * v, axis=-1, keepdims=True) + EPS) * g_ref[...]
        if vn_out:
            vn_out[0][rows, :] = vn
        vb = vn.astype(BF16)
        for g in range(groups):
            sl = slice(g * gw, (g + 1) * gw)
            s = jnp.dot(wm_ref[g], vb[:, sl], preferred_element_type=F32) + b_ref[:, sl]
            a_ref[rows, sl] = (u_ref[rows, sl] * s).astype(a_ref.dtype)


def gmlp(uv, g_norm, wm, bias, want_vn):
    m, d2 = uv.shape
    da = d2 // 2
    tc = GMLP_CHUNK * math.gcd(m // GMLP_CHUNK, GMLP_CHUNKS_PER_STEP)
    out_shape = [jax.ShapeDtypeStruct((m, da), BF16)]
    out_specs = [pl.BlockSpec((tc, da), lambda i: (i, 0))]
    if want_vn:
        out_shape.append(jax.ShapeDtypeStruct((m, da), F32))
        out_specs.append(pl.BlockSpec((tc, da), lambda i: (i, 0)))
    res = pl.pallas_call(
        functools.partial(_gmlp_kernel, groups=wm.shape[0]),
        grid=(m // tc,),
        in_specs=[
            pl.BlockSpec((tc, da), lambda i: (i, 0)),
            pl.BlockSpec((tc, da), lambda i: (i, 1)),
            pl.BlockSpec((1, da), lambda i: (0, 0)),
            pl.BlockSpec(wm.shape, lambda i: (0, 0, 0)),
            pl.BlockSpec((GMLP_CHUNK, da), lambda i: (0, 0)),
        ],
        out_specs=out_specs,
        out_shape=out_shape,
        compiler_params=_params("parallel"),
        name="gmlp",
    )(uv, uv, g_norm.reshape(1, da), wm, bias)
    return res if want_vn else (res[0], None)


LOG2_E = 1.4426950408889634
ROW_REDUCE_GROUP = 64
ATTN_ROW_GROUP = 32


def _reduce_rows(x, op):
    pair = {jnp.sum: jnp.add, jnp.min: jnp.minimum, jnp.max: jnp.maximum}[op]
    rows = x.shape[0]
    if rows % ROW_REDUCE_GROUP == 0:
        parts = [x[i:i + ROW_REDUCE_GROUP] for i in range(0, rows, ROW_REDUCE_GROUP)]
        while len(parts) > 1:
            parts = [pair(parts[i], parts[i + 1]) if i + 1 < len(parts) else parts[i]
                     for i in range(0, len(parts), 2)]
        x = parts[0]
    return op(x, axis=0, keepdims=True)

def _dsa_select_attend(qi_ref, kwq_ref, q_ref, klo_ref, khi_ref, ks_ref, vt_ref, o_ref, bias_ref,
                       lg_ref, p_ref, *,
                       s_c, tq, n_kv, topk, row0, n_valid, scale):
    n_tiles = s_c // LANE
    reps = LANE // tq

    def rep_rows(x):
        return x if reps == 1 else jnp.concatenate([x] * reps, axis=0)

    w_t = jnp.transpose(rep_rows(kwq_ref[...]))
    klo = klo_ref[0:s_c, :]
    khi = khi_ref[0:s_c, :]

    score = jnp.zeros((s_c, LANE), F32)
    for pp in range(N_IDX_HEADS // 4):
        qp2 = jnp.concatenate([rep_rows(qi_ref[:, (2 * pp + i) * LANE:(2 * pp + i + 1) * LANE])
                               for i in range(2)], axis=0)
        rel_lo = lax.dot_general(klo, qp2, NT_DIMS, preferred_element_type=F32)
        rel_hi = lax.dot_general(khi, qp2, NT_DIMS, preferred_element_type=F32)
        for i in range(2):
            r0 = IDX_DIM + 2 * (2 * pp + i)
            score = (score + jnp.maximum(rel_lo[:, i * LANE:(i + 1) * LANE], 0.0) * w_t[r0:r0 + 1, :]
                     + jnp.maximum(rel_hi[:, i * LANE:(i + 1) * LANE], 0.0) * w_t[r0 + 1:r0 + 2, :])

    kpos = lax.broadcasted_iota(jnp.int32, (s_c, LANE), 0)
    if row0 is not None:
        qrow = row0 + jnp.bitwise_and(lax.broadcasted_iota(jnp.int32, (s_c, LANE), 1), tq - 1)
        adm = kpos < (jnp.right_shift(qrow, CHUNK.bit_length() - 1) + 1) * CHUNK
    else:
        adm = kpos < n_valid
    score = jnp.where(adm, score, -jnp.inf)
    bias_ref[0:s_c, :] = score

    kf = float(topk)
    lo0 = _reduce_rows(jnp.where(adm, score, jnp.inf), jnp.min)
    hi0 = _reduce_rows(score, jnp.max)
    c_lo0 = _reduce_rows(jnp.where(adm, 1.0, 0.0), jnp.sum)
    c_hi0 = jnp.zeros((1, LANE), F32)

    def probe(t):
        cnt = jnp.zeros((ROW_REDUCE_GROUP, LANE), F32)
        nxt = jnp.full((ROW_REDUCE_GROUP, LANE), jnp.inf, F32)
        for i in range(0, s_c, ROW_REDUCE_GROUP):
            sc = bias_ref[i:i + ROW_REDUCE_GROUP, :]
            above = sc > t
            cnt = cnt + jnp.where(above, 1.0, 0.0)
            nxt = jnp.minimum(nxt, jnp.where(above, sc, jnp.inf))
        return jnp.sum(cnt, axis=0, keepdims=True), jnp.min(nxt, axis=0, keepdims=True)

    def active_of(lo, hi, c_lo):
        return (c_lo > kf) & (hi > lo)

    def cond(carry):
        lo, hi, c_lo, _ = carry
        return jnp.max(jnp.where(active_of(lo, hi, c_lo), 1.0, 0.0)) > 0.0

    def body(carry):
        for _ in range(TOPK_PROBES_PER_CHECK):
            carry = advance(carry)
        return carry

    def advance(carry):
        lo, hi, c_lo, c_hi = carry
        act = active_of(lo, hi, c_lo)
        mid = 0.5 * lo + 0.5 * hi
        step = (c_lo - kf <= TOPK_STEP_SURPLUS) | (mid <= lo) | (mid >= hi)
        t = jnp.where(step, lo, mid)
        cnt, nxt = probe(t)
        take = act & (cnt >= kf)
        drop = act & (cnt < kf)
        return (jnp.where(take, nxt, lo), jnp.where(drop, t, hi),
                jnp.where(take, cnt, c_lo), jnp.where(drop, cnt, c_hi))

    lo, hi, c_lo, c_hi = lax.while_loop(cond, body, (lo0, hi0, c_lo0, c_hi0))
    tie = c_lo > kf
    any_tie = jnp.max(jnp.where(tie, 1.0, 0.0)) > 0.0

    @pl.when(jnp.logical_not(any_tie))
    def _():
        for jt in range(n_tiles):
            sl = slice(jt * LANE, (jt + 1) * LANE)
            bias_ref[sl, :] = jnp.where(bias_ref[sl, :] >= lo, 0.0, -jnp.inf)

    @pl.when(any_tie)
    def _():
        quota = kf - c_hi
        tri = (lax.broadcasted_iota(jnp.int32, (LANE, LANE), 1)
               <= lax.broadcasted_iota(jnp.int32, (LANE, LANE), 0)).astype(F32).astype(BF16)
        before = jnp.zeros((1, LANE), F32)
        for jt in range(n_tiles):
            sl = slice(jt * LANE, (jt + 1) * LANE)
            sc = bias_ref[sl, :]
            cand = jnp.where((sc >= lo) & (sc <= hi), 1.0, 0.0)
            rank = jnp.dot(tri, cand.astype(BF16), preferred_element_type=F32) + before
            keep = (sc > hi) | ((cand > 0.0) & (rank <= quota))
            bias_ref[sl, :] = jnp.where(tie, jnp.where(keep, 0.0, -jnp.inf),
                                        jnp.where(sc >= lo, 0.0, -jnp.inf))
            before = before + jnp.sum(cand, axis=0, keepdims=True)

    width = Q_PER_KV * tq
    n_slabs = width // LANE
    grp = ATTN_ROW_GROUP
    def put_logits(n):
        qn = jnp.concatenate(
            [q_ref[:, (n * Q_PER_KV + g) * HEAD_DIM:(n * Q_PER_KV + g + 1) * HEAD_DIM]
             for g in range(Q_PER_KV)], axis=0)
        lg_ref[n % 2, 0:s_c, :] = lax.dot_general(
            ks_ref[0:s_c, n * HEAD_DIM:(n + 1) * HEAD_DIM], qn, NT_DIMS,
            preferred_element_type=F32) * (scale * LOG2_E)

    put_logits(0)
    for n in range(n_kv):
        if n + 1 < n_kv:
            put_logits(n + 1)
        ksl = slice(n * HEAD_DIM, (n + 1) * HEAD_DIM)
        lgn = lg_ref.at[n % 2]
        pn = p_ref.at[n % 2]
        mpart = jnp.full((grp, width), -jnp.inf, F32)
        for i in range(0, s_c, grp):
            b = bias_ref[i:i + grp, :]
            lg = lgn[i:i + grp, :] + (b if n_slabs == 1 else jnp.concatenate([b] * n_slabs, axis=1))
            lgn[i:i + grp, :] = lg
            mpart = jnp.maximum(mpart, lg)
        mx = jnp.max(mpart, axis=0, keepdims=True)
        dpart = jnp.zeros((grp, width), F32)
        for i in range(0, s_c, grp):
            p = jnp.exp2(lgn[i:i + grp, :] - mx)
            dpart = dpart + p
            pn[i:i + grp, :] = p.astype(BF16)
        den = jnp.sum(dpart, axis=0, keepdims=True)
        o_t = jnp.dot(vt_ref[ksl, 0:s_c], pn[0:s_c, :], preferred_element_type=F32) / den
        for sb in range(n_slabs):
            o = jnp.transpose(o_t[:, sb * LANE:(sb + 1) * LANE])
            for r in range(reps):
                g = sb * reps + r
                hsl = slice((n * Q_PER_KV + g) * HEAD_DIM, (n * Q_PER_KV + g + 1) * HEAD_DIM)
                o_ref[:, hsl] = o[r * tq:(r + 1) * tq].astype(o_ref.dtype)


def _dsa_kernel(qi_ref, kwq_ref, q_ref, kw_ref, k_ref, v_ref, *rest, tq, t_new, p_len, n_classes, topk):
    if p_len:
        (pki_ref, pk_ref, pv_ref, o_ref, klo_ref, khi_ref, ks_ref, vt_ref, bias_ref, lg_ref, p_ref,
         tmp_ref) = rest
    else:
        o_ref, klo_ref, khi_ref, ks_ref, vt_ref, bias_ref, lg_ref, p_ref = rest
        pk_ref = pv_ref = None
    s_pad = ks_ref.shape[0]
    n_valid = p_len + t_new
    n_kv = ks_ref.shape[1] // HEAD_DIM
    j = pl.program_id(1)

    def head_rows(src_ref, rows, n):
        return src_ref[pl.ds(n, rows, stride=n_kv), :]

    @pl.when(j == 0)
    def _():
        lane = lax.broadcasted_iota(jnp.int32, (t_new, LANE), 1)
        new_lo = jnp.where(lane < IDX_DIM, kw_ref[...], 0.0)
        if p_len:
            tmp_ref[...] = jnp.zeros(tmp_ref.shape, F32)
            tmp_ref[0:p_len, 0:IDX_DIM] = pki_ref[...]
            tmp_ref[p_len:n_valid, :] = new_lo
            lo = tmp_ref[...]
        else:
            lo = new_lo
        klo_ref[...] = lo.astype(BF16)
        khi_ref[...] = pltpu.roll(lo, IDX_DIM, 1).astype(BF16)
        tail = jnp.zeros((s_pad - n_valid, HEAD_DIM), F32)
        for n in range(n_kv):
            csl = slice(n * HEAD_DIM, (n + 1) * HEAD_DIM)
            new_k = head_rows(k_ref, t_new, n)
            new_v = head_rows(v_ref, t_new, n)
            if s_pad > n_valid:
                new_k = jnp.concatenate([new_k, tail], axis=0)
                new_v = jnp.concatenate([new_v, tail], axis=0)
            if p_len:
                ks_ref[0:p_len, csl] = head_rows(pk_ref, p_len, n).astype(BF16)
                vt_ref[csl, 0:p_len] = jnp.transpose(head_rows(pv_ref, p_len, n)).astype(BF16)
            ks_ref[p_len:s_pad, csl] = new_k.astype(BF16)
            vt_ref[csl, p_len:s_pad] = jnp.transpose(new_v).astype(BF16)

    common = dict(tq=tq, n_kv=n_kv, topk=topk, n_valid=n_valid, scale=HEAD_DIM ** -0.5)
    refs = (qi_ref, kwq_ref, q_ref, klo_ref, khi_ref, ks_ref, vt_ref, o_ref, bias_ref, lg_ref, p_ref)
    if p_len:
        _dsa_select_attend(*refs, s_c=s_pad, row0=None, **common)
    else:
        per_class = pl.num_programs(1) // n_classes
        for c in range(n_classes):
            @pl.when(j // per_class == c)
            def _(c=c):
                _dsa_select_attend(*refs, s_c=(c + 1) * (s_pad // n_classes), row0=j * tq, **common)


def dsa(qi, kw, q, k, v, past, *, tq, topk):
    b, t, dq = q.shape
    n_kv = k.shape[1] // t
    dkv = n_kv * HEAD_DIM
    nq = t // tq
    row = lambda i, j: (i, j, 0)
    full = lambda i, j: (i, 0, 0)
    in_specs = [
        pl.BlockSpec((None, tq, qi.shape[2]), row),
        pl.BlockSpec((None, tq, LANE), row),
        pl.BlockSpec((None, tq, dq), row),
        pl.BlockSpec((None, t, LANE), full),
        pl.BlockSpec((None, t * n_kv, HEAD_DIM), full),
        pl.BlockSpec((None, t * n_kv, HEAD_DIM), full),
    ]
    args = [qi, kw, q, kw, k, v]
    if past is None:
        p_len = 0
        s_pad = t
        n_classes = DSA_PREFIX_CLASSES if (nq % DSA_PREFIX_CLASSES == 0
                                           and t % (DSA_PREFIX_CLASSES * LANE) == 0) else 1
        assert tq % CHUNK == 0 and LANE % tq == 0
    else:
        layer, pki, pk, pv = past
        p_len = pki.shape[2]
        s_pad = -(-(p_len + t) // LANE) * LANE
        n_classes = 1
        assert nq == 1 and p_len % LANE == 0 and LANE % tq == 0 and Q_PER_KV * tq % LANE == 0
        cache = lambda i, j: (layer, i, 0, 0)
        in_specs += [pl.BlockSpec((None, None, p_len, IDX_DIM), cache),
                     pl.BlockSpec((None, None, p_len * n_kv, HEAD_DIM), cache),
                     pl.BlockSpec((None, None, p_len * n_kv, HEAD_DIM), cache)]
        args += [pki, pk, pv]
    scratch = [pltpu.VMEM((s_pad, LANE), BF16), pltpu.VMEM((s_pad, LANE), BF16),
               pltpu.VMEM((s_pad, dkv), BF16), pltpu.VMEM((dkv, s_pad), BF16),
               pltpu.VMEM((s_pad, LANE), F32),
               pltpu.VMEM((2, s_pad, Q_PER_KV * tq), F32), pltpu.VMEM((2, s_pad, Q_PER_KV * tq), BF16)]
    if past is not None:
        scratch.append(pltpu.VMEM((s_pad, LANE), F32))
    kern = functools.partial(_dsa_kernel, tq=tq, t_new=t, p_len=p_len, n_classes=n_classes, topk=topk)
    return pl.pallas_call(
        kern,
        grid=(b, nq),
        in_specs=in_specs,
        out_specs=pl.BlockSpec((None, tq, dq), row),
        out_shape=jax.ShapeDtypeStruct((b, t, dq), BF16),
        scratch_shapes=scratch,
        compiler_params=_params("arbitrary", "arbitrary"),
        name="dsa",
    )(*args)


def _merge_kernel(a_ref, b_ref, wa_ref, wb_ref, ga_ref, gb_ref, o_ref, waq_ref, wbq_ref):
    @pl.when(_first_row_tile())
    def _():
        waq_ref[...] = wa_ref[...].astype(BF16)
        wbq_ref[...] = wb_ref[...].astype(BF16)

    ya = jnp.dot(a_ref[...], waq_ref[...], preferred_element_type=F32)
    yb = jnp.dot(b_ref[...], wbq_ref[...], preferred_element_type=F32)
    y = _sigmoid(ga_ref[...]) * ya + _sigmoid(gb_ref[...]) * yb
    o_ref[...] = y.astype(o_ref.dtype)


def merge(a, b, wa, wb, gates):
    m, ka = a.shape
    kb = b.shape[1]
    d = wa.shape[1]
    tm = _tile(m, 1024, SUBLANE)
    tn = _tile(d, 512, LANE)
    nj = d // tn
    return pl.pallas_call(
        _merge_kernel,
        grid=(nj, m // tm),
        in_specs=[
            pl.BlockSpec((tm, ka), lambda j, i: (i, 0)),
            pl.BlockSpec((tm, kb), lambda j, i: (i, 0)),
            pl.BlockSpec((ka, tn), lambda j, i: (0, j)),
            pl.BlockSpec((kb, tn), lambda j, i: (0, j)),
            pl.BlockSpec((tm, tn), lambda j, i: (i, j)),
            pl.BlockSpec((tm, tn), lambda j, i: (i, j + nj)),
        ],
        out_specs=pl.BlockSpec((tm, tn), lambda j, i: (i, j)),
        out_shape=jax.ShapeDtypeStruct((m, d), BF16),
        scratch_shapes=[pltpu.VMEM((ka, tn), BF16), pltpu.VMEM((kb, tn), BF16)],
        compiler_params=_params("arbitrary", "arbitrary"),
        name="merge",
    )(a, b, wa, wb, gates, gates)


def _matmul_res_kernel(a_ref, b_ref, r_ref, o_ref):
    o_ref[...] = r_ref[...] + jnp.dot(a_ref[...], b_ref[...], preferred_element_type=F32)


def matmul_residual(a, b, res):
    m, k = a.shape
    n = b.shape[1]
    tm = _tile(m, 1024, SUBLANE)
    tn = _tile(n, 1024, LANE)
    return pl.pallas_call(
        _matmul_res_kernel,
        grid=(n // tn, m // tm),
        in_specs=[
            pl.BlockSpec((tm, k), lambda j, i: (i, 0)),
            pl.BlockSpec((k, tn), lambda j, i: (0, j)),
            pl.BlockSpec((tm, tn), lambda j, i: (i, j)),
        ],
        out_specs=pl.BlockSpec((tm, tn), lambda j, i: (i, j)),
        out_shape=jax.ShapeDtypeStruct((m, n), F32),
        compiler_params=_params("parallel", "arbitrary"),
        name="matmul_residual",
    )(a, b, res)


def _ffn_up_kernel(h_ref, wg_ref, wu_ref, cwg_ref, cwu_ref, cbg_ref, cbu_ref, sg_ref, su_ref,
                   act_ref, zg_ref, zu_ref, wq_ref, buf_ref, *, seq_len, tiles_per_seq, n_sub):
    tm = h_ref.shape[0]
    tn = act_ref.shape[1]
    pad = SUBLANE

    @pl.when(_first_row_tile())
    def _():
        wq_ref[:, 0:tn] = wg_ref[...].astype(BF16)
        wq_ref[:, tn:2 * tn] = wu_ref[...].astype(BF16)

    cw = jnp.concatenate([cwg_ref[...], cwu_ref[...]], axis=1)
    cb = jnp.concatenate([cbg_ref[...], cbu_ref[...]], axis=1)

    def gated(c):
        gate, up = c[:, :tn], c[:, tn:]
        return (gate * _sigmoid(gate) * up).astype(act_ref.dtype)

    if tiles_per_seq >= 1:
        @pl.when(pl.program_id(1) % tiles_per_seq == 0)
        def _():
            buf_ref[pad - 2:pad, 0:tn] = sg_ref[0]
            buf_ref[pad - 2:pad, tn:2 * tn] = su_ref[0]
        rs = tm // n_sub
        zs = [jnp.dot(h_ref[r * rs:(r + 1) * rs, :], wq_ref[...], preferred_element_type=F32)
              for r in range(n_sub)]
        for r, z in enumerate(zs):
            base = pad + r * rs
            buf_ref[base:base + rs, :] = z
            zext = buf_ref[base - pad:base + rs, :]
            z1 = pltpu.roll(zext, 1, 0)[pad:]
            z2 = pltpu.roll(zext, 2, 0)[pad:]
            c = cb + cw[0:1] * z2 + cw[1:2] * z1 + cw[2:3] * z
            act_ref[r * rs:(r + 1) * rs, :] = gated(c)
        tail = buf_ref[pad + tm - 2:pad + tm, :]
        zg_ref[0] = tail[:, :tn]
        zu_ref[0] = tail[:, tn:]
        buf_ref[pad - 2:pad, :] = tail
    else:
        buf_ref[0:pad, :] = jnp.zeros((pad, 2 * tn), F32)
        buf_ref[pad:pad + tm, :] = jnp.dot(h_ref[...], wq_ref[...], preferred_element_type=F32)
        rowi = lax.broadcasted_iota(jnp.int32, (seq_len, 2 * tn), 0)
        for s in range(tm // seq_len):
            base = pad + s * seq_len
            st0 = jnp.concatenate([sg_ref[s, 0:1, :], su_ref[s, 0:1, :]], axis=1)
            st1 = jnp.concatenate([sg_ref[s, 1:2, :], su_ref[s, 1:2, :]], axis=1)
            z0 = buf_ref[base:base + seq_len, :]
            p1 = jnp.where(rowi == 0, st1, buf_ref[base - 1:base - 1 + seq_len, :])
            p2 = jnp.where(rowi == 0, st0, jnp.where(rowi == 1, st1, buf_ref[base - 2:base - 2 + seq_len, :]))
            act_ref[s * seq_len:(s + 1) * seq_len, :] = gated(cb + cw[0:1] * p2 + cw[1:2] * p1 + cw[2:3] * z0)
            tail = buf_ref[base + seq_len - 2:base + seq_len, :]
            zg_ref[s] = tail[:, :tn]
            zu_ref[s] = tail[:, tn:]


def ffn_up(h, w_up, conv_w, conv_b, state, seq_len):
    m, d = h.shape
    f = w_up.shape[1] // 2
    tm = _tile(m, FFN_UP_TILE_ROWS, SUBLANE)
    tn = _tile(f, 256, LANE)
    nj = f // tn
    if seq_len >= tm:
        assert seq_len % tm == 0
        tiles_per_seq = seq_len // tm
        n_state = 1
        state_map_g = lambda j, i: (i // tiles_per_seq, 0, j)
        state_map_u = lambda j, i: (i // tiles_per_seq, 0, j + nj)
        n_last = m // tm
    else:
        assert tm % seq_len == 0 and seq_len % SUBLANE == 0
        tiles_per_seq = 0
        n_state = tm // seq_len
        state_map_g = lambda j, i: (i, 0, j)
        state_map_u = lambda j, i: (i, 0, j + nj)
        n_last = m // seq_len
    kern = functools.partial(_ffn_up_kernel, seq_len=seq_len, tiles_per_seq=tiles_per_seq,
                             n_sub=max(1, tm // FFN_UP_SUB_ROWS))
    cb = conv_b.reshape(1, 2 * f)
    act, zg, zu = pl.pallas_call(
        kern,
        grid=(nj, m // tm),
        in_specs=[
            pl.BlockSpec((tm, d), lambda j, i: (i, 0)),
            pl.BlockSpec((d, tn), lambda j, i: (0, j)),
            pl.BlockSpec((d, tn), lambda j, i: (0, j + nj)),
            pl.BlockSpec((CONV_WIDTH, tn), lambda j, i: (0, j)),
            pl.BlockSpec((CONV_WIDTH, tn), lambda j, i: (0, j + nj)),
            pl.BlockSpec((1, tn), lambda j, i: (0, j)),
            pl.BlockSpec((1, tn), lambda j, i: (0, j + nj)),
            pl.BlockSpec((n_state, 2, tn), state_map_g),
            pl.BlockSpec((n_state, 2, tn), state_map_u),
        ],
        out_specs=[
            pl.BlockSpec((tm, tn), lambda j, i: (i, j)),
            pl.BlockSpec((n_state, 2, tn), lambda j, i: (i, 0, j)),
            pl.BlockSpec((n_state, 2, tn), lambda j, i: (i, 0, j)),
        ],
        out_shape=[
            jax.ShapeDtypeStruct((m, f), BF16),
            jax.ShapeDtypeStruct((n_last, 2, f), F32),
            jax.ShapeDtypeStruct((n_last, 2, f), F32),
        ],
        scratch_shapes=[pltpu.VMEM((d, 2 * tn), BF16), pltpu.VMEM((tm + SUBLANE, 2 * tn), F32)],
        compiler_params=_params("arbitrary", "arbitrary", vmem_limit_bytes=FFN_UP_VMEM_LIMIT_BYTES),
        name="ffn_up",
    )(h, w_up, w_up, conv_w, conv_w, cb, cb, state, state)
    zlast = jnp.concatenate([zg, zu], axis=-1)
    if tiles_per_seq > 1:
        zlast = zlast[tiles_per_seq - 1::tiles_per_seq]
    return act, zlast


def _ffn_down_kernel(a_ref, b_ref, r_ref, o_ref):
    o_ref[...] = r_ref[...] + jnp.dot(a_ref[...], b_ref[...], preferred_element_type=F32)


def ffn_down(a, b, res):
    m, k = a.shape
    n = b.shape[1]
    tm = _tile(m, 512, SUBLANE)
    tn = _tile(n, 512, LANE)
    return pl.pallas_call(
        _ffn_down_kernel,
        grid=(m // tm, n // tn),
        in_specs=[
            pl.BlockSpec((tm, k), lambda i, j: (i, 0)),
            pl.BlockSpec((k, tn), lambda i, j: (0, j)),
            pl.BlockSpec((tm, tn), lambda i, j: (i, j)),
        ],
        out_specs=pl.BlockSpec((tm, tn), lambda i, j: (i, j)),
        out_shape=jax.ShapeDtypeStruct((m, n), F32),
        compiler_params=_params("parallel", "arbitrary"),
        name="ffn_down",
    )(a, b, res)


def _in_offsets(d):
    da = d // 2
    n_heads = d // 256
    n_kv = n_heads // Q_PER_KV
    sizes = (da, da, n_heads * HEAD_DIM, n_kv * HEAD_DIM, n_kv * HEAD_DIM,
             N_IDX_HEADS * IDX_DIM, IDX_DIM, N_IDX_HEADS, d, d)
    offs = [0]
    for s in sizes:
        offs.append(offs[-1] + s)
    return offs


def _layer(x, pos, seq_len, w, past, conv_state, want_vn):
    bsz, t, d = x.shape
    m = bsz * t
    x2 = x.reshape(m, d)
    n_heads = d // 256
    n_kv = n_heads // Q_PER_KV
    offs = _in_offsets(d)
    w_in = w["in_t"]
    assert offs[8] - offs[6] <= LANE and w["in_gates_t"].shape[0] == 2 * d

    h = rmsnorm(x2, w["norm_attn_g"], BF16)

    rows = max(t, _tile(m, 1024, SUBLANE))
    rope_h = (HEAD_DIM // 8,) + _rope_tables(pos, HEAD_DIM, HEAD_DIM // 4, rows)
    rope_i = (IDX_DIM // 8,) + _rope_tables(pos, IDX_DIM, IDX_DIM // 4, rows)
    ci, s1i, s2i = rope_i[1:]
    lane = jnp.arange(LANE)
    is_w = (lane >= IDX_DIM) & (lane < IDX_DIM + N_IDX_HEADS)
    w_scale = N_IDX_HEADS ** -0.5 * IDX_DIM ** -0.5
    rope_kw = (IDX_DIM // 8,
               jnp.where(is_w, w_scale, jnp.where(lane < IDX_DIM, ci, 1.0)).astype(F32),
               jnp.where(lane < IDX_DIM, s1i, 0.0), jnp.where(lane < IDX_DIM, s2i, 0.0))

    uv = project(h, w_in, offs[0], offs[2] - offs[0], F32)
    q = project(h, w_in, offs[2], offs[3] - offs[2], BF16, rope_h)
    qi = project(h, w_in, offs[5], offs[6] - offs[5], BF16, rope_i)
    k, v, kw = project_kv(h, w_in, offs[3], offs[4], offs[6], offs[4] - offs[3], rope_h, rope_kw)
    gates = project(h, w["in_gates_t"], 0, 2 * d, F32)

    chunk_rows = min(t, GMLP_CHUNK)
    a_out, vn = gmlp(uv, w["gmlp_norm_g"], w["gmlp_wm"](chunk_rows), w["gmlp_bias"](chunk_rows), want_vn)

    n_valid = t if past is None else past[1].shape[2] + t
    b_out = dsa(qi.reshape(bsz, t, -1), kw.reshape(bsz, t, LANE), q.reshape(bsz, t, -1),
                k.reshape(bsz, t * n_kv, HEAD_DIM), v.reshape(bsz, t * n_kv, HEAD_DIM), past,
                tq=min(DSA_QUERY_BLOCK, t), topk=min(TOPK_MAX, n_valid // 4))

    y = merge(a_out, b_out.reshape(m, n_heads * HEAD_DIM), w["a"], w["b"], gates)
    x2 = matmul_residual(y, w["o"], x2)

    hf = rmsnorm(x2, w["norm_ffn_g"], BF16)
    act, zlast = ffn_up(hf, w["up"], w["conv_w"], w["conv_b"], conv_state, seq_len)
    x2 = ffn_down(act, w["down"], x2)
    kidx = kw[:, :IDX_DIM].reshape(bsz, t, IDX_DIM)
    return (x2.reshape(bsz, t, d), k.reshape(bsz, t, n_kv, HEAD_DIM), v.reshape(bsz, t, n_kv, HEAD_DIM),
            kidx, zlast, vn)


def kernel(x_prompt, x_sample, cache_k, cache_v, cache_kidx, state_ffn_conv, norm_attn_g, w_in, gmlp_norm_g, gmlp_ws, gmlp_b, w_branch_a, w_branch_b, w_out, norm_ffn_g, w_up, conv_w, conv_b, w_down, norm_final_g):
    bsz, s, d = x_prompt.shape
    dbsz, t, _ = x_sample.shape
    depth = w_in.shape[0]
    p_len = cache_k.shape[2]
    da = d // 2
    f2 = w_up.shape[2]
    assert s % GMLP_CHUNK == 0 and GMLP_CHUNK % t == 0 and s % CHUNK == 0

    pos_p = jnp.arange(s, dtype=jnp.int32)
    pos_s = p_len + jnp.arange(t, dtype=jnp.int32)
    ci = jnp.arange(GMLP_CHUNK)
    chunk_mask = (ci[None, :] // CHUNK) <= (ci[:, None] // CHUNK)
    cache_k4 = cache_k.reshape(depth, dbsz, -1, HEAD_DIM)
    cache_v4 = cache_v.reshape(depth, dbsz, -1, HEAD_DIM)
    in_offs = _in_offsets(d)
    assert in_offs[-1] == w_in.shape[2]

    xp, xs = x_prompt, x_sample
    outs = [[] for _ in range(9)]
    for l in range(depth):
        wm_full = jnp.where(chunk_mask[None], gmlp_ws[l], 0.0)
        bias_rows = jnp.repeat(jnp.transpose(gmlp_b[l]), da // G_A, axis=1)

        def gmlp_wm(rows, wm_full=wm_full):
            reps = GMLP_CHUNK // rows
            blk = wm_full[:, :rows, :rows]
            eye = jnp.eye(reps, dtype=F32)
            return jnp.einsum("ab,gij->gaibj", eye, blk).reshape(G_A, GMLP_CHUNK, GMLP_CHUNK).astype(BF16)

        def gmlp_bias(rows, bias_rows=bias_rows):
            return jnp.tile(bias_rows[:rows], (GMLP_CHUNK // rows, 1))

        w = dict(
            norm_attn_g=norm_attn_g[l], gmlp_norm_g=gmlp_norm_g[l], norm_ffn_g=norm_ffn_g[l],
            gmlp_wm=gmlp_wm, gmlp_bias=gmlp_bias,
            a=w_branch_a[l], b=w_branch_b[l], up=w_up[l], conv_w=conv_w[l], conv_b=conv_b[l],
            o=w_out[l].astype(BF16), down=w_down[l].astype(BF16),
        )
        w_in_t = jnp.swapaxes(w_in[l], 0, 1)
        w["in_t"] = w_in_t[:in_offs[6] + in_offs[4] - in_offs[3]].astype(BF16)
        w["in_gates_t"] = w_in_t[in_offs[8]:].astype(BF16)

        xp, kp, vp, kip, cp, _ = _layer(xp, pos_p, s, w, None, jnp.zeros((bsz, CONV_WIDTH - 1, f2), F32),
                                        False)
        past = (l, cache_kidx, cache_k4, cache_v4)
        xs, ks, vs, kis, cs, gv = _layer(xs, pos_s, t, w, past, state_ffn_conv[l], True)
        for lst, val in zip(outs, (kp, vp, kip, cp, ks, vs, kis, cs, gv.reshape(dbsz, t, da))):
            lst.append(val)

    y_prompt = rmsnorm(xp.reshape(bsz * s, d), norm_final_g, F32).reshape(bsz, s, d)
    y_sample = rmsnorm(xs.reshape(dbsz * t, d), norm_final_g, F32).reshape(dbsz, t, d)
    stack = lambda o: o[0][None] if depth == 1 else jnp.stack(o)
    return (y_prompt, y_sample) + tuple(stack(o) for o in outs)
```

```python
import functools
import math

import jax
import jax.numpy as jnp
from jax import lax
from jax.experimental import pallas as pl
from jax.experimental.pallas import tpu as pltpu

CHUNK = 64
GMLP_CHUNK = 128
G_A = 8
HEAD_DIM = 128
Q_PER_KV = 4
N_IDX_HEADS = 16
IDX_DIM = 64
TOPK_MAX = 256
ROPE_THETA = 500000.0
CONV_WIDTH = 3
EPS = 1e-6

LANE = 128
SUBLANE = 8
VMEM_LIMIT_BYTES = 56 * 1024 * 1024

FFN_UP_TILE_ROWS = 2048
FFN_UP_SUB_ROWS = 256
FFN_UP_VMEM_LIMIT_BYTES = 60 * 1024 * 1024
DSA_PREFIX_CLASSES = 4
GMLP_CHUNKS_PER_STEP = 4
DSA_QUERY_BLOCK = 128
TOPK_STEP_SURPLUS = 2.0
TOPK_PROBES_PER_CHECK = 3

BF16 = jnp.bfloat16
F32 = jnp.float32
NT_DIMS = (((1,), (1,)), ((), ()))


def _params(*semantics, vmem_limit_bytes=VMEM_LIMIT_BYTES):
    return pltpu.CompilerParams(dimension_semantics=semantics, vmem_limit_bytes=vmem_limit_bytes)


def _tile(n, pref, align):
    if n <= pref:
        return n
    t = (pref // align) * align
    while t >= align:
        if n % t == 0:
            return t
        t -= align
    raise ValueError(f"no {align}-aligned tile of {n} below {pref}")


def _first_row_tile():
    return pl.program_id(1) == 0


def _sigmoid(x):
    return 0.5 * jnp.tanh(0.5 * x) + 0.5


def _rmsnorm_kernel(x_ref, g_ref, o_ref):
    x = x_ref[...]
    ms = jnp.mean(x * x, axis=-1, keepdims=True)
    o_ref[...] = (x * lax.rsqrt(ms + EPS) * g_ref[...]).astype(o_ref.dtype)


def rmsnorm(x, g, out_dtype):
    n, d = x.shape
    tr = _tile(n, 512, SUBLANE)
    return pl.pallas_call(
        _rmsnorm_kernel,
        grid=(n // tr,),
        in_specs=[pl.BlockSpec((tr, d), lambda i: (i, 0)), pl.BlockSpec((1, d), lambda i: (0, 0))],
        out_specs=pl.BlockSpec((tr, d), lambda i: (i, 0)),
        out_shape=jax.ShapeDtypeStruct((n, d), out_dtype),
        compiler_params=_params("parallel"),
        name="rmsnorm",
    )(x, g.reshape(1, d))


def _rope_slab(x, c, s1, s2, shift):
    return x * c + pltpu.roll(x, LANE - shift, 1) * s1 + pltpu.roll(x, shift, 1) * s2


def _proj_kernel(a_ref, b_ref, *rest, rope_shift):
    acc = lax.dot_general(a_ref[...], b_ref[...], NT_DIMS, preferred_element_type=F32)
    if rope_shift:
        c_ref, s1_ref, s2_ref, o_ref = rest
        c, s1, s2 = c_ref[...], s1_ref[...], s2_ref[...]
        for h in range(acc.shape[1] // LANE):
            sl = slice(h * LANE, (h + 1) * LANE)
            o_ref[:, sl] = _rope_slab(acc[:, sl], c, s1, s2, rope_shift).astype(o_ref.dtype)
    else:
        (o_ref,) = rest
        o_ref[...] = acc.astype(o_ref.dtype)


def project(a, wt, col0, ncols, out_dtype, rope=None):
    m, k = a.shape
    tm = _tile(m, 1024, SUBLANE)
    tn = _tile(math.gcd(ncols, col0), 1024, LANE)
    assert col0 % tn == 0 and ncols % tn == 0
    j0 = col0 // tn
    in_specs = [pl.BlockSpec((tm, k), lambda j, i: (i, 0)), pl.BlockSpec((tn, k), lambda j, i: (j + j0, 0))]
    args = [a, wt]
    shift = 0
    if rope is not None:
        shift, tabs = rope[0], rope[1:]
        reps = tabs[0].shape[0] // tm
        assert reps * tm == tabs[0].shape[0]
        for t in tabs:
            in_specs.append(pl.BlockSpec((tm, LANE), lambda j, i: (i % reps, 0)))
            args.append(t)
    return pl.pallas_call(
        functools.partial(_proj_kernel, rope_shift=shift),
        grid=(ncols // tn, m // tm),
        in_specs=in_specs,
        out_specs=pl.BlockSpec((tm, tn), lambda j, i: (i, j)),
        out_shape=jax.ShapeDtypeStruct((m, ncols), out_dtype),
        compiler_params=_params("parallel", "arbitrary"),
        name="project",
    )(*args)


def _proj_kv_kernel(a_ref, b_ref, ck_ref, s1k_ref, s2k_ref, cw_ref, s1w_ref, s2w_ref,
                    k_ref, v_ref, kw_ref, *, shift_k, shift_w):
    g = pl.program_id(1)
    tm = a_ref.shape[0]
    n_heads = b_ref.shape[0] // LANE

    def matmul(rows):
        return lax.dot_general(a_ref[...], b_ref[0:rows, :], NT_DIMS, preferred_element_type=F32)

    @pl.when(g == 0)
    def _():
        acc = matmul(n_heads * LANE)
        c, s1, s2 = ck_ref[...], s1k_ref[...], s2k_ref[...]
        for h in range(n_heads):
            k_ref[pl.ds(h, tm, stride=n_heads), :] = _rope_slab(acc[:, h * LANE:(h + 1) * LANE], c, s1, s2, shift_k)

    @pl.when(g == 1)
    def _():
        acc = matmul(n_heads * LANE)
        for h in range(n_heads):
            v_ref[pl.ds(h, tm, stride=n_heads), :] = acc[:, h * LANE:(h + 1) * LANE]

    @pl.when(g == 2)
    def _():
        kw_ref[...] = _rope_slab(matmul(LANE), cw_ref[...], s1w_ref[...], s2w_ref[...], shift_w)


def project_kv(a, wt, col_k, col_v, col_kw, width, rope_k, rope_kw):
    m, kdim = a.shape
    tm = _tile(m, 1024, SUBLANE)
    assert col_k % width == 0 and col_v % width == 0 and col_kw % width == 0
    assert col_kw + width <= wt.shape[0] and width % LANE == 0
    blocks = (col_k // width, col_v // width, col_kw // width)
    heads = width // LANE

    def w_map(i, g):
        return (jnp.where(g == 0, blocks[0], jnp.where(g == 1, blocks[1], blocks[2])), 0)

    tabs = rope_k[1:] + rope_kw[1:]
    reps = tabs[0].shape[0] // tm
    assert reps * tm == tabs[0].shape[0]
    return pl.pallas_call(
        functools.partial(_proj_kv_kernel, shift_k=rope_k[0], shift_w=rope_kw[0]),
        grid=(m // tm, 3),
        in_specs=[pl.BlockSpec((tm, kdim), lambda i, g: (i, 0)), pl.BlockSpec((width, kdim), w_map)]
        + [pl.BlockSpec((tm, LANE), lambda i, g: (i % reps, 0)) for _ in tabs],
        out_specs=[
            pl.BlockSpec((tm * heads, LANE), lambda i, g: (i, 0)),
            pl.BlockSpec((tm * heads, LANE), lambda i, g: (i, 0)),
            pl.BlockSpec((tm, LANE), lambda i, g: (i, 0)),
        ],
        out_shape=[
            jax.ShapeDtypeStruct((m * heads, LANE), F32),
            jax.ShapeDtypeStruct((m * heads, LANE), F32),
            jax.ShapeDtypeStruct((m, LANE), F32),
        ],
        compiler_params=_params("parallel", "arbitrary"),
        name="project_kv",
    )(a, wt, *tabs)


def _rope_tables(pos, head_dim, rot_dim, rows):
    half = rot_dim // 2
    inv_freq = ROPE_THETA ** (-jnp.arange(half, dtype=F32) / half)
    ang = pos.astype(F32)[:, None] * inv_freq[None, :]
    cos, sin = jnp.cos(ang), jnp.sin(ang)
    t = pos.shape[0]
    zeros_h = jnp.zeros((t, half), F32)
    rest0 = jnp.zeros((t, head_dim - rot_dim), F32)
    c = jnp.concatenate([cos, cos, jnp.ones((t, head_dim - rot_dim), F32)], axis=1)
    s1 = jnp.concatenate([-sin, zeros_h, rest0], axis=1)
    s2 = jnp.concatenate([zeros_h, sin, rest0], axis=1)
    reps_l = LANE // head_dim
    reps_r = rows // t
    return tuple(jnp.tile(x, (reps_r, reps_l)) for x in (c, s1, s2))


def _gmlp_kernel(u_ref, v_ref, g_ref, wm_ref, b_ref, a_ref, *vn_out, groups):
    gw = v_ref.shape[1] // groups
    for c in range(v_ref.shape[0] // GMLP_CHUNK):
        rows = slice(c * GMLP_CHUNK, (c + 1) * GMLP_CHUNK)
        v = v_ref[rows, :]
        vn = v * lax.rsqrt(jnp.mean(v * v, axis=-1, keepdims=True) + EPS) * g_ref[...]
        if vn_out:
            vn_out[0][rows, :] = vn
        vb = vn.astype(BF16)
        for g in range(groups):
            sl = slice(g * gw, (g + 1) * gw)
            s = jnp.dot(wm_ref[g], vb[:, sl], preferred_element_type=F32) + b_ref[:, sl]
            a_ref[rows, sl] = (u_ref[rows, sl] * s).astype(a_ref.dtype)


def gmlp(uv, g_norm, wm, bias, want_vn):
    m, d2 = uv.shape
    da = d2 // 2
    tc = GMLP_CHUNK * math.gcd(m // GMLP_CHUNK, GMLP_CHUNKS_PER_STEP)
    out_shape = [jax.ShapeDtypeStruct((m, da), BF16)]
    out_specs = [pl.BlockSpec((tc, da), lambda i: (i, 0))]
    if want_vn:
        out_shape.append(jax.ShapeDtypeStruct((m, da), F32))
        out_specs.append(pl.BlockSpec((tc, da), lambda i: (i, 0)))
    res = pl.pallas_call(
        functools.partial(_gmlp_kernel, groups=wm.shape[0]),
        grid=(m // tc,),
        in_specs=[
            pl.BlockSpec((tc, da), lambda i: (i, 0)),
            pl.BlockSpec((tc, da), lambda i: (i, 1)),
            pl.BlockSpec((1, da), lambda i: (0, 0)),
            pl.BlockSpec(wm.shape, lambda i: (0, 0, 0)),
            pl.BlockSpec((GMLP_CHUNK, da), lambda i: (0, 0)),
        ],
        out_specs=out_specs,
        out_shape=out_shape,
        compiler_params=_params("parallel"),
        name="gmlp",
    )(uv, uv, g_norm.reshape(1, da), wm, bias)
    return res if want_vn else (res[0], None)


LOG2_E = 1.4426950408889634
ROW_REDUCE_GROUP = 64
ATTN_ROW_GROUP = 32


def _reduce_rows(x, op):
    pair = {jnp.sum: jnp.add, jnp.min: jnp.minimum, jnp.max: jnp.maximum}[op]
    rows = x.shape[0]
    if rows % ROW_REDUCE_GROUP == 0:
        parts = [x[i:i + ROW_REDUCE_GROUP] for i in range(0, rows, ROW_REDUCE_GROUP)]
        while len(parts) > 1:
            parts = [pair(parts[i], parts[i + 1]) if i + 1 < len(parts) else parts[i]
                     for i in range(0, len(parts), 2)]
        x = parts[0]
    return op(x, axis=0, keepdims=True)

def _dsa_select_attend(qi_ref, kwq_ref, q_ref, klo_ref, khi_ref, ks_ref, vt_ref, o_ref, bias_ref,
                       lg_ref, p_ref, *,
                       s_c, tq, n_kv, topk, row0, n_valid, scale):
    n_tiles = s_c // LANE
    reps = LANE // tq

    def rep_rows(x):
        return x if reps == 1 else jnp.concatenate([x] * reps, axis=0)

    w_t = jnp.transpose(rep_rows(kwq_ref[...]))
    klo = klo_ref[0:s_c, :]
    khi = khi_ref[0:s_c, :]

    score = jnp.zeros((s_c, LANE), F32)
    for pp in range(N_IDX_HEADS // 4):
        qp2 = jnp.concatenate([rep_rows(qi_ref[:, (2 * pp + i) * LANE:(2 * pp + i + 1) * LANE])
                               for i in range(2)], axis=0)
        rel_lo = lax.dot_general(klo, qp2, NT_DIMS, preferred_element_type=F32)
        rel_hi = lax.dot_general(khi, qp2, NT_DIMS, preferred_element_type=F32)
        for i in range(2):
            r0 = IDX_DIM + 2 * (2 * pp + i)
            score = (score + jnp.maximum(rel_lo[:, i * LANE:(i + 1) * LANE], 0.0) * w_t[r0:r0 + 1, :]
                     + jnp.maximum(rel_hi[:, i * LANE:(i + 1) * LANE], 0.0) * w_t[r0 + 1:r0 + 2, :])

    kpos = lax.broadcasted_iota(jnp.int32, (s_c, LANE), 0)
    if row0 is not None:
        qrow = row0 + jnp.bitwise_and(lax.broadcasted_iota(jnp.int32, (s_c, LANE), 1), tq - 1)
        adm = kpos < (jnp.right_shift(qrow, CHUNK.bit_length() - 1) + 1) * CHUNK
    else:
        adm = kpos < n_valid
    score = jnp.where(adm, score, -jnp.inf)
    bias_ref[0:s_c, :] = score

    kf = float(topk)
    lo0 = _reduce_rows(jnp.where(adm, score, jnp.inf), jnp.min)
    hi0 = _reduce_rows(score, jnp.max)
    c_lo0 = _reduce_rows(jnp.where(adm, 1.0, 0.0), jnp.sum)
    c_hi0 = jnp.zeros((1, LANE), F32)

    def probe(t):
        cnt = jnp.zeros((ROW_REDUCE_GROUP, LANE), F32)
        nxt = jnp.full((ROW_REDUCE_GROUP, LANE), jnp.inf, F32)
        for i in range(0, s_c, ROW_REDUCE_GROUP):
            sc = bias_ref[i:i + ROW_REDUCE_GROUP, :]
            above = sc > t
            cnt = cnt + jnp.where(above, 1.0, 0.0)
            nxt = jnp.minimum(nxt, jnp.where(above, sc, jnp.inf))
        return jnp.sum(cnt, axis=0, keepdims=True), jnp.min(nxt, axis=0, keepdims=True)

    def active_of(lo, hi, c_lo):
        return (c_lo > kf) & (hi > lo)

    def cond(carry):
        lo, hi, c_lo, _ = carry
        return jnp.max(jnp.where(active_of(lo, hi, c_lo), 1.0, 0.0)) > 0.0

    def body(carry):
        for _ in range(TOPK_PROBES_PER_CHECK):
            carry = advance(carry)
        return carry

    def advance(carry):
        lo, hi, c_lo, c_hi = carry
        act = active_of(lo, hi, c_lo)
        mid = 0.5 * lo + 0.5 * hi
        step = (c_lo - kf <= TOPK_STEP_SURPLUS) | (mid <= lo) | (mid >= hi)
        t = jnp.where(step, lo, mid)
        cnt, nxt = probe(t)
        take = act & (cnt >= kf)
        drop = act & (cnt < kf)
        return (jnp.where(take, nxt, lo), jnp.where(drop, t, hi),
                jnp.where(take, cnt, c_lo), jnp.where(drop, cnt, c_hi))

    lo, hi, c_lo, c_hi = lax.while_loop(cond, body, (lo0, hi0, c_lo0, c_hi0))
    tie = c_lo > kf
    any_tie = jnp.max(jnp.where(tie, 1.0, 0.0)) > 0.0

    @pl.when(jnp.logical_not(any_tie))
    def _():
        for jt in range(n_tiles):
            sl = slice(jt * LANE, (jt + 1) * LANE)
            bias_ref[sl, :] = jnp.where(bias_ref[sl, :] >= lo, 0.0, -jnp.inf)

    @pl.when(any_tie)
    def _():
        quota = kf - c_hi
        tri = (lax.broadcasted_iota(jnp.int32, (LANE, LANE), 1)
               <= lax.broadcasted_iota(jnp.int32, (LANE, LANE), 0)).astype(F32).astype(BF16)
        before = jnp.zeros((1, LANE), F32)
        for jt in range(n_tiles):
            sl = slice(jt * LANE, (jt + 1) * LANE)
            sc = bias_ref[sl, :]
            cand = jnp.where((sc >= lo) & (sc <= hi), 1.0, 0.0)
            rank = jnp.dot(tri, cand.astype(BF16), preferred_element_type=F32) + before
            keep = (sc > hi) | ((cand > 0.0) & (rank <= quota))
            bias_ref[sl, :] = jnp.where(tie, jnp.where(keep, 0.0, -jnp.inf),
                                        jnp.where(sc >= lo, 0.0, -jnp.inf))
            before = before + jnp.sum(cand, axis=0, keepdims=True)

    width = Q_PER_KV * tq
    n_slabs = width // LANE
    grp = ATTN_ROW_GROUP
    def put_logits(n):
        qn = jnp.concatenate(
            [q_ref[:, (n * Q_PER_KV + g) * HEAD_DIM:(n * Q_PER_KV + g + 1) * HEAD_DIM]
             for g in range(Q_PER_KV)], axis=0)
        lg_ref[n % 2, 0:s_c, :] = lax.dot_general(
            ks_ref[0:s_c, n * HEAD_DIM:(n + 1) * HEAD_DIM], qn, NT_DIMS,
            preferred_element_type=F32) * (scale * LOG2_E)

    put_logits(0)
    for n in range(n_kv):
        if n + 1 < n_kv:
            put_logits(n + 1)
        ksl = slice(n * HEAD_DIM, (n + 1) * HEAD_DIM)
        lgn = lg_ref.at[n % 2]
        pn = p_ref.at[n % 2]
        mpart = jnp.full((grp, width), -jnp.inf, F32)
        for i in range(0, s_c, grp):
            b = bias_ref[i:i + grp, :]
            lg = lgn[i:i + grp, :] + (b if n_slabs == 1 else jnp.concatenate([b] * n_slabs, axis=1))
            lgn[i:i + grp, :] = lg
            mpart = jnp.maximum(mpart, lg)
        mx = jnp.max(mpart, axis=0, keepdims=True)
        dpart = jnp.zeros((grp, width), F32)
        for i in range(0, s_c, grp):
            p = jnp.exp2(lgn[i:i + grp, :] - mx)
            dpart = dpart + p
            pn[i:i + grp, :] = p.astype(BF16)
        den = jnp.sum(dpart, axis=0, keepdims=True)
        o_t = jnp.dot(vt_ref[ksl, 0:s_c], pn[0:s_c, :], preferred_element_type=F32) / den
        for sb in range(n_slabs):
            o = jnp.transpose(o_t[:, sb * LANE:(sb + 1) * LANE])
            for r in range(reps):
                g = sb * reps + r
                hsl = slice((n * Q_PER_KV + g) * HEAD_DIM, (n * Q_PER_KV + g + 1) * HEAD_DIM)
                o_ref[:, hsl] = o[r * tq:(r + 1) * tq].astype(o_ref.dtype)


def _dsa_kernel(qi_ref, kwq_ref, q_ref, kw_ref, k_ref, v_ref, *rest, tq, t_new, p_len, n_classes, topk):
    if p_len:
        (pki_ref, pk_ref, pv_ref, o_ref, klo_ref, khi_ref, ks_ref, vt_ref, bias_ref, lg_ref, p_ref,
         tmp_ref) = rest
    else:
        o_ref, klo_ref, khi_ref, ks_ref, vt_ref, bias_ref, lg_ref, p_ref = rest
        pk_ref = pv_ref = None
    s_pad = ks_ref.shape[0]
    n_valid = p_len + t_new
    n_kv = ks_ref.shape[1] // HEAD_DIM
    j = pl.program_id(1)

    def head_rows(src_ref, rows, n):
        return src_ref[pl.ds(n, rows, stride=n_kv), :]

    @pl.when(j == 0)
    def _():
        lane = lax.broadcasted_iota(jnp.int32, (t_new, LANE), 1)
        new_lo = jnp.where(lane < IDX_DIM, kw_ref[...], 0.0)
        if p_len:
            tmp_ref[...] = jnp.zeros(tmp_ref.shape, F32)
            tmp_ref[0:p_len, 0:IDX_DIM] = pki_ref[...]
            tmp_ref[p_len:n_valid, :] = new_lo
            lo = tmp_ref[...]
        else:
            lo = new_lo
        klo_ref[...] = lo.astype(BF16)
        khi_ref[...] = pltpu.roll(lo, IDX_DIM, 1).astype(BF16)
        tail = jnp.zeros((s_pad - n_valid, HEAD_DIM), F32)
        for n in range(n_kv):
            csl = slice(n * HEAD_DIM, (n + 1) * HEAD_DIM)
            new_k = head_rows(k_ref, t_new, n)
            new_v = head_rows(v_ref, t_new, n)
            if s_pad > n_valid:
                new_k = jnp.concatenate([new_k, tail], axis=0)
                new_v = jnp.concatenate([new_v, tail], axis=0)
            if p_len:
                ks_ref[0:p_len, csl] = head_rows(pk_ref, p_len, n).astype(BF16)
                vt_ref[csl, 0:p_len] = jnp.transpose(head_rows(pv_ref, p_len, n)).astype(BF16)
            ks_ref[p_len:s_pad, csl] = new_k.astype(BF16)
            vt_ref[csl, p_len:s_pad] = jnp.transpose(new_v).astype(BF16)

    common = dict(tq=tq, n_kv=n_kv, topk=topk, n_valid=n_valid, scale=HEAD_DIM ** -0.5)
    refs = (qi_ref, kwq_ref, q_ref, klo_ref, khi_ref, ks_ref, vt_ref, o_ref, bias_ref, lg_ref, p_ref)
    if p_len:
        _dsa_select_attend(*refs, s_c=s_pad, row0=None, **common)
    else:
        per_class = pl.num_programs(1) // n_classes
        for c in range(n_classes):
            @pl.when(j // per_class == c)
            def _(c=c):
                _dsa_select_attend(*refs, s_c=(c + 1) * (s_pad // n_classes), row0=j * tq, **common)


def dsa(qi, kw, q, k, v, past, *, tq, topk):
    b, t, dq = q.shape
    n_kv = k.shape[1] // t
    dkv = n_kv * HEAD_DIM
    nq = t // tq
    row = lambda i, j: (i, j, 0)
    full = lambda i, j: (i, 0, 0)
    in_specs = [
        pl.BlockSpec((None, tq, qi.shape[2]), row),
        pl.BlockSpec((None, tq, LANE), row),
        pl.BlockSpec((None, tq, dq), row),
        pl.BlockSpec((None, t, LANE), full),
        pl.BlockSpec((None, t * n_kv, HEAD_DIM), full),
        pl.BlockSpec((None, t * n_kv, HEAD_DIM), full),
    ]
    args = [qi, kw, q, kw, k, v]
    if past is None:
        p_len = 0
        s_pad = t
        n_classes = DSA_PREFIX_CLASSES if (nq % DSA_PREFIX_CLASSES == 0
                                           and t % (DSA_PREFIX_CLASSES * LANE) == 0) else 1
        assert tq % CHUNK == 0 and LANE % tq == 0
    else:
        layer, pki, pk, pv = past
        p_len = pki.shape[2]
        s_pad = -(-(p_len + t) // LANE) * LANE
        n_classes = 1
        assert nq == 1 and p_len % LANE == 0 and LANE % tq == 0 and Q_PER_KV * tq % LANE == 0
        cache = lambda i, j: (layer, i, 0, 0)
        in_specs += [pl.BlockSpec((None, None, p_len, IDX_DIM), cache),
                     pl.BlockSpec((None, None, p_len * n_kv, HEAD_DIM), cache),
                     pl.BlockSpec((None, None, p_len * n_kv, HEAD_DIM), cache)]
        args += [pki, pk, pv]
    scratch = [pltpu.VMEM((s_pad, LANE), BF16), pltpu.VMEM((s_pad, LANE), BF16),
               pltpu.VMEM((s_pad, dkv), BF16), pltpu.VMEM((dkv, s_pad), BF16),
               pltpu.VMEM((s_pad, LANE), F32),
               pltpu.VMEM((2, s_pad, Q_PER_KV * tq), F32), pltpu.VMEM((2, s_pad, Q_PER_KV * tq), BF16)]
    if past is not None:
        scratch.append(pltpu.VMEM((s_pad, LANE), F32))
    kern = functools.partial(_dsa_kernel, tq=tq, t_new=t, p_len=p_len, n_classes=n_classes, topk=topk)
    return pl.pallas_call(
        kern,
        grid=(b, nq),
        in_specs=in_specs,
        out_specs=pl.BlockSpec((None, tq, dq), row),
        out_shape=jax.ShapeDtypeStruct((b, t, dq), BF16),
        scratch_shapes=scratch,
        compiler_params=_params("arbitrary", "arbitrary"),
        name="dsa",
    )(*args)


def _merge_kernel(a_ref, b_ref, wa_ref, wb_ref, ga_ref, gb_ref, o_ref, waq_ref, wbq_ref):
    @pl.when(_first_row_tile())
    def _():
        waq_ref[...] = wa_ref[...].astype(BF16)
        wbq_ref[...] = wb_ref[...].astype(BF16)

    ya = jnp.dot(a_ref[...], waq_ref[...], preferred_element_type=F32)
    yb = jnp.dot(b_ref[...], wbq_ref[...], preferred_element_type=F32)
    y = _sigmoid(ga_ref[...]) * ya + _sigmoid(gb_ref[...]) * yb
    o_ref[...] = y.astype(o_ref.dtype)


def merge(a, b, wa, wb, gates):
    m, ka = a.shape
    kb = b.shape[1]
    d = wa.shape[1]
    tm = _tile(m, 1024, SUBLANE)
    tn = _tile(d, 512, LANE)
    nj = d // tn
    return pl.pallas_call(
        _merge_kernel,
        grid=(nj, m // tm),
        in_specs=[
            pl.BlockSpec((tm, ka), lambda j, i: (i, 0)),
            pl.BlockSpec((tm, kb), lambda j, i: (i, 0)),
            pl.BlockSpec((ka, tn), lambda j, i: (0, j)),
            pl.BlockSpec((kb, tn), lambda j, i: (0, j)),
            pl.BlockSpec((tm, tn), lambda j, i: (i, j)),
            pl.BlockSpec((tm, tn), lambda j, i: (i, j + nj)),
        ],
        out_specs=pl.BlockSpec((tm, tn), lambda j, i: (i, j)),
        out_shape=jax.ShapeDtypeStruct((m, d), BF16),
        scratch_shapes=[pltpu.VMEM((ka, tn), BF16), pltpu.VMEM((kb, tn), BF16)],
        compiler_params=_params("arbitrary", "arbitrary"),
        name="merge",
    )(a, b, wa, wb, gates, gates)


def _matmul_res_kernel(a_ref, b_ref, r_ref, o_ref):
    o_ref[...] = r_ref[...] + jnp.dot(a_ref[...], b_ref[...], preferred_element_type=F32)


def matmul_residual(a, b, res):
    m, k = a.shape
    n = b.shape[1]
    tm = _tile(m, 1024, SUBLANE)
    tn = _tile(n, 1024, LANE)
    return pl.pallas_call(
        _matmul_res_kernel,
        grid=(n // tn, m // tm),
        in_specs=[
            pl.BlockSpec((tm, k), lambda j, i: (i, 0)),
            pl.BlockSpec((k, tn), lambda j, i: (0, j)),
            pl.BlockSpec((tm, tn), lambda j, i: (i, j)),
        ],
        out_specs=pl.BlockSpec((tm, tn), lambda j, i: (i, j)),
        out_shape=jax.ShapeDtypeStruct((m, n), F32),
        compiler_params=_params("parallel", "arbitrary"),
        name="matmul_residual",
    )(a, b, res)


def _ffn_up_kernel(h_ref, wg_ref, wu_ref, cwg_ref, cwu_ref, cbg_ref, cbu_ref, sg_ref, su_ref,
                   act_ref, zg_ref, zu_ref, wq_ref, buf_ref, *, seq_len, tiles_per_seq, n_sub):
    tm = h_ref.shape[0]
    tn = act_ref.shape[1]
    pad = SUBLANE

    @pl.when(_first_row_tile())
    def _():
        wq_ref[:, 0:tn] = wg_ref[...].astype(BF16)
        wq_ref[:, tn:2 * tn] = wu_ref[...].astype(BF16)

    cw = jnp.concatenate([cwg_ref[...], cwu_ref[...]], axis=1)
    cb = jnp.concatenate([cbg_ref[...], cbu_ref[...]], axis=1)

    def gated(c):
        gate, up = c[:, :tn], c[:, tn:]
        return (gate * _sigmoid(gate) * up).astype(act_ref.dtype)

    if tiles_per_seq >= 1:
        @pl.when(pl.program_id(1) % tiles_per_seq == 0)
        def _():
            buf_ref[pad - 2:pad, 0:tn] = sg_ref[0]
            buf_ref[pad - 2:pad, tn:2 * tn] = su_ref[0]
        rs = tm // n_sub
        zs = [jnp.dot(h_ref[r * rs:(r + 1) * rs, :], wq_ref[...], preferred_element_type=F32)
              for r in range(n_sub)]
        for r, z in enumerate(zs):
            base = pad + r * rs
            buf_ref[base:base + rs, :] = z
            zext = buf_ref[base - pad:base + rs, :]
            z1 = pltpu.roll(zext, 1, 0)[pad:]
            z2 = pltpu.roll(zext, 2, 0)[pad:]
            c = cb + cw[0:1] * z2 + cw[1:2] * z1 + cw[2:3] * z
            act_ref[r * rs:(r + 1) * rs, :] = gated(c)
        tail = buf_ref[pad + tm - 2:pad + tm, :]
        zg_ref[0] = tail[:, :tn]
        zu_ref[0] = tail[:, tn:]
        buf_ref[pad - 2:pad, :] = tail
    else:
        buf_ref[0:pad, :] = jnp.zeros((pad, 2 * tn), F32)
        buf_ref[pad:pad + tm, :] = jnp.dot(h_ref[...], wq_ref[...], preferred_element_type=F32)
        rowi = lax.broadcasted_iota(jnp.int32, (seq_len, 2 * tn), 0)
        for s in range(tm // seq_len):
            base = pad + s * seq_len
            st0 = jnp.concatenate([sg_ref[s, 0:1, :], su_ref[s, 0:1, :]], axis=1)
            st1 = jnp.concatenate([sg_ref[s, 1:2, :], su_ref[s, 1:2, :]], axis=1)
            z0 = buf_ref[base:base + seq_len, :]
            p1 = jnp.where(rowi == 0, st1, buf_ref[base - 1:base - 1 + seq_len, :])
            p2 = jnp.where(rowi == 0, st0, jnp.where(rowi == 1, st1, buf_ref[base - 2:base - 2 + seq_len, :]))
            act_ref[s * seq_len:(s + 1) * seq_len, :] = gated(cb + cw[0:1] * p2 + cw[1:2] * p1 + cw[2:3] * z0)
            tail = buf_ref[base + seq_len - 2:base + seq_len, :]
            zg_ref[s] = tail[:, :tn]
            zu_ref[s] = tail[:, tn:]


def ffn_up(h, w_up, conv_w, conv_b, state, seq_len):
    m, d = h.shape
    f = w_up.shape[1] // 2
    tm = _tile(m, FFN_UP_TILE_ROWS, SUBLANE)
    tn = _tile(f, 256, LANE)
    nj = f // tn
    if seq_len >= tm:
        assert seq_len % tm == 0
        tiles_per_seq = seq_len // tm
        n_state = 1
        state_map_g = lambda j, i: (i // tiles_per_seq, 0, j)
        state_map_u = lambda j, i: (i // tiles_per_seq, 0, j + nj)
        n_last = m // tm
    else:
        assert tm % seq_len == 0 and seq_len % SUBLANE == 0
        tiles_per_seq = 0
        n_state = tm // seq_len
        state_map_g = lambda j, i: (i, 0, j)
        state_map_u = lambda j, i: (i, 0, j + nj)
        n_last = m // seq_len
    kern = functools.partial(_ffn_up_kernel, seq_len=seq_len, tiles_per_seq=tiles_per_seq,
                             n_sub=max(1, tm // FFN_UP_SUB_ROWS))
    cb = conv_b.reshape(1, 2 * f)
    act, zg, zu = pl.pallas_call(
        kern,
        grid=(nj, m // tm),
        in_specs=[
            pl.BlockSpec((tm, d), lambda j, i: (i, 0)),
            pl.BlockSpec((d, tn), lambda j, i: (0, j)),
            pl.BlockSpec((d, tn), lambda j, i: (0, j + nj)),
            pl.BlockSpec((CONV_WIDTH, tn), lambda j, i: (0, j)),
            pl.BlockSpec((CONV_WIDTH, tn), lambda j, i: (0, j + nj)),
            pl.BlockSpec((1, tn), lambda j, i: (0, j)),
            pl.BlockSpec((1, tn), lambda j, i: (0, j + nj)),
            pl.BlockSpec((n_state, 2, tn), state_map_g),
            pl.BlockSpec((n_state, 2, tn), state_map_u),
        ],
        out_specs=[
            pl.BlockSpec((tm, tn), lambda j, i: (i, j)),
            pl.BlockSpec((n_state, 2, tn), lambda j, i: (i, 0, j)),
            pl.BlockSpec((n_state, 2, tn), lambda j, i: (i, 0, j)),
        ],
        out_shape=[
            jax.ShapeDtypeStruct((m, f), BF16),
            jax.ShapeDtypeStruct((n_last, 2, f), F32),
            jax.ShapeDtypeStruct((n_last, 2, f), F32),
        ],
        scratch_shapes=[pltpu.VMEM((d, 2 * tn), BF16), pltpu.VMEM((tm + SUBLANE, 2 * tn), F32)],
        compiler_params=_params("arbitrary", "arbitrary", vmem_limit_bytes=FFN_UP_VMEM_LIMIT_BYTES),
        name="ffn_up",
    )(h, w_up, w_up, conv_w, conv_w, cb, cb, state, state)
    zlast = jnp.concatenate([zg, zu], axis=-1)
    if tiles_per_seq > 1:
        zlast = zlast[tiles_per_seq - 1::tiles_per_seq]
    return act, zlast


def _ffn_down_kernel(a_ref, b_ref, r_ref, o_ref):
    o_ref[...] = r_ref[...] + jnp.dot(a_ref[...], b_ref[...], preferred_element_type=F32)


def ffn_down(a, b, res):
    m, k = a.shape
    n = b.shape[1]
    tm = _tile(m, 512, SUBLANE)
    tn = _tile(n, 512, LANE)
    return pl.pallas_call(
        _ffn_down_kernel,
        grid=(m // tm, n // tn),
        in_specs=[
            pl.BlockSpec((tm, k), lambda i, j: (i, 0)),
            pl.BlockSpec((k, tn), lambda i, j: (0, j)),
            pl.BlockSpec((tm, tn), lambda i, j: (i, j)),
        ],
        out_specs=pl.BlockSpec((tm, tn), lambda i, j: (i, j)),
        out_shape=jax.ShapeDtypeStruct((m, n), F32),
        compiler_params=_params("parallel", "arbitrary"),
        name="ffn_down",
    )(a, b, res)


def _in_offsets(d):
    da = d // 2
    n_heads = d // 256
    n_kv = n_heads // Q_PER_KV
    sizes = (da, da, n_heads * HEAD_DIM, n_kv * HEAD_DIM, n_kv * HEAD_DIM,
             N_IDX_HEADS * IDX_DIM, IDX_DIM, N_IDX_HEADS, d, d)
    offs = [0]
    for s in sizes:
        offs.append(offs[-1] + s)
    return offs


def _layer(x, pos, seq_len, w, past, conv_state, want_vn):
    bsz, t, d = x.shape
    m = bsz * t
    x2 = x.reshape(m, d)
    n_heads = d // 256
    n_kv = n_heads // Q_PER_KV
    offs = _in_offsets(d)
    w_in = w["in_t"]
    assert offs[8] - offs[6] <= LANE and w["in_gates_t"].shape[0] == 2 * d

    h = rmsnorm(x2, w["norm_attn_g"], BF16)

    rows = max(t, _tile(m, 1024, SUBLANE))
    rope_h = (HEAD_DIM // 8,) + _rope_tables(pos, HEAD_DIM, HEAD_DIM // 4, rows)
    rope_i = (IDX_DIM // 8,) + _rope_tables(pos, IDX_DIM, IDX_DIM // 4, rows)
    ci, s1i, s2i = rope_i[1:]
    lane = jnp.arange(LANE)
    is_w = (lane >= IDX_DIM) & (lane < IDX_DIM + N_IDX_HEADS)
    w_scale = N_IDX_HEADS ** -0.5 * IDX_DIM ** -0.5
    rope_kw = (IDX_DIM // 8,
               jnp.where(is_w, w_scale, jnp.where(lane < IDX_DIM, ci, 1.0)).astype(F32),
               jnp.where(lane < IDX_DIM, s1i, 0.0), jnp.where(lane < IDX_DIM, s2i, 0.0))

    uv = project(h, w_in, offs[0], offs[2] - offs[0], F32)
    q = project(h, w_in, offs[2], offs[3] - offs[2], BF16, rope_h)
    qi = project(h, w_in, offs[5], offs[6] - offs[5], BF16, rope_i)
    k, v, kw = project_kv(h, w_in, offs[3], offs[4], offs[6], offs[4] - offs[3], rope_h, rope_kw)
    gates = project(h, w["in_gates_t"], 0, 2 * d, F32)

    chunk_rows = min(t, GMLP_CHUNK)
    a_out, vn = gmlp(uv, w["gmlp_norm_g"], w["gmlp_wm"](chunk_rows), w["gmlp_bias"](chunk_rows), want_vn)

    n_valid = t if past is None else past[1].shape[2] + t
    b_out = dsa(qi.reshape(bsz, t, -1), kw.reshape(bsz, t, LANE), q.reshape(bsz, t, -1),
                k.reshape(bsz, t * n_kv, HEAD_DIM), v.reshape(bsz, t * n_kv, HEAD_DIM), past,
                tq=min(DSA_QUERY_BLOCK, t), topk=min(TOPK_MAX, n_valid // 4))

    y = merge(a_out, b_out.reshape(m, n_heads * HEAD_DIM), w["a"], w["b"], gates)
    x2 = matmul_residual(y, w["o"], x2)

    hf = rmsnorm(x2, w["norm_ffn_g"], BF16)
    act, zlast = ffn_up(hf, w["up"], w["conv_w"], w["conv_b"], conv_state, seq_len)
    x2 = ffn_down(act, w["down"], x2)
    kidx = kw[:, :IDX_DIM].reshape(bsz, t, IDX_DIM)
    return (x2.reshape(bsz, t, d), k.reshape(bsz, t, n_kv, HEAD_DIM), v.reshape(bsz, t, n_kv, HEAD_DIM),
            kidx, zlast, vn)


def kernel(x_prompt, x_sample, cache_k, cache_v, cache_kidx, state_ffn_conv, norm_attn_g, w_in, gmlp_norm_g, gmlp_ws, gmlp_b, w_branch_a, w_branch_b, w_out, norm_ffn_g, w_up, conv_w, conv_b, w_down, norm_final_g):
    bsz, s, d = x_prompt.shape
    dbsz, t, _ = x_sample.shape
    depth = w_in.shape[0]
    p_len = cache_k.shape[2]
    da = d // 2
    f2 = w_up.shape[2]
    assert s % GMLP_CHUNK == 0 and GMLP_CHUNK % t == 0 and s % CHUNK == 0

    pos_p = jnp.arange(s, dtype=jnp.int32)
    pos_s = p_len + jnp.arange(t, dtype=jnp.int32)
    ci = jnp.arange(GMLP_CHUNK)
    chunk_mask = (ci[None, :] // CHUNK) <= (ci[:, None] // CHUNK)
    cache_k4 = cache_k.reshape(depth, dbsz, -1, HEAD_DIM)
    cache_v4 = cache_v.reshape(depth, dbsz, -1, HEAD_DIM)
    in_offs = _in_offsets(d)
    assert in_offs[-1] == w_in.shape[2]

    xp, xs = x_prompt, x_sample
    outs = [[] for _ in range(9)]
    for l in range(depth):
        wm_full = jnp.where(chunk_mask[None], gmlp_ws[l], 0.0)
        bias_rows = jnp.repeat(jnp.transpose(gmlp_b[l]), da // G_A, axis=1)

        def gmlp_wm(rows, wm_full=wm_full):
            reps = GMLP_CHUNK // rows
            blk = wm_full[:, :rows, :rows]
            eye = jnp.eye(reps, dtype=F32)
            return jnp.einsum("ab,gij->gaibj", eye, blk).reshape(G_A, GMLP_CHUNK, GMLP_CHUNK).astype(BF16)

        def gmlp_bias(rows, bias_rows=bias_rows):
            return jnp.tile(bias_rows[:rows], (GMLP_CHUNK // rows, 1))

        w = dict(
            norm_attn_g=norm_attn_g[l], gmlp_norm_g=gmlp_norm_g[l], norm_ffn_g=norm_ffn_g[l],
            gmlp_wm=gmlp_wm, gmlp_bias=gmlp_bias,
            a=w_branch_a[l], b=w_branch_b[l], up=w_up[l], conv_w=conv_w[l], conv_b=conv_b[l],
            o=w_out[l].astype(BF16), down=w_down[l].astype(BF16),
        )
        w["in_t"] = jnp.swapaxes(w_in[l], 0, 1).astype(BF16)
        w["in_gates_t"] = w["in_t"][in_offs[8]:]

        xp, kp, vp, kip, cp, _ = _layer(xp, pos_p, s, w, None, jnp.zeros((bsz, CONV_WIDTH - 1, f2), F32),
                                        False)
        past = (l, cache_kidx, cache_k4, cache_v4)
        xs, ks, vs, kis, cs, gv = _layer(xs, pos_s, t, w, past, state_ffn_conv[l], True)
        for lst, val in zip(outs, (kp, vp, kip, cp, ks, vs, kis, cs, gv.reshape(dbsz, t, da))):
            lst.append(val)

    y_prompt = rmsnorm(xp.reshape(bsz * s, d), norm_final_g, F32).reshape(bsz, s, d)
    y_sample = rmsnorm(xs.reshape(dbsz * t, d), norm_final_g, F32).reshape(dbsz, t, d)
    stack = lambda o: o[0][None] if depth == 1 else jnp.stack(o)
    return (y_prompt, y_sample) + tuple(stack(o) for o in outs)
```

```python
import functools
import math

import jax
import jax.numpy as jnp
from jax import lax
from jax.experimental import pallas as pl
from jax.experimental.pallas import tpu as pltpu

CHUNK = 64
GMLP_CHUNK = 128
G_A = 8
HEAD_DIM = 128
Q_PER_KV = 4
N_IDX_HEADS = 16
IDX_DIM = 64
TOPK_MAX = 256
ROPE_THETA = 500000.0
CONV_WIDTH = 3
EPS = 1e-6

LANE = 128
SUBLANE = 8
VMEM_LIMIT_BYTES = 56 * 1024 * 1024

FFN_UP_TILE_ROWS = 2048
FFN_UP_SUB_ROWS = 512
FFN_UP_VMEM_LIMIT_BYTES = 60 * 1024 * 1024
DSA_PREFIX_CLASSES = 4
GMLP_CHUNKS_PER_STEP = 4
DSA_QUERY_BLOCK = 128
TOPK_STEP_SURPLUS = 2.0
TOPK_PROBES_PER_CHECK = 3

BF16 = jnp.bfloat16
F32 = jnp.float32
NT_DIMS = (((1,), (1,)), ((), ()))


def _params(*semantics, vmem_limit_bytes=VMEM_LIMIT_BYTES):
    return pltpu.CompilerParams(dimension_semantics=semantics, vmem_limit_bytes=vmem_limit_bytes)


def _tile(n, pref, align):
    if n <= pref:
        return n
    t = (pref // align) * align
    while t >= align:
        if n % t == 0:
            return t
        t -= align
    raise ValueError(f"no {align}-aligned tile of {n} below {pref}")


def _first_row_tile():
    return pl.program_id(1) == 0


def _sigmoid(x):
    return 0.5 * jnp.tanh(0.5 * x) + 0.5


def _rmsnorm_kernel(x_ref, g_ref, o_ref):
    x = x_ref[...]
    ms = jnp.mean(x * x, axis=-1, keepdims=True)
    o_ref[...] = (x * lax.rsqrt(ms + EPS) * g_ref[...]).astype(o_ref.dtype)


def rmsnorm(x, g, out_dtype):
    n, d = x.shape
    tr = _tile(n, 512, SUBLANE)
    return pl.pallas_call(
        _rmsnorm_kernel,
        grid=(n // tr,),
        in_specs=[pl.BlockSpec((tr, d), lambda i: (i, 0)), pl.BlockSpec((1, d), lambda i: (0, 0))],
        out_specs=pl.BlockSpec((tr, d), lambda i: (i, 0)),
        out_shape=jax.ShapeDtypeStruct((n, d), out_dtype),
        compiler_params=_params("parallel"),
        name="rmsnorm",
    )(x, g.reshape(1, d))


def _rope_slab(x, c, s1, s2, shift):
    return x * c + pltpu.roll(x, LANE - shift, 1) * s1 + pltpu.roll(x, shift, 1) * s2


def _proj_kernel(a_ref, b_ref, *rest, rope_shift):
    acc = lax.dot_general(a_ref[...], b_ref[...], NT_DIMS, preferred_element_type=F32)
    if rope_shift:
        c_ref, s1_ref, s2_ref, o_ref = rest
        c, s1, s2 = c_ref[...], s1_ref[...], s2_ref[...]
        for h in range(acc.shape[1] // LANE):
            sl = slice(h * LANE, (h + 1) * LANE)
            o_ref[:, sl] = _rope_slab(acc[:, sl], c, s1, s2, rope_shift).astype(o_ref.dtype)
    else:
        (o_ref,) = rest
        o_ref[...] = acc.astype(o_ref.dtype)


def project(a, wt, col0, ncols, out_dtype, rope=None):
    m, k = a.shape
    tm = _tile(m, 1024, SUBLANE)
    tn = _tile(math.gcd(ncols, col0), 1024, LANE)
    assert col0 % tn == 0 and ncols % tn == 0
    j0 = col0 // tn
    in_specs = [pl.BlockSpec((tm, k), lambda j, i: (i, 0)), pl.BlockSpec((tn, k), lambda j, i: (j + j0, 0))]
    args = [a, wt]
    shift = 0
    if rope is not None:
        shift, tabs = rope[0], rope[1:]
        reps = tabs[0].shape[0] // tm
        assert reps * tm == tabs[0].shape[0]
        for t in tabs:
            in_specs.append(pl.BlockSpec((tm, LANE), lambda j, i: (i % reps, 0)))
            args.append(t)
    return pl.pallas_call(
        functools.partial(_proj_kernel, rope_shift=shift),
        grid=(ncols // tn, m // tm),
        in_specs=in_specs,
        out_specs=pl.BlockSpec((tm, tn), lambda j, i: (i, j)),
        out_shape=jax.ShapeDtypeStruct((m, ncols), out_dtype),
        compiler_params=_params("parallel", "arbitrary"),
        name="project",
    )(*args)


def _proj_kv_kernel(a_ref, b_ref, ck_ref, s1k_ref, s2k_ref, cw_ref, s1w_ref, s2w_ref,
                    k_ref, v_ref, kw_ref, *, shift_k, shift_w):
    g = pl.program_id(1)
    tm = a_ref.shape[0]
    n_heads = b_ref.shape[0] // LANE

    def matmul(rows):
        return lax.dot_general(a_ref[...], b_ref[0:rows, :], NT_DIMS, preferred_element_type=F32)

    @pl.when(g == 0)
    def _():
        acc = matmul(n_heads * LANE)
        c, s1, s2 = ck_ref[...], s1k_ref[...], s2k_ref[...]
        for h in range(n_heads):
            k_ref[pl.ds(h, tm, stride=n_heads), :] = _rope_slab(acc[:, h * LANE:(h + 1) * LANE], c, s1, s2, shift_k)

    @pl.when(g == 1)
    def _():
        acc = matmul(n_heads * LANE)
        for h in range(n_heads):
            v_ref[pl.ds(h, tm, stride=n_heads), :] = acc[:, h * LANE:(h + 1) * LANE]

    @pl.when(g == 2)
    def _():
        kw_ref[...] = _rope_slab(matmul(LANE), cw_ref[...], s1w_ref[...], s2w_ref[...], shift_w)


def project_kv(a, wt, col_k, col_v, col_kw, width, rope_k, rope_kw):
    m, kdim = a.shape
    tm = _tile(m, 1024, SUBLANE)
    assert col_k % width == 0 and col_v % width == 0 and col_kw % width == 0
    assert col_kw + width <= wt.shape[0] and width % LANE == 0
    blocks = (col_k // width, col_v // width, col_kw // width)
    heads = width // LANE

    def w_map(i, g):
        return (jnp.where(g == 0, blocks[0], jnp.where(g == 1, blocks[1], blocks[2])), 0)

    tabs = rope_k[1:] + rope_kw[1:]
    reps = tabs[0].shape[0] // tm
    assert reps * tm == tabs[0].shape[0]
    return pl.pallas_call(
        functools.partial(_proj_kv_kernel, shift_k=rope_k[0], shift_w=rope_kw[0]),
        grid=(m // tm, 3),
        in_specs=[pl.BlockSpec((tm, kdim), lambda i, g: (i, 0)), pl.BlockSpec((width, kdim), w_map)]
        + [pl.BlockSpec((tm, LANE), lambda i, g: (i % reps, 0)) for _ in tabs],
        out_specs=[
            pl.BlockSpec((tm * heads, LANE), lambda i, g: (i, 0)),
            pl.BlockSpec((tm * heads, LANE), lambda i, g: (i, 0)),
            pl.BlockSpec((tm, LANE), lambda i, g: (i, 0)),
        ],
        out_shape=[
            jax.ShapeDtypeStruct((m * heads, LANE), F32),
            jax.ShapeDtypeStruct((m * heads, LANE), F32),
            jax.ShapeDtypeStruct((m, LANE), F32),
        ],
        compiler_params=_params("parallel", "arbitrary"),
        name="project_kv",
    )(a, wt, *tabs)


def _rope_tables(pos, head_dim, rot_dim, rows):
    half = rot_dim // 2
    inv_freq = ROPE_THETA ** (-jnp.arange(half, dtype=F32) / half)
    ang = pos.astype(F32)[:, None] * inv_freq[None, :]
    cos, sin = jnp.cos(ang), jnp.sin(ang)
    t = pos.shape[0]
    zeros_h = jnp.zeros((t, half), F32)
    rest0 = jnp.zeros((t, head_dim - rot_dim), F32)
    c = jnp.concatenate([cos, cos, jnp.ones((t, head_dim - rot_dim), F32)], axis=1)
    s1 = jnp.concatenate([-sin, zeros_h, rest0], axis=1)
    s2 = jnp.concatenate([zeros_h, sin, rest0], axis=1)
    reps_l = LANE // head_dim
    reps_r = rows // t
    return tuple(jnp.tile(x, (reps_r, reps_l)) for x in (c, s1, s2))


def _gmlp_kernel(u_ref, v_ref, g_ref, wm_ref, b_ref, a_ref, *vn_out, groups):
    gw = v_ref.shape[1] // groups
    for c in range(v_ref.shape[0] // GMLP_CHUNK):
        rows = slice(c * GMLP_CHUNK, (c + 1) * GMLP_CHUNK)
        v = v_ref[rows, :]
        vn = v * lax.rsqrt(jnp.mean(v * v, axis=-1, keepdims=True) + EPS) * g_ref[...]
        if vn_out:
            vn_out[0][rows, :] = vn
        vb = vn.astype(BF16)
        for g in range(groups):
            sl = slice(g * gw, (g + 1) * gw)
            s = jnp.dot(wm_ref[g], vb[:, sl], preferred_element_type=F32) + b_ref[:, sl]
            a_ref[rows, sl] = (u_ref[rows, sl] * s).astype(a_ref.dtype)


def gmlp(uv, g_norm, wm, bias, want_vn):
    m, d2 = uv.shape
    da = d2 // 2
    tc = GMLP_CHUNK * math.gcd(m // GMLP_CHUNK, GMLP_CHUNKS_PER_STEP)
    out_shape = [jax.ShapeDtypeStruct((m, da), BF16)]
    out_specs = [pl.BlockSpec((tc, da), lambda i: (i, 0))]
    if want_vn:
        out_shape.append(jax.ShapeDtypeStruct((m, da), F32))
        out_specs.append(pl.BlockSpec((tc, da), lambda i: (i, 0)))
    res = pl.pallas_call(
        functools.partial(_gmlp_kernel, groups=wm.shape[0]),
        grid=(m // tc,),
        in_specs=[
            pl.BlockSpec((tc, da), lambda i: (i, 0)),
            pl.BlockSpec((tc, da), lambda i: (i, 1)),
            pl.BlockSpec((1, da), lambda i: (0, 0)),
            pl.BlockSpec(wm.shape, lambda i: (0, 0, 0)),
            pl.BlockSpec((GMLP_CHUNK, da), lambda i: (0, 0)),
        ],
        out_specs=out_specs,
        out_shape=out_shape,
        compiler_params=_params("parallel"),
        name="gmlp",
    )(uv, uv, g_norm.reshape(1, da), wm, bias)
    return res if want_vn else (res[0], None)


LOG2_E = 1.4426950408889634
ROW_REDUCE_GROUP = 64
ATTN_ROW_GROUP = 32


def _reduce_rows(x, op):
    pair = {jnp.sum: jnp.add, jnp.min: jnp.minimum, jnp.max: jnp.maximum}[op]
    rows = x.shape[0]
    if rows % ROW_REDUCE_GROUP == 0:
        parts = [x[i:i + ROW_REDUCE_GROUP] for i in range(0, rows, ROW_REDUCE_GROUP)]
        while len(parts) > 1:
            parts = [pair(parts[i], parts[i + 1]) if i + 1 < len(parts) else parts[i]
                     for i in range(0, len(parts), 2)]
        x = parts[0]
    return op(x, axis=0, keepdims=True)

def _dsa_select_attend(qi_ref, kwq_ref, q_ref, klo_ref, khi_ref, ks_ref, vt_ref, o_ref, bias_ref,
                       lg_ref, p_ref, *,
                       s_c, tq, n_kv, topk, row0, n_valid, scale):
    n_tiles = s_c // LANE
    reps = LANE // tq

    def rep_rows(x):
        return x if reps == 1 else jnp.concatenate([x] * reps, axis=0)

    w_t = jnp.transpose(rep_rows(kwq_ref[...]))
    klo = klo_ref[0:s_c, :]
    khi = khi_ref[0:s_c, :]

    score = jnp.zeros((s_c, LANE), F32)
    for pp in range(N_IDX_HEADS // 4):
        qp2 = jnp.concatenate([rep_rows(qi_ref[:, (2 * pp + i) * LANE:(2 * pp + i + 1) * LANE])
                               for i in range(2)], axis=0)
        rel_lo = lax.dot_general(klo, qp2, NT_DIMS, preferred_element_type=F32)
        rel_hi = lax.dot_general(khi, qp2, NT_DIMS, preferred_element_type=F32)
        for i in range(2):
            r0 = IDX_DIM + 2 * (2 * pp + i)
            score = (score + jnp.maximum(rel_lo[:, i * LANE:(i + 1) * LANE], 0.0) * w_t[r0:r0 + 1, :]
                     + jnp.maximum(rel_hi[:, i * LANE:(i + 1) * LANE], 0.0) * w_t[r0 + 1:r0 + 2, :])

    kpos = lax.broadcasted_iota(jnp.int32, (s_c, LANE), 0)
    if row0 is not None:
        qrow = row0 + jnp.bitwise_and(lax.broadcasted_iota(jnp.int32, (s_c, LANE), 1), tq - 1)
        adm = kpos < (jnp.right_shift(qrow, CHUNK.bit_length() - 1) + 1) * CHUNK
    else:
        adm = kpos < n_valid
    score = jnp.where(adm, score, -jnp.inf)
    bias_ref[0:s_c, :] = score

    kf = float(topk)
    lo0 = _reduce_rows(jnp.where(adm, score, jnp.inf), jnp.min)
    hi0 = _reduce_rows(score, jnp.max)
    c_lo0 = _reduce_rows(jnp.where(adm, 1.0, 0.0), jnp.sum)
    c_hi0 = jnp.zeros((1, LANE), F32)

    def probe(t):
        cnt = jnp.zeros((ROW_REDUCE_GROUP, LANE), F32)
        nxt = jnp.full((ROW_REDUCE_GROUP, LANE), jnp.inf, F32)
        for i in range(0, s_c, ROW_REDUCE_GROUP):
            sc = bias_ref[i:i + ROW_REDUCE_GROUP, :]
            above = sc > t
            cnt = cnt + jnp.where(above, 1.0, 0.0)
            nxt = jnp.minimum(nxt, jnp.where(above, sc, jnp.inf))
        return jnp.sum(cnt, axis=0, keepdims=True), jnp.min(nxt, axis=0, keepdims=True)

    def active_of(lo, hi, c_lo):
        return (c_lo > kf) & (hi > lo)

    def cond(carry):
        lo, hi, c_lo, _ = carry
        return jnp.max(jnp.where(active_of(lo, hi, c_lo), 1.0, 0.0)) > 0.0

    def body(carry):
        for _ in range(TOPK_PROBES_PER_CHECK):
            carry = advance(carry)
        return carry

    def advance(carry):
        lo, hi, c_lo, c_hi = carry
        act = active_of(lo, hi, c_lo)
        mid = 0.5 * lo + 0.5 * hi
        step = (c_lo - kf <= TOPK_STEP_SURPLUS) | (mid <= lo) | (mid >= hi)
        t = jnp.where(step, lo, mid)
        cnt, nxt = probe(t)
        take = act & (cnt >= kf)
        drop = act & (cnt < kf)
        return (jnp.where(take, nxt, lo), jnp.where(drop, t, hi),
                jnp.where(take, cnt, c_lo), jnp.where(drop, cnt, c_hi))

    lo, hi, c_lo, c_hi = lax.while_loop(cond, body, (lo0, hi0, c_lo0, c_hi0))
    tie = c_lo > kf
    any_tie = jnp.max(jnp.where(tie, 1.0, 0.0)) > 0.0

    @pl.when(jnp.logical_not(any_tie))
    def _():
        for jt in range(n_tiles):
            sl = slice(jt * LANE, (jt + 1) * LANE)
            bias_ref[sl, :] = jnp.where(bias_ref[sl, :] >= lo, 0.0, -jnp.inf)

    @pl.when(any_tie)
    def _():
        quota = kf - c_hi
        tri = (lax.broadcasted_iota(jnp.int32, (LANE, LANE), 1)
               <= lax.broadcasted_iota(jnp.int32, (LANE, LANE), 0)).astype(F32).astype(BF16)
        before = jnp.zeros((1, LANE), F32)
        for jt in range(n_tiles):
            sl = slice(jt * LANE, (jt + 1) * LANE)
            sc = bias_ref[sl, :]
            cand = jnp.where((sc >= lo) & (sc <= hi), 1.0, 0.0)
            rank = jnp.dot(tri, cand.astype(BF16), preferred_element_type=F32) + before
            keep = (sc > hi) | ((cand > 0.0) & (rank <= quota))
            bias_ref[sl, :] = jnp.where(tie, jnp.where(keep, 0.0, -jnp.inf),
                                        jnp.where(sc >= lo, 0.0, -jnp.inf))
            before = before + jnp.sum(cand, axis=0, keepdims=True)

    width = Q_PER_KV * tq
    n_slabs = width // LANE
    grp = ATTN_ROW_GROUP
    def put_logits(n):
        qn = jnp.concatenate(
            [q_ref[:, (n * Q_PER_KV + g) * HEAD_DIM:(n * Q_PER_KV + g + 1) * HEAD_DIM]
             for g in range(Q_PER_KV)], axis=0)
        lg_ref[n % 2, 0:s_c, :] = lax.dot_general(
            ks_ref[0:s_c, n * HEAD_DIM:(n + 1) * HEAD_DIM], qn, NT_DIMS,
            preferred_element_type=F32) * (scale * LOG2_E)

    put_logits(0)
    for n in range(n_kv):
        if n + 1 < n_kv:
            put_logits(n + 1)
        ksl = slice(n * HEAD_DIM, (n + 1) * HEAD_DIM)
        lgn = lg_ref.at[n % 2]
        pn = p_ref.at[n % 2]
        mpart = jnp.full((grp, width), -jnp.inf, F32)
        for i in range(0, s_c, grp):
            b = bias_ref[i:i + grp, :]
            lg = lgn[i:i + grp, :] + (b if n_slabs == 1 else jnp.concatenate([b] * n_slabs, axis=1))
            lgn[i:i + grp, :] = lg
            mpart = jnp.maximum(mpart, lg)
        mx = jnp.max(mpart, axis=0, keepdims=True)
        dpart = jnp.zeros((grp, width), F32)
        for i in range(0, s_c, grp):
            p = jnp.exp2(lgn[i:i + grp, :] - mx)
            dpart = dpart + p
            pn[i:i + grp, :] = p.astype(BF16)
        den = jnp.sum(dpart, axis=0, keepdims=True)
        o_t = jnp.dot(vt_ref[ksl, 0:s_c], pn[0:s_c, :], preferred_element_type=F32) / den
        for sb in range(n_slabs):
            o = jnp.transpose(o_t[:, sb * LANE:(sb + 1) * LANE])
            for r in range(reps):
                g = sb * reps + r
                hsl = slice((n * Q_PER_KV + g) * HEAD_DIM, (n * Q_PER_KV + g + 1) * HEAD_DIM)
                o_ref[:, hsl] = o[r * tq:(r + 1) * tq].astype(o_ref.dtype)


def _dsa_kernel(qi_ref, kwq_ref, q_ref, kw_ref, k_ref, v_ref, *rest, tq, t_new, p_len, n_classes, topk):
    if p_len:
        (pki_ref, pk_ref, pv_ref, o_ref, klo_ref, khi_ref, ks_ref, vt_ref, bias_ref, lg_ref, p_ref,
         tmp_ref) = rest
    else:
        o_ref, klo_ref, khi_ref, ks_ref, vt_ref, bias_ref, lg_ref, p_ref = rest
        pk_ref = pv_ref = None
    s_pad = ks_ref.shape[0]
    n_valid = p_len + t_new
    n_kv = ks_ref.shape[1] // HEAD_DIM
    j = pl.program_id(1)

    def head_rows(src_ref, rows, n):
        return src_ref[pl.ds(n, rows, stride=n_kv), :]

    @pl.when(j == 0)
    def _():
        lane = lax.broadcasted_iota(jnp.int32, (t_new, LANE), 1)
        new_lo = jnp.where(lane < IDX_DIM, kw_ref[...], 0.0)
        if p_len:
            tmp_ref[...] = jnp.zeros(tmp_ref.shape, F32)
            tmp_ref[0:p_len, 0:IDX_DIM] = pki_ref[...]
            tmp_ref[p_len:n_valid, :] = new_lo
            lo = tmp_ref[...]
        else:
            lo = new_lo
        klo_ref[...] = lo.astype(BF16)
        khi_ref[...] = pltpu.roll(lo, IDX_DIM, 1).astype(BF16)
        tail = jnp.zeros((s_pad - n_valid, HEAD_DIM), F32)
        for n in range(n_kv):
            csl = slice(n * HEAD_DIM, (n + 1) * HEAD_DIM)
            new_k = head_rows(k_ref, t_new, n)
            new_v = head_rows(v_ref, t_new, n)
            if s_pad > n_valid:
                new_k = jnp.concatenate([new_k, tail], axis=0)
                new_v = jnp.concatenate([new_v, tail], axis=0)
            if p_len:
                ks_ref[0:p_len, csl] = head_rows(pk_ref, p_len, n).astype(BF16)
                vt_ref[csl, 0:p_len] = jnp.transpose(head_rows(pv_ref, p_len, n)).astype(BF16)
            ks_ref[p_len:s_pad, csl] = new_k.astype(BF16)
            vt_ref[csl, p_len:s_pad] = jnp.transpose(new_v).astype(BF16)

    common = dict(tq=tq, n_kv=n_kv, topk=topk, n_valid=n_valid, scale=HEAD_DIM ** -0.5)
    refs = (qi_ref, kwq_ref, q_ref, klo_ref, khi_ref, ks_ref, vt_ref, o_ref, bias_ref, lg_ref, p_ref)
    if p_len:
        _dsa_select_attend(*refs, s_c=s_pad, row0=None, **common)
    else:
        per_class = pl.num_programs(1) // n_classes
        for c in range(n_classes):
            @pl.when(j // per_class == c)
            def _(c=c):
                _dsa_select_attend(*refs, s_c=(c + 1) * (s_pad // n_classes), row0=j * tq, **common)


def dsa(qi, kw, q, k, v, past, *, tq, topk):
    b, t, dq = q.shape
    n_kv = k.shape[1] // t
    dkv = n_kv * HEAD_DIM
    nq = t // tq
    row = lambda i, j: (i, j, 0)
    full = lambda i, j: (i, 0, 0)
    in_specs = [
        pl.BlockSpec((None, tq, qi.shape[2]), row),
        pl.BlockSpec((None, tq, LANE), row),
        pl.BlockSpec((None, tq, dq), row),
        pl.BlockSpec((None, t, LANE), full),
        pl.BlockSpec((None, t * n_kv, HEAD_DIM), full),
        pl.BlockSpec((None, t * n_kv, HEAD_DIM), full),
    ]
    args = [qi, kw, q, kw, k, v]
    if past is None:
        p_len = 0
        s_pad = t
        n_classes = DSA_PREFIX_CLASSES if (nq % DSA_PREFIX_CLASSES == 0
                                           and t % (DSA_PREFIX_CLASSES * LANE) == 0) else 1
        assert tq % CHUNK == 0 and LANE % tq == 0
    else:
        layer, pki, pk, pv = past
        p_len = pki.shape[2]
        s_pad = -(-(p_len + t) // LANE) * LANE
        n_classes = 1
        assert nq == 1 and p_len % LANE == 0 and LANE % tq == 0 and Q_PER_KV * tq % LANE == 0
        cache = lambda i, j: (layer, i, 0, 0)
        in_specs += [pl.BlockSpec((None, None, p_len, IDX_DIM), cache),
                     pl.BlockSpec((None, None, p_len * n_kv, HEAD_DIM), cache),
                     pl.BlockSpec((None, None, p_len * n_kv, HEAD_DIM), cache)]
        args += [pki, pk, pv]
    scratch = [pltpu.VMEM((s_pad, LANE), BF16), pltpu.VMEM((s_pad, LANE), BF16),
               pltpu.VMEM((s_pad, dkv), BF16), pltpu.VMEM((dkv, s_pad), BF16),
               pltpu.VMEM((s_pad, LANE), F32),
               pltpu.VMEM((2, s_pad, Q_PER_KV * tq), F32), pltpu.VMEM((2, s_pad, Q_PER_KV * tq), BF16)]
    if past is not None:
        scratch.append(pltpu.VMEM((s_pad, LANE), F32))
    kern = functools.partial(_dsa_kernel, tq=tq, t_new=t, p_len=p_len, n_classes=n_classes, topk=topk)
    return pl.pallas_call(
        kern,
        grid=(b, nq),
        in_specs=in_specs,
        out_specs=pl.BlockSpec((None, tq, dq), row),
        out_shape=jax.ShapeDtypeStruct((b, t, dq), BF16),
        scratch_shapes=scratch,
        compiler_params=_params("arbitrary", "arbitrary"),
        name="dsa",
    )(*args)


def _merge_kernel(a_ref, b_ref, wa_ref, wb_ref, ga_ref, gb_ref, o_ref, waq_ref, wbq_ref):
    @pl.when(_first_row_tile())
    def _():
        waq_ref[...] = wa_ref[...].astype(BF16)
        wbq_ref[...] = wb_ref[...].astype(BF16)

    ya = jnp.dot(a_ref[...], waq_ref[...], preferred_element_type=F32)
    yb = jnp.dot(b_ref[...], wbq_ref[...], preferred_element_type=F32)
    y = _sigmoid(ga_ref[...]) * ya + _sigmoid(gb_ref[...]) * yb
    o_ref[...] = y.astype(o_ref.dtype)


def merge(a, b, wa, wb, gates):
    m, ka = a.shape
    kb = b.shape[1]
    d = wa.shape[1]
    tm = _tile(m, 1024, SUBLANE)
    tn = _tile(d, 512, LANE)
    nj = d // tn
    return pl.pallas_call(
        _merge_kernel,
        grid=(nj, m // tm),
        in_specs=[
            pl.BlockSpec((tm, ka), lambda j, i: (i, 0)),
            pl.BlockSpec((tm, kb), lambda j, i: (i, 0)),
            pl.BlockSpec((ka, tn), lambda j, i: (0, j)),
            pl.BlockSpec((kb, tn), lambda j, i: (0, j)),
            pl.BlockSpec((tm, tn), lambda j, i: (i, j)),
            pl.BlockSpec((tm, tn), lambda j, i: (i, j + nj)),
        ],
        out_specs=pl.BlockSpec((tm, tn), lambda j, i: (i, j)),
        out_shape=jax.ShapeDtypeStruct((m, d), BF16),
        scratch_shapes=[pltpu.VMEM((ka, tn), BF16), pltpu.VMEM((kb, tn), BF16)],
        compiler_params=_params("arbitrary", "arbitrary"),
        name="merge",
    )(a, b, wa, wb, gates, gates)


def _matmul_res_kernel(a_ref, b_ref, r_ref, o_ref):
    o_ref[...] = r_ref[...] + jnp.dot(a_ref[...], b_ref[...], preferred_element_type=F32)


def matmul_residual(a, b, res):
    m, k = a.shape
    n = b.shape[1]
    tm = _tile(m, 1024, SUBLANE)
    tn = _tile(n, 1024, LANE)
    return pl.pallas_call(
        _matmul_res_kernel,
        grid=(n // tn, m // tm),
        in_specs=[
            pl.BlockSpec((tm, k), lambda j, i: (i, 0)),
            pl.BlockSpec((k, tn), lambda j, i: (0, j)),
            pl.BlockSpec((tm, tn), lambda j, i: (i, j)),
        ],
        out_specs=pl.BlockSpec((tm, tn), lambda j, i: (i, j)),
        out_shape=jax.ShapeDtypeStruct((m, n), F32),
        compiler_params=_params("parallel", "arbitrary"),
        name="matmul_residual",
    )(a, b, res)


def _ffn_up_kernel(h_ref, wg_ref, wu_ref, cwg_ref, cwu_ref, cbg_ref, cbu_ref, sg_ref, su_ref,
                   act_ref, zg_ref, zu_ref, wq_ref, buf_ref, *, seq_len, tiles_per_seq, n_sub):
    tm = h_ref.shape[0]
    tn = act_ref.shape[1]
    pad = SUBLANE

    @pl.when(_first_row_tile())
    def _():
        wq_ref[:, 0:tn] = wg_ref[...].astype(BF16)
        wq_ref[:, tn:2 * tn] = wu_ref[...].astype(BF16)

    cw = jnp.concatenate([cwg_ref[...], cwu_ref[...]], axis=1)
    cb = jnp.concatenate([cbg_ref[...], cbu_ref[...]], axis=1)

    def gated(c):
        gate, up = c[:, :tn], c[:, tn:]
        return (gate * _sigmoid(gate) * up).astype(act_ref.dtype)

    if tiles_per_seq >= 1:
        @pl.when(pl.program_id(1) % tiles_per_seq == 0)
        def _():
            buf_ref[pad - 2:pad, 0:tn] = sg_ref[0]
            buf_ref[pad - 2:pad, tn:2 * tn] = su_ref[0]
        rs = tm // n_sub
        zs = [jnp.dot(h_ref[r * rs:(r + 1) * rs, :], wq_ref[...], preferred_element_type=F32)
              for r in range(n_sub)]
        for r, z in enumerate(zs):
            base = pad + r * rs
            buf_ref[base:base + rs, :] = z
            zext = buf_ref[base - pad:base + rs, :]
            z1 = pltpu.roll(zext, 1, 0)[pad:]
            z2 = pltpu.roll(zext, 2, 0)[pad:]
            c = cb + cw[0:1] * z2 + cw[1:2] * z1 + cw[2:3] * z
            act_ref[r * rs:(r + 1) * rs, :] = gated(c)
        tail = buf_ref[pad + tm - 2:pad + tm, :]
        zg_ref[0] = tail[:, :tn]
        zu_ref[0] = tail[:, tn:]
        buf_ref[pad - 2:pad, :] = tail
    else:
        buf_ref[0:pad, :] = jnp.zeros((pad, 2 * tn), F32)
        buf_ref[pad:pad + tm, :] = jnp.dot(h_ref[...], wq_ref[...], preferred_element_type=F32)
        rowi = lax.broadcasted_iota(jnp.int32, (seq_len, 2 * tn), 0)
        for s in range(tm // seq_len):
            base = pad + s * seq_len
            st0 = jnp.concatenate([sg_ref[s, 0:1, :], su_ref[s, 0:1, :]], axis=1)
            st1 = jnp.concatenate([sg_ref[s, 1:2, :], su_ref[s, 1:2, :]], axis=1)
            z0 = buf_ref[base:base + seq_len, :]
            p1 = jnp.where(rowi == 0, st1, buf_ref[base - 1:base - 1 + seq_len, :])
            p2 = jnp.where(rowi == 0, st0, jnp.where(rowi == 1, st1, buf_ref[base - 2:base - 2 + seq_len, :]))
            act_ref[s * seq_len:(s + 1) * seq_len, :] = gated(cb + cw[0:1] * p2 + cw[1:2] * p1 + cw[2:3] * z0)
            tail = buf_ref[base + seq_len - 2:base + seq_len, :]
            zg_ref[s] = tail[:, :tn]
            zu_ref[s] = tail[:, tn:]


def ffn_up(h, w_up, conv_w, conv_b, state, seq_len):
    m, d = h.shape
    f = w_up.shape[1] // 2
    tm = _tile(m, FFN_UP_TILE_ROWS, SUBLANE)
    tn = _tile(f, 256, LANE)
    nj = f // tn
    if seq_len >= tm:
        assert seq_len % tm == 0
        tiles_per_seq = seq_len // tm
        n_state = 1
        state_map_g = lambda j, i: (i // tiles_per_seq, 0, j)
        state_map_u = lambda j, i: (i // tiles_per_seq, 0, j + nj)
        n_last = m // tm
    else:
        assert tm % seq_len == 0 and seq_len % SUBLANE == 0
        tiles_per_seq = 0
        n_state = tm // seq_len
        state_map_g = lambda j, i: (i, 0, j)
        state_map_u = lambda j, i: (i, 0, j + nj)
        n_last = m // seq_len
    kern = functools.partial(_ffn_up_kernel, seq_len=seq_len, tiles_per_seq=tiles_per_seq,
                             n_sub=max(1, tm // FFN_UP_SUB_ROWS))
    cb = conv_b.reshape(1, 2 * f)
    act, zg, zu = pl.pallas_call(
        kern,
        grid=(nj, m // tm),
        in_specs=[
            pl.BlockSpec((tm, d), lambda j, i: (i, 0)),
            pl.BlockSpec((d, tn), lambda j, i: (0, j)),
            pl.BlockSpec((d, tn), lambda j, i: (0, j + nj)),
            pl.BlockSpec((CONV_WIDTH, tn), lambda j, i: (0, j)),
            pl.BlockSpec((CONV_WIDTH, tn), lambda j, i: (0, j + nj)),
            pl.BlockSpec((1, tn), lambda j, i: (0, j)),
            pl.BlockSpec((1, tn), lambda j, i: (0, j + nj)),
            pl.BlockSpec((n_state, 2, tn), state_map_g),
            pl.BlockSpec((n_state, 2, tn), state_map_u),
        ],
        out_specs=[
            pl.BlockSpec((tm, tn), lambda j, i: (i, j)),
            pl.BlockSpec((n_state, 2, tn), lambda j, i: (i, 0, j)),
            pl.BlockSpec((n_state, 2, tn), lambda j, i: (i, 0, j)),
        ],
        out_shape=[
            jax.ShapeDtypeStruct((m, f), BF16),
            jax.ShapeDtypeStruct((n_last, 2, f), F32),
            jax.ShapeDtypeStruct((n_last, 2, f), F32),
        ],
        scratch_shapes=[pltpu.VMEM((d, 2 * tn), BF16), pltpu.VMEM((tm + SUBLANE, 2 * tn), F32)],
        compiler_params=_params("arbitrary", "arbitrary", vmem_limit_bytes=FFN_UP_VMEM_LIMIT_BYTES),
        name="ffn_up",
    )(h, w_up, w_up, conv_w, conv_w, cb, cb, state, state)
    zlast = jnp.concatenate([zg, zu], axis=-1)
    if tiles_per_seq > 1:
        zlast = zlast[tiles_per_seq - 1::tiles_per_seq]
    return act, zlast


def _ffn_down_kernel(a_ref, b_ref, r_ref, o_ref):
    o_ref[...] = r_ref[...] + jnp.dot(a_ref[...], b_ref[...], preferred_element_type=F32)


def ffn_down(a, b, res):
    m, k = a.shape
    n = b.shape[1]
    tm = _tile(m, 512, SUBLANE)
    tn = _tile(n, 512, LANE)
    return pl.pallas_call(
        _ffn_down_kernel,
        grid=(m // tm, n // tn),
        in_specs=[
            pl.BlockSpec((tm, k), lambda i, j: (i, 0)),
            pl.BlockSpec((k, tn), lambda i, j: (0, j)),
            pl.BlockSpec((tm, tn), lambda i, j: (i, j)),
        ],
        out_specs=pl.BlockSpec((tm, tn), lambda i, j: (i, j)),
        out_shape=jax.ShapeDtypeStruct((m, n), F32),
        compiler_params=_params("parallel", "arbitrary"),
        name="ffn_down",
    )(a, b, res)


def _in_offsets(d):
    da = d // 2
    n_heads = d // 256
    n_kv = n_heads // Q_PER_KV
    sizes = (da, da, n_heads * HEAD_DIM, n_kv * HEAD_DIM, n_kv * HEAD_DIM,
             N_IDX_HEADS * IDX_DIM, IDX_DIM, N_IDX_HEADS, d, d)
    offs = [0]
    for s in sizes:
        offs.append(offs[-1] + s)
    return offs


def _layer(x, pos, seq_len, w, past, conv_state, want_vn):
    bsz, t, d = x.shape
    m = bsz * t
    x2 = x.reshape(m, d)
    n_heads = d // 256
    n_kv = n_heads // Q_PER_KV
    offs = _in_offsets(d)
    w_in = w["in_t"]
    assert offs[8] - offs[6] <= LANE and w["in_gates_t"].shape[0] == 2 * d

    h = rmsnorm(x2, w["norm_attn_g"], BF16)

    rows = max(t, _tile(m, 1024, SUBLANE))
    rope_h = (HEAD_DIM // 8,) + _rope_tables(pos, HEAD_DIM, HEAD_DIM // 4, rows)
    rope_i = (IDX_DIM // 8,) + _rope_tables(pos, IDX_DIM, IDX_DIM // 4, rows)
    ci, s1i, s2i = rope_i[1:]
    lane = jnp.arange(LANE)
    is_w = (lane >= IDX_DIM) & (lane < IDX_DIM + N_IDX_HEADS)
    w_scale = N_IDX_HEADS ** -0.5 * IDX_DIM ** -0.5
    rope_kw = (IDX_DIM // 8,
               jnp.where(is_w, w_scale, jnp.where(lane < IDX_DIM, ci, 1.0)).astype(F32),
               jnp.where(lane < IDX_DIM, s1i, 0.0), jnp.where(lane < IDX_DIM, s2i, 0.0))

    uv = project(h, w_in, offs[0], offs[2] - offs[0], F32)
    q = project(h, w_in, offs[2], offs[3] - offs[2], BF16, rope_h)
    qi = project(h, w_in, offs[5], offs[6] - offs[5], BF16, rope_i)
    k, v, kw = project_kv(h, w_in, offs[3], offs[4], offs[6], offs[4] - offs[3], rope_h, rope_kw)
    gates = project(h, w["in_gates_t"], 0, 2 * d, F32)

    chunk_rows = min(t, GMLP_CHUNK)
    a_out, vn = gmlp(uv, w["gmlp_norm_g"], w["gmlp_wm"](chunk_rows), w["gmlp_bias"](chunk_rows), want_vn)

    n_valid = t if past is None else past[1].shape[2] + t
    b_out = dsa(qi.reshape(bsz, t, -1), kw.reshape(bsz, t, LANE), q.reshape(bsz, t, -1),
                k.reshape(bsz, t * n_kv, HEAD_DIM), v.reshape(bsz, t * n_kv, HEAD_DIM), past,
                tq=min(DSA_QUERY_BLOCK, t), topk=min(TOPK_MAX, n_valid // 4))

    y = merge(a_out, b_out.reshape(m, n_heads * HEAD_DIM), w["a"], w["b"], gates)
    x2 = matmul_residual(y, w["o"], x2)

    hf = rmsnorm(x2, w["norm_ffn_g"], BF16)
    act, zlast = ffn_up(hf, w["up"], w["conv_w"], w["conv_b"], conv_state, seq_len)
    x2 = ffn_down(act, w["down"], x2)
    kidx = kw[:, :IDX_DIM].reshape(bsz, t, IDX_DIM)
    return (x2.reshape(bsz, t, d), k.reshape(bsz, t, n_kv, HEAD_DIM), v.reshape(bsz, t, n_kv, HEAD_DIM),
            kidx, zlast, vn)


def kernel(x_prompt, x_sample, cache_k, cache_v, cache_kidx, state_ffn_conv, norm_attn_g, w_in, gmlp_norm_g, gmlp_ws, gmlp_b, w_branch_a, w_branch_b, w_out, norm_ffn_g, w_up, conv_w, conv_b, w_down, norm_final_g):
    bsz, s, d = x_prompt.shape
    dbsz, t, _ = x_sample.shape
    depth = w_in.shape[0]
    p_len = cache_k.shape[2]
    da = d // 2
    f2 = w_up.shape[2]
    assert s % GMLP_CHUNK == 0 and GMLP_CHUNK % t == 0 and s % CHUNK == 0

    pos_p = jnp.arange(s, dtype=jnp.int32)
    pos_s = p_len + jnp.arange(t, dtype=jnp.int32)
    ci = jnp.arange(GMLP_CHUNK)
    chunk_mask = (ci[None, :] // CHUNK) <= (ci[:, None] // CHUNK)
    cache_k4 = cache_k.reshape(depth, dbsz, -1, HEAD_DIM)
    cache_v4 = cache_v.reshape(depth, dbsz, -1, HEAD_DIM)
    in_offs = _in_offsets(d)
    assert in_offs[-1] == w_in.shape[2]

    xp, xs = x_prompt, x_sample
    outs = [[] for _ in range(9)]
    for l in range(depth):
        wm_full = jnp.where(chunk_mask[None], gmlp_ws[l], 0.0)
        bias_rows = jnp.repeat(jnp.transpose(gmlp_b[l]), da // G_A, axis=1)

        def gmlp_wm(rows, wm_full=wm_full):
            reps = GMLP_CHUNK // rows
            blk = wm_full[:, :rows, :rows]
            eye = jnp.eye(reps, dtype=F32)
            return jnp.einsum("ab,gij->gaibj", eye, blk).reshape(G_A, GMLP_CHUNK, GMLP_CHUNK).astype(BF16)

        def gmlp_bias(rows, bias_rows=bias_rows):
            return jnp.tile(bias_rows[:rows], (GMLP_CHUNK // rows, 1))

        w = dict(
            norm_attn_g=norm_attn_g[l], gmlp_norm_g=gmlp_norm_g[l], norm_ffn_g=norm_ffn_g[l],
            gmlp_wm=gmlp_wm, gmlp_bias=gmlp_bias,
            a=w_branch_a[l], b=w_branch_b[l], up=w_up[l], conv_w=conv_w[l], conv_b=conv_b[l],
            o=w_out[l].astype(BF16), down=w_down[l].astype(BF16),
        )
        w["in_t"] = jnp.swapaxes(w_in[l], 0, 1).astype(BF16)
        w["in_gates_t"] = w["in_t"][in_offs[8]:]

        xp, kp, vp, kip, cp, _ = _layer(xp, pos_p, s, w, None, jnp.zeros((bsz, CONV_WIDTH - 1, f2), F32),
                                        False)
        past = (l, cache_kidx, cache_k4, cache_v4)
        xs, ks, vs, kis, cs, gv = _layer(xs, pos_s, t, w, past, state_ffn_conv[l], True)
        for lst, val in zip(outs, (kp, vp, kip, cp, ks, vs, kis, cs, gv.reshape(dbsz, t, da))):
            lst.append(val)

    y_prompt = rmsnorm(xp.reshape(bsz * s, d), norm_final_g, F32).reshape(bsz, s, d)
    y_sample = rmsnorm(xs.reshape(dbsz * t, d), norm_final_g, F32).reshape(dbsz, t, d)
    stack = lambda o: o[0][None] if depth == 1 else jnp.stack(o)
    return (y_prompt, y_sample) + tuple(stack(o) for o in outs)
```
